```python
import math
import jax, jax.numpy as jnp
from jax import lax
import numpy as np

D_MODEL = 2048
BATCH = 8
SEQ = 2048
DEPTH = 4

N_A_LAYERS = DEPTH // 2
N_B_LAYERS = DEPTH - N_A_LAYERS
SB_HEADS = 16
SB_HEAD_DIM = D_MODEL // SB_HEADS
DIFF_HEADS = 16
DIFF_QK_DIM = D_MODEL // DIFF_HEADS // 2
DIFF_V_DIM = 2 * DIFF_QK_DIM
ROPE_THETA = 10000.0
N_GROUPS = 4
EXPERTS_PER_GROUP = 4
N_EXPERTS = N_GROUPS * EXPERTS_PER_GROUP
TOP_K_INNER = 2
D_EXPERT = D_MODEL // 2
Q_BLOCK = 128
DEEPNORM_ALPHA = (2.0 * DEPTH) ** 0.25
DEEPNORM_BETA = (8.0 * DEPTH) ** -0.25
LN_EPS = 1e-5
SUBLN_EPS = 1e-5

kernel_name = "yoco_stickbreak_diffattn_hmoe"


def layer_norm(x, g, b):
    xf = x.astype(jnp.float32)
    mu = jnp.mean(xf, axis=-1, keepdims=True)
    var = jnp.mean(jnp.square(xf - mu), axis=-1, keepdims=True)
    y = (xf - mu) * lax.rsqrt(var + LN_EPS) * g.astype(jnp.float32) + b.astype(jnp.float32)
    return y.astype(x.dtype)


def rope(x, pos):
    d = x.shape[-1]
    half = d // 2
    inv_freq = ROPE_THETA ** (-jnp.arange(half, dtype=jnp.float32) / half)
    ang = pos.astype(jnp.float32)[:, None] * inv_freq[None, :]
    cos, sin = jnp.cos(ang), jnp.sin(ang)
    xf = x.astype(jnp.float32)
    x1, x2 = xf[..., :half], xf[..., half:]
    out = jnp.concatenate([x1 * cos - x2 * sin, x2 * cos + x1 * sin], axis=-1)
    return out.astype(x.dtype)


def to_blocks(t):
    B, H, S, d = t.shape
    return t.reshape(B, H, S // Q_BLOCK, Q_BLOCK, d).transpose(2, 0, 1, 3, 4)


def from_blocks(t):
    nb, B, H, qb, d = t.shape
    return t.transpose(1, 2, 0, 3, 4).reshape(B, H, nb * qb, d)


def stick_breaking_attention(q, k, v):
    S = q.shape[2]
    scale = q.shape[-1] ** -0.5
    kf = k.astype(jnp.float32)
    vf = v.astype(jnp.float32)
    key_pos = jnp.arange(S)

    def block(args):
        qb, start = args
        q_pos = start + jnp.arange(Q_BLOCK)
        z = jnp.einsum('bhqd,bhkd->bhqk', qb.astype(jnp.float32), kf) * scale
        past = key_pos[None, :] < q_pos[:, None]
        log_beta = jax.nn.log_sigmoid(z)
        log_keep = jnp.where(past, jax.nn.log_sigmoid(-z), 0.0)
        tail = lax.cumsum(log_keep, axis=3, reverse=True) - log_keep
        w = jnp.where(past, jnp.exp(log_beta + tail), 0.0)
        return jnp.einsum('bhqk,bhkd->bhqd', w, vf)

    starts = jnp.arange(S // Q_BLOCK, dtype=jnp.int32) * Q_BLOCK
    o = lax.map(block, (to_blocks(q), starts))
    return from_blocks(o).astype(v.dtype)


def differential_attention(q1, q2, k1, k2, v, lam):
    S = q1.shape[2]
    scale = q1.shape[-1] ** -0.5
    k1f, k2f, vf = k1.astype(jnp.float32), k2.astype(jnp.float32), v.astype(jnp.float32)
    key_pos = jnp.arange(S)

    def block(args):
        qb1, qb2, start = args
        q_pos = start + jnp.arange(Q_BLOCK)
        causal = key_pos[None, :] <= q_pos[:, None]

        def probs(qb, kf):
            s = jnp.einsum('bhqd,bhkd->bhqk', qb.astype(jnp.float32), kf) * scale
            return jax.nn.softmax(jnp.where(causal, s, -jnp.inf), axis=-1)

        a = probs(qb1, k1f) - lam * probs(qb2, k2f)
        return jnp.einsum('bhqk,bhkd->bhqd', a, vf)

    starts = jnp.arange(S // Q_BLOCK, dtype=jnp.int32) * Q_BLOCK
    o = lax.map(block, (to_blocks(q1), to_blocks(q2), starts))
    return from_blocks(o)


def stick_breaking_mixer(h, w_qkv, w_o):
    B, S, D = h.shape
    qkv = jnp.einsum('bsd,de->bse', h, w_qkv).reshape(B, S, 3, SB_HEADS, SB_HEAD_DIM)
    qkv = qkv.transpose(2, 0, 3, 1, 4)
    o = stick_breaking_attention(qkv[0], qkv[1], qkv[2])
    o = o.transpose(0, 2, 1, 3).reshape(B, S, SB_HEADS * SB_HEAD_DIM)
    return jnp.einsum('bse,ed->bsd', o, w_o)


def shared_kv(h, w_kv, pos):
    B, S, D = h.shape
    kv = jnp.einsum('bsd,de->bse', h, w_kv)
    kd = DIFF_HEADS * 2 * DIFF_QK_DIM
    k = kv[..., :kd].reshape(B, S, DIFF_HEADS, 2, DIFF_QK_DIM).transpose(3, 0, 2, 1, 4)
    v = kv[..., kd:].reshape(B, S, DIFF_HEADS, DIFF_V_DIM).transpose(0, 2, 1, 3)
    return rope(k[0], pos), rope(k[1], pos), v


def diff_mixer(h, k1, k2, v, w_q, w_o, lam_params, subln_g, layer_idx, pos):
    B, S, D = h.shape
    q = jnp.einsum('bsd,de->bse', h, w_q).reshape(B, S, DIFF_HEADS, 2, DIFF_QK_DIM)
    q = q.transpose(3, 0, 2, 1, 4)
    q1, q2 = rope(q[0], pos), rope(q[1], pos)
    lam_init = 0.8 - 0.6 * math.exp(-0.3 * layer_idx)
    lp = lam_params.astype(jnp.float32)
    lam = jnp.exp(jnp.sum(lp[0] * lp[1])) - jnp.exp(jnp.sum(lp[2] * lp[3])) + lam_init
    o = differential_attention(q1, q2, k1, k2, v, lam)
    o = o * lax.rsqrt(jnp.mean(jnp.square(o), axis=-1, keepdims=True) + SUBLN_EPS)
    o = o * subln_g.astype(jnp.float32) * (1.0 - lam_init)
    o = o.transpose(0, 2, 1, 3).reshape(B, S, DIFF_HEADS * DIFF_V_DIM).astype(h.dtype)
    return jnp.einsum('bse,ed->bsd', o, w_o)


def hierarchical_moe(h, w_group, w_inner, w_gate, w_up, w_down):
    B, S, D = h.shape
    t = h.reshape(B * S, D)
    group_logits = jnp.einsum('nd,dg->ng', t, w_group).astype(jnp.float32)
    group_prob = jax.nn.softmax(group_logits, axis=-1)
    g_sel = jnp.argmax(group_logits, axis=-1)
    g_w = jnp.take_along_axis(group_prob, g_sel[:, None], axis=1)
    inner_logits = jnp.einsum('nd,gde->nge', t, w_inner).astype(jnp.float32)
    inner_sel = jnp.take_along_axis(inner_logits, g_sel[:, None, None], axis=1)[:, 0]
    top_v, top_i = lax.top_k(inner_sel, TOP_K_INNER)
    top_w = jax.nn.softmax(top_v, axis=-1) * g_w
    expert_id = g_sel[:, None] * EXPERTS_PER_GROUP + top_i
    gates = jnp.sum(jax.nn.one_hot(expert_id, N_EXPERTS, dtype=jnp.float32) * top_w[..., None], axis=1)
    y = jnp.zeros((B * S, D), jnp.float32)
    for e in range(N_EXPERTS):
        a = jax.nn.silu(t @ w_gate[e]) * (t @ w_up[e])
        y = y + gates[:, e:e + 1] * (a @ w_down[e]).astype(jnp.float32)
    return y.reshape(B, S, D).astype(h.dtype)


def setup_inputs(seed: int = 0) -> dict:
    key = jax.random.key(seed)
    ks = jax.random.split(key, 20)
    D = D_MODEL
    f32 = jnp.float32

    def nrm(k, shape, scale):
        return jax.random.normal(k, shape, f32) * scale

    x = nrm(ks[0], (BATCH, SEQ, D), 1.0)
    sb_col = jnp.concatenate([jnp.ones((2 * D,), f32), jnp.full((D,), DEEPNORM_BETA, f32)]) * D ** -0.5
    sb_w_qkv = nrm(ks[1], (N_A_LAYERS, D, 3 * D), 1.0) * sb_col
    sb_w_o = nrm(ks[2], (N_A_LAYERS, D, D), D ** -0.5 * DEEPNORM_BETA)
    kd = DIFF_HEADS * 2 * DIFF_QK_DIM
    vd = DIFF_HEADS * DIFF_V_DIM
    kv_col = jnp.concatenate([jnp.ones((kd,), f32), jnp.full((vd,), DEEPNORM_BETA, f32)]) * D ** -0.5
    shared_w_kv = nrm(ks[3], (D, kd + vd), 1.0) * kv_col
    diff_w_q = nrm(ks[4], (N_B_LAYERS, D, kd), D ** -0.5)
    diff_w_o = nrm(ks[5], (N_B_LAYERS, vd, D), vd ** -0.5 * DEEPNORM_BETA)
    diff_lambda = nrm(ks[6], (N_B_LAYERS, 4, DIFF_QK_DIM), 0.1)
    diff_subln_g = 1.0 + nrm(ks[7], (N_B_LAYERS, DIFF_V_DIM), 0.01)
    ln_mix_g = 1.0 + nrm(ks[8], (DEPTH, D), 0.01)
    ln_mix_b = nrm(ks[9], (DEPTH, D), 0.01)
    ln_ffn_g = 1.0 + nrm(ks[10], (DEPTH, D), 0.01)
    ln_ffn_b = nrm(ks[11], (DEPTH, D), 0.01)
    moe_w_group = nrm(ks[12], (DEPTH, D, N_GROUPS), D ** -0.5)
    moe_w_inner = nrm(ks[13], (DEPTH, N_GROUPS, D, EXPERTS_PER_GROUP), D ** -0.5)
    moe_w_gate = nrm(ks[14], (DEPTH, N_EXPERTS, D, D_EXPERT), D ** -0.5)
    moe_w_up = nrm(ks[15], (DEPTH, N_EXPERTS, D, D_EXPERT), D ** -0.5)
    moe_w_down = nrm(ks[16], (DEPTH, N_EXPERTS, D_EXPERT, D), D_EXPERT ** -0.5 * DEEPNORM_BETA)
    return {"x": x, "sb_w_qkv": sb_w_qkv, "sb_w_o": sb_w_o, "shared_w_kv": shared_w_kv,
            "diff_w_q": diff_w_q, "diff_w_o": diff_w_o, "diff_lambda": diff_lambda,
            "diff_subln_g": diff_subln_g, "ln_mix_g": ln_mix_g, "ln_mix_b": ln_mix_b,
            "ln_ffn_g": ln_ffn_g, "ln_ffn_b": ln_ffn_b, "moe_w_group": moe_w_group,
            "moe_w_inner": moe_w_inner, "moe_w_gate": moe_w_gate, "moe_w_up": moe_w_up,
            "moe_w_down": moe_w_down}


def reference(x, sb_w_qkv, sb_w_o, shared_w_kv, diff_w_q, diff_w_o, diff_lambda, diff_subln_g,
              ln_mix_g, ln_mix_b, ln_ffn_g, ln_ffn_b, moe_w_group, moe_w_inner, moe_w_gate,
              moe_w_up, moe_w_down):
    S = x.shape[1]
    pos = jnp.arange(S, dtype=jnp.int32)
    h = x
    k1 = k2 = v = None
    for layer in range(DEPTH):
        if layer < N_A_LAYERS:
            mix = stick_breaking_mixer(h, sb_w_qkv[layer], sb_w_o[layer])
        else:
            if layer == N_A_LAYERS:
                k1, k2, v = shared_kv(h, shared_w_kv, pos)
            i = layer - N_A_LAYERS
            mix = diff_mixer(h, k1, k2, v, diff_w_q[i], diff_w_o[i], diff_lambda[i],
                             diff_subln_g[i], layer, pos)
        h = layer_norm(DEEPNORM_ALPHA * h + mix, ln_mix_g[layer], ln_mix_b[layer])
        ffn = hierarchical_moe(h, moe_w_group[layer], moe_w_inner[layer], moe_w_gate[layer],
                               moe_w_up[layer], moe_w_down[layer])
        h = layer_norm(DEEPNORM_ALPHA * h + ffn, ln_ffn_g[layer], ln_ffn_b[layer])
    return h
```

```python
import functools
import math

import jax
import jax.numpy as jnp
from jax import lax
from jax.experimental import pallas as pl
from jax.experimental.pallas import tpu as pltpu

HEAD_DIM = 128
DIFF_QK_DIM = HEAD_DIM // 2
N_GROUPS = 4
EXPERTS_PER_GROUP = 4
N_EXPERTS = N_GROUPS * EXPERTS_PER_GROUP
ROPE_THETA = 10000.0
LN_EPS = 1e-5
SUBLN_EPS = 1e-5
LANES = 128
ROUTE_LANE0 = N_GROUPS
NEG_BIG = -1e30
VMEM_LIMIT = 56 * 1024 * 1024

F32 = jnp.float32
BF16 = jnp.bfloat16


def _tile(n, pref):
    t = min(n, pref)
    assert n % t == 0, (n, pref)
    return t


def _params(sem):
    return pltpu.CompilerParams(dimension_semantics=sem, vmem_limit_bytes=VMEM_LIMIT)


def _dot(a, b):
    return jnp.dot(a, b, preferred_element_type=F32)


def _dot_nt(a, b):
    return lax.dot_general(a, b, (((1,), (1,)), ((), ())), preferred_element_type=F32)


def _rope_tile(acc, cos, sin):
    lane = lax.broadcasted_iota(jnp.int32, cos.shape, 1)
    first_half = (lane % DIFF_QK_DIM) < (DIFF_QK_DIM // 2)
    outs = []
    for g in range(acc.shape[1] // LANES):
        xg = acc[:, g * LANES:(g + 1) * LANES]
        up = pltpu.roll(xg, LANES - DIFF_QK_DIM // 2, axis=1)
        dn = pltpu.roll(xg, DIFF_QK_DIM // 2, axis=1)
        partner = jnp.where(first_half, up, dn)
        outs.append(xg * cos + partner * sin)
    return jnp.concatenate(outs, axis=1)


def _proj_kernel(x_ref, w_ref, *rest, scale, scale_tiles, rope_tiles):
    if rope_tiles:
        cos_ref, sin_ref, o_ref = rest
    else:
        (o_ref,) = rest
    j = pl.program_id(0)
    acc = _dot(x_ref[...], w_ref[...])
    if scale_tiles:
        acc = acc * jnp.where(j < scale_tiles, jnp.float32(scale), jnp.float32(1.0))
    if rope_tiles:
        @pl.when(j < rope_tiles)
        def _():
            o_ref[...] = _rope_tile(acc, cos_ref[...], sin_ref[...]).astype(o_ref.dtype)

        @pl.when(j >= rope_tiles)
        def _():
            o_ref[...] = acc.astype(o_ref.dtype)
    else:
        o_ref[...] = acc.astype(o_ref.dtype)


def _proj(x, w, *, scale=1.0, scale_cols=0, rope_cols=0, rope_tabs=None, seq=None):
    n, k = x.shape
    m = w.shape[1]
    tm = _tile(n, 512)
    tn = math.gcd(1024, m, scale_cols, rope_cols)
    assert tn % LANES == 0
    in_specs = [pl.BlockSpec((tm, k), lambda j, i: (i, 0)),
                pl.BlockSpec((k, tn), lambda j, i: (0, j))]
    args = [x, w]
    if rope_cols:
        assert seq % tm == 0
        spb = seq // tm
        in_specs += [pl.BlockSpec((tm, LANES), lambda j, i: (i % spb, 0))] * 2
        args += list(rope_tabs)
    kern = functools.partial(_proj_kernel, scale=scale, scale_tiles=scale_cols // tn,
                             rope_tiles=rope_cols // tn)
    return pl.pallas_call(
        kern,
        out_shape=jax.ShapeDtypeStruct((n, m), BF16),
        grid=(m // tn, n // tm),
        in_specs=in_specs,
        out_specs=pl.BlockSpec((tm, tn), lambda j, i: (i, j)),
        compiler_params=_params(("arbitrary", "arbitrary")),
        name="proj",
    )(*args)


def _layer_norm_rows(v, g, b):
    mu = jnp.mean(v, axis=-1, keepdims=True)
    c = v - mu
    var = jnp.mean(c * c, axis=-1, keepdims=True)
    return c * lax.rsqrt(var + LN_EPS) * g + b


def _proj_ln_kernel(x_ref, w_ref, res_ref, g_ref, b_ref, o_ref, ob_ref, *, alpha):
    acc = _dot(x_ref[...], w_ref[...])
    y = _layer_norm_rows(alpha * res_ref[...] + acc, g_ref[...], b_ref[...])
    o_ref[...] = y
    ob_ref[...] = y.astype(BF16)


def _proj_ln(x, w, res, g, b, alpha):
    n, k = x.shape
    d = w.shape[1]
    tm = _tile(n, 256)
    return pl.pallas_call(
        functools.partial(_proj_ln_kernel, alpha=alpha),
        out_shape=(jax.ShapeDtypeStruct((n, d), F32), jax.ShapeDtypeStruct((n, d), BF16)),
        grid=(n // tm,),
        in_specs=[pl.BlockSpec((tm, k), lambda i: (i, 0)),
                  pl.BlockSpec((k, d), lambda i: (0, 0)),
                  pl.BlockSpec((tm, d), lambda i: (i, 0)),
                  pl.BlockSpec((1, d), lambda i: (0, 0)),
                  pl.BlockSpec((1, d), lambda i: (0, 0))],
        out_specs=(pl.BlockSpec((tm, d), lambda i: (i, 0)),
                   pl.BlockSpec((tm, d), lambda i: (i, 0))),
        compiler_params=_params(("arbitrary",)),
        name="proj_ln",
    )(x, w, res, g.reshape(1, d), b.reshape(1, d))


def _sb_attn_kernel(q_ref, k_ref, v_ref, o_ref, *, tq, tk):
    qi = pl.program_id(2)
    q = q_ref[...]
    row = lax.broadcasted_iota(jnp.int32, (2 * tk, tk), 0) % tk
    col = lax.broadcasted_iota(jnp.int32, (2 * tk, tk), 1)
    later = (row > col).astype(BF16)
    q_pos = qi * tq + lax.broadcasted_iota(jnp.int32, (tq, tk), 0)
    k_off = lax.broadcasted_iota(jnp.int32, (tq, tk), 1)

    def step(kc, carry, masked):
        acc, tail_after = carry
        start = pl.multiple_of(kc * tk, tk)
        k = k_ref[pl.ds(start, tk), :]
        v = v_ref[pl.ds(start, tk), :]
        z = _dot_nt(q, k)
        nl = -jnp.log(1.0 + jnp.exp(-jnp.abs(z)))
        log_beta = nl + jnp.minimum(z, 0.0)
        log_keep = nl - jnp.maximum(z, 0.0)
        if masked:
            past = (start + k_off) < q_pos
            log_keep = jnp.where(past, log_keep, 0.0)
        hi = log_keep.astype(BF16)
        lo = (log_keep - hi.astype(F32)).astype(BF16)
        tail_in = _dot(jnp.concatenate([hi, lo], axis=1), later)
        w = jnp.exp(log_beta + tail_in + tail_after)
        if masked:
            w = jnp.where(past, w, 0.0)
        acc = acc + _dot(w.astype(BF16), v)
        tail_after = tail_after + jnp.sum(log_keep, axis=1, keepdims=True)
        return acc, tail_after

    carry = (jnp.zeros((tq, HEAD_DIM), F32), jnp.zeros((tq, 1), F32))
    n_diag = tq // tk
    first_rest = qi * n_diag
    for d in reversed(range(n_diag)):
        carry = step(first_rest + d, carry, True)
    carry = lax.fori_loop(0, first_rest, lambda t, c: step(first_rest - 1 - t, c, False), carry)
    o_ref[...] = carry[0].astype(o_ref.dtype)


def _sb_attention(qkv, batch, seq):
    n, three_d = qkv.shape
    d = three_d // 3
    heads = d // HEAD_DIM
    tq = _tile(seq, 256)
    tk = _tile(tq, 128)
    nq = seq // tq
    return pl.pallas_call(
        functools.partial(_sb_attn_kernel, tq=tq, tk=tk),
        out_shape=jax.ShapeDtypeStruct((n, d), BF16),
        grid=(batch, heads, nq),
        in_specs=[pl.BlockSpec((tq, HEAD_DIM), lambda b, h, i: (b * nq + i, h)),
                  pl.BlockSpec((seq, HEAD_DIM), lambda b, h, i: (b, heads + h)),
                  pl.BlockSpec((seq, HEAD_DIM), lambda b, h, i: (b, 2 * heads + h))],
        out_specs=pl.BlockSpec((tq, HEAD_DIM), lambda b, h, i: (b * nq + i, h)),
        compiler_params=_params(("arbitrary", "arbitrary", "arbitrary")),
        name="sb_attn",
    )(qkv, qkv, qkv)


def _diff_attn_kernel(q_ref, k_ref, v_ref, lam_ref, g_ref, o_ref, *, t, lam_init):
    qi = pl.program_id(2)
    q = q_ref[...]
    lane = lax.broadcasted_iota(jnp.int32, (t, HEAD_DIM), 1)
    zero = jnp.zeros_like(q)
    q1 = jnp.where(lane < DIFF_QK_DIM, q, zero)
    q2 = jnp.where(lane >= DIFF_QK_DIM, q, zero)
    q_pos = lax.broadcasted_iota(jnp.int32, (t, t), 0)
    k_pos = lax.broadcasted_iota(jnp.int32, (t, t), 1)
    causal = k_pos <= q_pos

    def stream(qx, k, v, state, masked):
        m, l, acc = state
        s = _dot_nt(qx, k)
        if masked:
            s = jnp.where(causal, s, NEG_BIG)
        m_new = jnp.maximum(m, jnp.max(s, axis=1, keepdims=True))
        p = jnp.exp(s - m_new)
        corr = jnp.exp(m - m_new)
        l = corr * l + jnp.sum(p, axis=1, keepdims=True)
        acc = corr * acc + _dot(p.astype(BF16), v)
        return m_new, l, acc

    def step(kc, carry, masked):
        start = pl.multiple_of(kc * t, t)
        k = k_ref[pl.ds(start, t), :]
        v = v_ref[pl.ds(start, t), :]
        return (stream(q1, k, v, carry[0], masked), stream(q2, k, v, carry[1], masked))

    init = (jnp.full((t, 1), NEG_BIG, F32), jnp.zeros((t, 1), F32), jnp.zeros((t, HEAD_DIM), F32))
    carry = step(qi, (init, init), True)
    carry = lax.fori_loop(0, qi, lambda c, s: step(c, s, False), carry)
    (_, l1, a1), (_, l2, a2) = carry

    lp = lam_ref[...]
    lam = (jnp.exp(jnp.sum(lp[0:1] * lp[1:2], axis=1, keepdims=True))
           - jnp.exp(jnp.sum(lp[2:3] * lp[3:4], axis=1, keepdims=True)) + lam_init)
    o = a1 / l1 - lam * (a2 / l2)
    o = o * lax.rsqrt(jnp.mean(o * o, axis=1, keepdims=True) + SUBLN_EPS)
    o_ref[...] = (o * g_ref[...] * (1.0 - lam_init)).astype(o_ref.dtype)


def _diff_attention(qd, kv, lam_params, subln_g, lam_init, batch, seq):
    n, d = qd.shape
    heads = d // HEAD_DIM
    t = _tile(seq, 512)
    nq = seq // t
    return pl.pallas_call(
        functools.partial(_diff_attn_kernel, t=t, lam_init=lam_init),
        out_shape=jax.ShapeDtypeStruct((n, d), BF16),
        grid=(batch, heads, nq),
        in_specs=[pl.BlockSpec((t, HEAD_DIM), lambda b, h, i: (b * nq + i, h)),
                  pl.BlockSpec((seq, HEAD_DIM), lambda b, h, i: (b, h)),
                  pl.BlockSpec((seq, HEAD_DIM), lambda b, h, i: (b, heads + h)),
                  pl.BlockSpec(lam_params.shape, lambda b, h, i: (0, 0)),
                  pl.BlockSpec((1, HEAD_DIM), lambda b, h, i: (0, 0))],
        out_specs=pl.BlockSpec((t, HEAD_DIM), lambda b, h, i: (b * nq + i, h)),
        compiler_params=_params(("arbitrary", "arbitrary", "arbitrary")),
        name="diff_attn",
    )(qd, kv, kv, lam_params, subln_g.reshape(1, HEAD_DIM))


def _split_bf16(x):
    hi = x.astype(BF16)
    return hi, (x - hi.astype(F32)).astype(BF16)


def _route_kernel(h_ref, w_ref, info_ref, cnt_ref, *, tm):
    i = pl.program_id(0)

    @pl.when(i == 0)
    def _():
        cnt_ref[...] = jnp.zeros_like(cnt_ref)

    h_hi, h_lo = _split_bf16(h_ref[...])
    w_hi, w_lo = _split_bf16(w_ref[...])
    logits = _dot(h_hi, w_hi) + (_dot(h_hi, w_lo) + _dot(h_lo, w_hi))

    lane = lax.broadcasted_iota(jnp.int32, (tm, LANES), 1).astype(F32)

    def first_max(mask, vals):
        top = jnp.max(jnp.where(mask, vals, NEG_BIG), axis=1, keepdims=True)
        idx = jnp.min(jnp.where(mask & (vals == top), lane, float(LANES)), axis=1, keepdims=True)
        return top, idx

    g_mask = lane < N_GROUPS
    g_top, g_sel = first_max(g_mask, logits)
    g_den = jnp.sum(jnp.where(g_mask, jnp.exp(logits - g_top), 0.0), axis=1, keepdims=True)
    g_w = 1.0 / g_den

    lo_lane = ROUTE_LANE0 + EXPERTS_PER_GROUP * g_sel
    in_group = (lane >= lo_lane) & (lane < lo_lane + EXPERTS_PER_GROUP)
    v1, i1 = first_max(in_group, logits)
    v2, i2 = first_max(in_group & (lane != i1), logits)
    ex = jnp.exp(v2 - v1)
    w1 = g_w / (1.0 + ex)
    w2 = w1 * ex

    m1 = (lane == i1).astype(F32)
    m2 = (lane == i2).astype(F32)
    both = m1 + m2
    r = lax.broadcasted_iota(jnp.int32, (tm, tm), 0)
    c = lax.broadcasted_iota(jnp.int32, (tm, tm), 1)
    earlier = (c < r).astype(BF16)
    before = _dot(earlier, both.astype(BF16)) + cnt_ref[0:1, :]
    rank1 = jnp.sum(m1 * before, axis=1, keepdims=True)
    rank2 = jnp.sum(m2 * before, axis=1, keepdims=True)
    cnt_ref[...] = cnt_ref[...] + jnp.sum(both, axis=0, keepdims=True)

    info = jnp.where(lane == 0, i1 - ROUTE_LANE0, 0.0)
    info = jnp.where(lane == 1, i2 - ROUTE_LANE0, info)
    info = jnp.where(lane == 2, rank1, info)
    info = jnp.where(lane == 3, rank2, info)
    info = jnp.where(lane == 4, w1, info)
    info = jnp.where(lane == 5, w2, info)
    info_ref[...] = info


def _route(h, w_route):
    n, d = h.shape
    tm = _tile(n, 512)
    return pl.pallas_call(
        functools.partial(_route_kernel, tm=tm),
        out_shape=(jax.ShapeDtypeStruct((n, LANES), F32), jax.ShapeDtypeStruct((8, LANES), F32)),
        grid=(n // tm,),
        in_specs=[pl.BlockSpec((tm, d), lambda i: (i, 0)),
                  pl.BlockSpec((d, LANES), lambda i: (0, 0))],
        out_specs=(pl.BlockSpec((tm, LANES), lambda i: (i, 0)),
                   pl.BlockSpec((8, LANES), lambda i: (0, 0))),
        compiler_params=_params(("arbitrary",)),
        name="route",
    )(h, w_route)


def _dispatch_kernel(slot_ref, h_hbm, xs_in_hbm, xs_hbm, sem, *, tm):
    del xs_in_hbm
    base = pl.program_id(0) * tm

    def row_copy(t, s):
        return pltpu.make_async_copy(h_hbm.at[pl.ds(t, 1), :], xs_hbm.at[pl.ds(s, 1), :], sem)

    def issue(r, _):
        t = base + r
        row_copy(t, slot_ref[2 * t]).start()
        row_copy(t, slot_ref[2 * t + 1]).start()
        return 0

    def drain(r, _):
        row_copy(0, 0).wait()
        row_copy(0, 0).wait()
        return 0

    lax.fori_loop(0, tm, issue, 0)
    lax.fori_loop(0, tm, drain, 0)


def _dispatch(h, slots, n_rows):
    n, d = h.shape
    tm = _tile(n, 512)
    return pl.pallas_call(
        functools.partial(_dispatch_kernel, tm=tm),
        out_shape=jax.ShapeDtypeStruct((n_rows, d), h.dtype),
        grid_spec=pltpu.PrefetchScalarGridSpec(
            num_scalar_prefetch=1,
            grid=(n // tm,),
            in_specs=[pl.BlockSpec(memory_space=pl.ANY), pl.BlockSpec(memory_space=pl.ANY)],
            out_specs=pl.BlockSpec(memory_space=pl.ANY),
            scratch_shapes=[pltpu.SemaphoreType.DMA(())],
        ),
        input_output_aliases={2: 0},
        compiler_params=_params(("arbitrary",)),
        name="moe_dispatch",
    )(slots, h, jnp.zeros((n_rows, d), h.dtype))


def _moe_up_kernel(te_ref, fresh_ref, valid_ref, x_ref, wg_ref, wu_ref, o_ref, wg_bf, wu_bf):
    i = pl.program_id(1)

    @pl.when(fresh_ref[i] == 1)
    def _():
        wg_bf[...] = wg_ref[...].astype(BF16)
        wu_bf[...] = wu_ref[...].astype(BF16)

    @pl.when(valid_ref[i] == 1)
    def _():
        x = x_ref[...].astype(BF16)
        g = _dot(x, wg_bf[...])
        u = _dot(x, wu_bf[...])
        o_ref[...] = (g * (1.0 / (1.0 + jnp.exp(-g))) * u).astype(o_ref.dtype)

    @pl.when(valid_ref[i] == 0)
    def _():
        o_ref[...] = jnp.zeros_like(o_ref)


def _moe_up(xs, w_gate, w_up, tile_expert, fresh, valid, tm):
    p, d = xs.shape
    f = w_gate.shape[2]
    tn = _tile(f, 512)
    return pl.pallas_call(
        _moe_up_kernel,
        out_shape=jax.ShapeDtypeStruct((p, f), BF16),
        grid_spec=pltpu.PrefetchScalarGridSpec(
            num_scalar_prefetch=3,
            grid=(f // tn, p // tm),
            in_specs=[pl.BlockSpec((tm, d), lambda j, i, te, fr, va: (i, 0)),
                      pl.BlockSpec((None, d, tn), lambda j, i, te, fr, va: (te[i], 0, j)),
                      pl.BlockSpec((None, d, tn), lambda j, i, te, fr, va: (te[i], 0, j))],
            out_specs=pl.BlockSpec((tm, tn), lambda j, i, te, fr, va: (i, j)),
            scratch_shapes=[pltpu.VMEM((d, tn), BF16), pltpu.VMEM((d, tn), BF16)],
        ),
        compiler_params=_params(("arbitrary", "arbitrary")),
        name="moe_up",
    )(tile_expert, fresh, valid, xs, w_gate, w_up)


def _moe_down_kernel(te_ref, fresh_ref, valid_ref, a_ref, wd_ref, o_ref, wd_bf):
    i = pl.program_id(1)

    @pl.when(fresh_ref[i] == 1)
    def _():
        wd_bf[...] = wd_ref[...].astype(BF16)

    @pl.when(valid_ref[i] == 1)
    def _():
        o_ref[...] = _dot(a_ref[...], wd_bf[...])

    @pl.when(valid_ref[i] == 0)
    def _():
        o_ref[...] = jnp.zeros_like(o_ref)


def _moe_down(act, w_down, tile_expert, fresh, valid, tm):
    p, f = act.shape
    d = w_down.shape[2]
    tn = _tile(d, 1024)
    return pl.pallas_call(
        _moe_down_kernel,
        out_shape=jax.ShapeDtypeStruct((p, d), F32),
        grid_spec=pltpu.PrefetchScalarGridSpec(
            num_scalar_prefetch=3,
            grid=(d // tn, p // tm),
            in_specs=[pl.BlockSpec((tm, f), lambda j, i, te, fr, va: (i, 0)),
                      pl.BlockSpec((None, f, tn), lambda j, i, te, fr, va: (te[i], 0, j))],
            out_specs=pl.BlockSpec((tm, tn), lambda j, i, te, fr, va: (i, j)),
            scratch_shapes=[pltpu.VMEM((f, tn), BF16)],
        ),
        compiler_params=_params(("arbitrary", "arbitrary")),
        name="moe_down",
    )(tile_expert, fresh, valid, act, w_down)


def _combine_ln_kernel(slot_ref, h_ref, info_ref, g_ref, b_ref, ys_hbm, o_ref, ob_ref, buf, sem, *, tm, alpha):
    base = pl.program_id(0) * tm

    def row_copy(s, k, r):
        return pltpu.make_async_copy(ys_hbm.at[pl.ds(s, 1), :], buf.at[k, pl.ds(r, 1), :], sem)

    def issue(r, _):
        t = base + r
        row_copy(slot_ref[2 * t], 0, r).start()
        row_copy(slot_ref[2 * t + 1], 1, r).start()
        return 0

    def drain(r, _):
        row_copy(0, 0, 0).wait()
        row_copy(0, 0, 0).wait()
        return 0

    lax.fori_loop(0, tm, issue, 0)
    lax.fori_loop(0, tm, drain, 0)

    info = info_ref[...]
    ffn = info[:, 4:5] * buf[0] + info[:, 5:6] * buf[1]
    y = _layer_norm_rows(alpha * h_ref[...] + ffn, g_ref[...], b_ref[...])
    o_ref[...] = y
    ob_ref[...] = y.astype(BF16)


def _combine_ln(h, info, ys, slots, g, b, alpha):
    n, d = h.shape
    tm = _tile(n, 256)
    return pl.pallas_call(
        functools.partial(_combine_ln_kernel, tm=tm, alpha=alpha),
        out_shape=(jax.ShapeDtypeStruct((n, d), F32), jax.ShapeDtypeStruct((n, d), BF16)),
        grid_spec=pltpu.PrefetchScalarGridSpec(
            num_scalar_prefetch=1,
            grid=(n // tm,),
            in_specs=[pl.BlockSpec((tm, d), lambda i, s: (i, 0)),
                      pl.BlockSpec((tm, LANES), lambda i, s: (i, 0)),
                      pl.BlockSpec((1, d), lambda i, s: (0, 0)),
                      pl.BlockSpec((1, d), lambda i, s: (0, 0)),
                      pl.BlockSpec(memory_space=pl.ANY)],
            out_specs=(pl.BlockSpec((tm, d), lambda i, s: (i, 0)),
                       pl.BlockSpec((tm, d), lambda i, s: (i, 0))),
            scratch_shapes=[pltpu.VMEM((2, tm, d), F32), pltpu.SemaphoreType.DMA(())],
        ),
        compiler_params=_params(("arbitrary",)),
        name="moe_combine_ln",
    )(slots, h, info, g.reshape(1, d), b.reshape(1, d), ys)


def _moe_ln(h, w_group, w_inner, w_gate, w_up, w_down, ln_g, ln_b, alpha):
    n, d = h.shape
    tm = _tile(2 * n // N_EXPERTS, 256)
    n_rows = 2 * n + N_EXPERTS * tm
    n_tiles = n_rows // tm

    w_route = jnp.concatenate(
        [w_group, w_inner.transpose(1, 0, 2).reshape(d, N_EXPERTS),
         jnp.zeros((d, LANES - N_GROUPS - N_EXPERTS), F32)], axis=1)
    info, cnt = _route(h, w_route)

    expert = info[:, 0:2].astype(jnp.int32)
    rank = info[:, 2:4].astype(jnp.int32)
    counts = cnt[0, ROUTE_LANE0:ROUTE_LANE0 + N_EXPERTS].astype(jnp.int32)
    padded = (counts + tm - 1) // tm * tm
    ends = jnp.cumsum(padded)
    slots = ((ends - padded)[expert] + rank).reshape(-1)
    tile_ids = jnp.arange(n_tiles, dtype=jnp.int32)
    tile_expert = jnp.minimum(jnp.searchsorted(ends // tm, tile_ids, side="right"),
                              N_EXPERTS - 1).astype(jnp.int32)
    valid = (tile_ids < ends[-1] // tm).astype(jnp.int32)
    fresh = jnp.concatenate([jnp.ones((1,), jnp.int32),
                             (tile_expert[1:] != tile_expert[:-1]).astype(jnp.int32)])

    xs = _dispatch(h, slots, n_rows)
    act = _moe_up(xs, w_gate, w_up, tile_expert, fresh, valid, tm)
    ys = _moe_down(act, w_down, tile_expert, fresh, valid, tm)
    return _combine_ln(h, info, ys, slots, ln_g, ln_b, alpha)


def _rope_tables(seq):
    half = DIFF_QK_DIM // 2
    lane = jnp.arange(LANES)
    inv_freq = ROPE_THETA ** (-(lane % half).astype(F32) / half)
    ang = jnp.arange(seq, dtype=F32)[:, None] * inv_freq[None, :]
    sign = jnp.where((lane % DIFF_QK_DIM) < half, -1.0, 1.0)
    return jnp.cos(ang), jnp.sin(ang) * sign


def kernel(x, sb_w_qkv, sb_w_o, shared_w_kv, diff_w_q, diff_w_o, diff_lambda, diff_subln_g, ln_mix_g, ln_mix_b,
           ln_ffn_g, ln_ffn_b, moe_w_group, moe_w_inner, moe_w_gate, moe_w_up, moe_w_down):
    batch, seq, d = x.shape
    n = batch * seq
    depth = ln_mix_g.shape[0]
    n_sb = sb_w_qkv.shape[0]
    alpha = (2.0 * depth) ** 0.25
    rope_tabs = _rope_tables(seq)

    h = x.reshape(n, d)
    hb = h.astype(BF16)
    kv = None
    for layer in range(depth):
        if layer < n_sb:
            qkv = _proj(hb, sb_w_qkv[layer].astype(BF16), scale=HEAD_DIM ** -0.5, scale_cols=d)
            o = _sb_attention(qkv, batch, seq)
            w_o = sb_w_o[layer]
        else:
            i = layer - n_sb
            if kv is None:
                kv = _proj(hb, shared_w_kv.astype(BF16), rope_cols=d, rope_tabs=rope_tabs, seq=seq)
            qd = _proj(hb, diff_w_q[i].astype(BF16), scale=DIFF_QK_DIM ** -0.5, scale_cols=d,
                       rope_cols=d, rope_tabs=rope_tabs, seq=seq)
            lam_init = 0.8 - 0.6 * math.exp(-0.3 * layer)
            o = _diff_attention(qd, kv, diff_lambda[i], diff_subln_g[i], lam_init, batch, seq)
            w_o = diff_w_o[i]
        h, hb = _proj_ln(o, w_o.astype(BF16), h, ln_mix_g[layer], ln_mix_b[layer], alpha)
        h, hb = _moe_ln(h, moe_w_group[layer], moe_w_inner[layer], moe_w_gate[layer], moe_w_up[layer],
                        moe_w_down[layer], ln_ffn_g[layer], ln_ffn_b[layer], alpha)
    return h.reshape(batch, seq, d)
```

```python
import functools
import math

import jax
import jax.numpy as jnp
from jax import lax
from jax.experimental import pallas as pl
from jax.experimental.pallas import tpu as pltpu

HEAD_DIM = 128
DIFF_QK_DIM = HEAD_DIM // 2
N_GROUPS = 4
EXPERTS_PER_GROUP = 4
N_EXPERTS = N_GROUPS * EXPERTS_PER_GROUP
ROPE_THETA = 10000.0
LN_EPS = 1e-5
SUBLN_EPS = 1e-5
LANES = 128
ROUTE_LANE0 = N_GROUPS
NEG_BIG = -1e30
SB_SKIP_BELOW = -104.0
VMEM_LIMIT = 56 * 1024 * 1024

F32 = jnp.float32
BF16 = jnp.bfloat16


def _tile(n, pref):
    t = min(n, pref)
    assert n % t == 0, (n, pref)
    return t


def _params(sem):
    return pltpu.CompilerParams(dimension_semantics=sem, vmem_limit_bytes=VMEM_LIMIT)


def _dot(a, b):
    return jnp.dot(a, b, preferred_element_type=F32)


def _dot_nt(a, b):
    return lax.dot_general(a, b, (((1,), (1,)), ((), ())), preferred_element_type=F32)


def _rope_tile(acc, cos, sin):
    lane = lax.broadcasted_iota(jnp.int32, cos.shape, 1)
    first_half = (lane % DIFF_QK_DIM) < (DIFF_QK_DIM // 2)
    outs = []
    for g in range(acc.shape[1] // LANES):
        xg = acc[:, g * LANES:(g + 1) * LANES]
        up = pltpu.roll(xg, LANES - DIFF_QK_DIM // 2, axis=1)
        dn = pltpu.roll(xg, DIFF_QK_DIM // 2, axis=1)
        partner = jnp.where(first_half, up, dn)
        outs.append(xg * cos + partner * sin)
    return jnp.concatenate(outs, axis=1)


def _proj_kernel(x_ref, w_ref, *rest, scale, scale_tiles, rope_tiles):
    if rope_tiles:
        cos_ref, sin_ref, o_ref = rest
    else:
        (o_ref,) = rest
    j = pl.program_id(0)
    acc = _dot(x_ref[...], w_ref[...])
    if scale_tiles:
        acc = acc * jnp.where(j < scale_tiles, jnp.float32(scale), jnp.float32(1.0))
    if rope_tiles:
        @pl.when(j < rope_tiles)
        def _():
            o_ref[...] = _rope_tile(acc, cos_ref[...], sin_ref[...]).astype(o_ref.dtype)

        @pl.when(j >= rope_tiles)
        def _():
            o_ref[...] = acc.astype(o_ref.dtype)
    else:
        o_ref[...] = acc.astype(o_ref.dtype)


def _proj(x, w, *, scale=1.0, scale_cols=0, rope_cols=0, rope_tabs=None, seq=None):
    n, k = x.shape
    m = w.shape[1]
    tm = _tile(n, 512)
    tn = math.gcd(1024, m, scale_cols, rope_cols)
    assert tn % LANES == 0
    in_specs = [pl.BlockSpec((tm, k), lambda j, i: (i, 0)),
                pl.BlockSpec((k, tn), lambda j, i: (0, j))]
    args = [x, w]
    if rope_cols:
        assert seq % tm == 0
        spb = seq // tm
        in_specs += [pl.BlockSpec((tm, LANES), lambda j, i: (i % spb, 0))] * 2
        args += list(rope_tabs)
    kern = functools.partial(_proj_kernel, scale=scale, scale_tiles=scale_cols // tn,
                             rope_tiles=rope_cols // tn)
    return pl.pallas_call(
        kern,
        out_shape=jax.ShapeDtypeStruct((n, m), BF16),
        grid=(m // tn, n // tm),
        in_specs=in_specs,
        out_specs=pl.BlockSpec((tm, tn), lambda j, i: (i, j)),
        compiler_params=_params(("arbitrary", "arbitrary")),
        name="proj",
    )(*args)


def _layer_norm_rows(v, g, b):
    mu = jnp.mean(v, axis=-1, keepdims=True)
    c = v - mu
    var = jnp.mean(c * c, axis=-1, keepdims=True)
    return c * lax.rsqrt(var + LN_EPS) * g + b


def _proj_ln_kernel(x_ref, w_ref, res_ref, g_ref, b_ref, o_ref, ob_ref, *, alpha):
    acc = _dot(x_ref[...], w_ref[...])
    y = _layer_norm_rows(alpha * res_ref[...] + acc, g_ref[...], b_ref[...])
    o_ref[...] = y
    ob_ref[...] = y.astype(BF16)


def _proj_ln(x, w, res, g, b, alpha):
    n, k = x.shape
    d = w.shape[1]
    tm = _tile(n, 256)
    return pl.pallas_call(
        functools.partial(_proj_ln_kernel, alpha=alpha),
        out_shape=(jax.ShapeDtypeStruct((n, d), F32), jax.ShapeDtypeStruct((n, d), BF16)),
        grid=(n // tm,),
        in_specs=[pl.BlockSpec((tm, k), lambda i: (i, 0)),
                  pl.BlockSpec((k, d), lambda i: (0, 0)),
                  pl.BlockSpec((tm, d), lambda i: (i, 0)),
                  pl.BlockSpec((1, d), lambda i: (0, 0)),
                  pl.BlockSpec((1, d), lambda i: (0, 0))],
        out_specs=(pl.BlockSpec((tm, d), lambda i: (i, 0)),
                   pl.BlockSpec((tm, d), lambda i: (i, 0))),
        compiler_params=_params(("arbitrary",)),
        name="proj_ln",
    )(x, w, res, g.reshape(1, d), b.reshape(1, d))


def _sb_attn_kernel(q_ref, k_ref, v_ref, o_ref, *, t, hb):
    qi = pl.program_id(2)
    row = lax.broadcasted_iota(jnp.int32, (2 * t, t), 0) % t
    col = lax.broadcasted_iota(jnp.int32, (2 * t, t), 1)
    later = (row > col).astype(BF16)
    past = (lax.broadcasted_iota(jnp.int32, (t, t), 1)
            < lax.broadcasted_iota(jnp.int32, (t, t), 0))

    def step(kc, accs, tails, masked):
        start = pl.multiple_of(kc * t, t)
        cols = [slice(h * HEAD_DIM, (h + 1) * HEAD_DIM) for h in range(hb)]
        zs = [_dot_nt(q_ref[:, c], k_ref[pl.ds(start, t), c]) for c in cols]
        log_betas, log_keeps, splits = [], [], []
        for z in zs:
            nl = -jnp.log(1.0 + jnp.exp(-jnp.abs(z)))
            log_betas.append(nl + jnp.minimum(z, 0.0))
            log_keep = nl - jnp.maximum(z, 0.0)
            if masked:
                log_keep = jnp.where(past, log_keep, 0.0)
            hi = log_keep.astype(BF16)
            lo = (log_keep - hi.astype(F32)).astype(BF16)
            log_keeps.append(log_keep)
            splits.append(jnp.concatenate([hi, lo], axis=1))
        tail_ins = [_dot(s, later) for s in splits]
        ws = []
        for h in range(hb):
            w = jnp.exp(log_betas[h] + tail_ins[h] + tails[h])
            if masked:
                w = jnp.where(past, w, 0.0)
            ws.append(w.astype(BF16))
        outs = [_dot(ws[h], v_ref[pl.ds(start, t), cols[h]]) for h in range(hb)]
        accs = tuple(a + o for a, o in zip(accs, outs))
        tails = tuple(s + jnp.sum(lk, axis=1, keepdims=True) for s, lk in zip(tails, log_keeps))
        return accs, tails

    def any_weight_left(tails):
        worst = functools.reduce(jnp.maximum, tails)
        return (jnp.max(worst) >= SB_SKIP_BELOW).astype(jnp.int32)

    accs = tuple(jnp.zeros((t, HEAD_DIM), F32) for _ in range(hb))
    tails = tuple(jnp.zeros((t, 1), F32) for _ in range(hb))
    accs, tails = step(qi, accs, tails, True)

    def cond(c):
        return jnp.logical_and(c[0] >= 0, c[1] > 0)

    def body(c):
        kc, _, accs, tails = c
        accs, tails = step(kc, accs, tails, False)
        return kc - 1, any_weight_left(tails), accs, tails

    _, _, accs, _ = lax.while_loop(cond, body, (qi - 1, any_weight_left(tails), accs, tails))
    for h in range(hb):
        o_ref[:, h * HEAD_DIM:(h + 1) * HEAD_DIM] = accs[h].astype(o_ref.dtype)


def _sb_attention(qkv, batch, seq):
    n, three_d = qkv.shape
    d = three_d // 3
    heads = d // HEAD_DIM
    hb = math.gcd(heads, 8)
    t = _tile(seq, 128)
    nq = seq // t
    nh = heads // hb
    w = hb * HEAD_DIM
    return pl.pallas_call(
        functools.partial(_sb_attn_kernel, t=t, hb=hb),
        out_shape=jax.ShapeDtypeStruct((n, d), BF16),
        grid=(batch, nh, nq),
        in_specs=[pl.BlockSpec((t, w), lambda b, h, i: (b * nq + i, h)),
                  pl.BlockSpec((seq, w), lambda b, h, i: (b, nh + h)),
                  pl.BlockSpec((seq, w), lambda b, h, i: (b, 2 * nh + h))],
        out_specs=pl.BlockSpec((t, w), lambda b, h, i: (b * nq + i, h)),
        compiler_params=_params(("arbitrary", "arbitrary", "arbitrary")),
        name="sb_attn",
    )(qkv, qkv, qkv)


def _diff_attn_kernel(q_ref, k_ref, v_ref, lam_ref, g_ref, o_ref, *, t, lam_init):
    qi = pl.program_id(2)
    q = q_ref[...]
    lane = lax.broadcasted_iota(jnp.int32, (t, HEAD_DIM), 1)
    zero = jnp.zeros_like(q)
    q1 = jnp.where(lane < DIFF_QK_DIM, q, zero)
    q2 = jnp.where(lane >= DIFF_QK_DIM, q, zero)
    q_pos = lax.broadcasted_iota(jnp.int32, (t, t), 0)
    k_pos = lax.broadcasted_iota(jnp.int32, (t, t), 1)
    causal = k_pos <= q_pos

    def stream(qx, k, v, state, masked):
        m, l, acc = state
        s = _dot_nt(qx, k)
        if masked:
            s = jnp.where(causal, s, NEG_BIG)
        m_new = jnp.maximum(m, jnp.max(s, axis=1, keepdims=True))
        p = jnp.exp(s - m_new)
        corr = jnp.exp(m - m_new)
        l = corr * l + jnp.sum(p, axis=1, keepdims=True)
        acc = corr * acc + _dot(p.astype(BF16), v)
        return m_new, l, acc

    def step(kc, carry, masked):
        start = pl.multiple_of(kc * t, t)
        k = k_ref[pl.ds(start, t), :]
        v = v_ref[pl.ds(start, t), :]
        return (stream(q1, k, v, carry[0], masked), stream(q2, k, v, carry[1], masked))

    init = (jnp.full((t, 1), NEG_BIG, F32), jnp.zeros((t, 1), F32), jnp.zeros((t, HEAD_DIM), F32))
    carry = step(qi, (init, init), True)
    carry = lax.fori_loop(0, qi, lambda c, s: step(c, s, False), carry)
    (_, l1, a1), (_, l2, a2) = carry

    lp = lam_ref[...]
    lam = (jnp.exp(jnp.sum(lp[0:1] * lp[1:2], axis=1, keepdims=True))
           - jnp.exp(jnp.sum(lp[2:3] * lp[3:4], axis=1, keepdims=True)) + lam_init)
    o = a1 / l1 - lam * (a2 / l2)
    o = o * lax.rsqrt(jnp.mean(o * o, axis=1, keepdims=True) + SUBLN_EPS)
    o_ref[...] = (o * g_ref[...] * (1.0 - lam_init)).astype(o_ref.dtype)


def _diff_attention(qd, kv, lam_params, subln_g, lam_init, batch, seq):
    n, d = qd.shape
    heads = d // HEAD_DIM
    t = _tile(seq, 512)
    nq = seq // t
    return pl.pallas_call(
        functools.partial(_diff_attn_kernel, t=t, lam_init=lam_init),
        out_shape=jax.ShapeDtypeStruct((n, d), BF16),
        grid=(batch, heads, nq),
        in_specs=[pl.BlockSpec((t, HEAD_DIM), lambda b, h, i: (b * nq + i, h)),
                  pl.BlockSpec((seq, HEAD_DIM), lambda b, h, i: (b, h)),
                  pl.BlockSpec((seq, HEAD_DIM), lambda b, h, i: (b, heads + h)),
                  pl.BlockSpec(lam_params.shape, lambda b, h, i: (0, 0)),
                  pl.BlockSpec((1, HEAD_DIM), lambda b, h, i: (0, 0))],
        out_specs=pl.BlockSpec((t, HEAD_DIM), lambda b, h, i: (b * nq + i, h)),
        compiler_params=_params(("arbitrary", "arbitrary", "arbitrary")),
        name="diff_attn",
    )(qd, kv, kv, lam_params, subln_g.reshape(1, HEAD_DIM))


def _split_bf16(x):
    hi = x.astype(BF16)
    return hi, (x - hi.astype(F32)).astype(BF16)


def _route_kernel(h_ref, w_ref, info_ref, cnt_ref, *, tm):
    i = pl.program_id(0)

    @pl.when(i == 0)
    def _():
        cnt_ref[...] = jnp.zeros_like(cnt_ref)

    h_hi, h_lo = _split_bf16(h_ref[...])
    w_hi, w_lo = _split_bf16(w_ref[...])
    logits = _dot(h_hi, w_hi) + (_dot(h_hi, w_lo) + _dot(h_lo, w_hi))

    lane = lax.broadcasted_iota(jnp.int32, (tm, LANES), 1).astype(F32)

    def first_max(mask, vals):
        top = jnp.max(jnp.where(mask, vals, NEG_BIG), axis=1, keepdims=True)
        idx = jnp.min(jnp.where(mask & (vals == top), lane, float(LANES)), axis=1, keepdims=True)
        return top, idx

    g_mask = lane < N_GROUPS
    g_top, g_sel = first_max(g_mask, logits)
    g_den = jnp.sum(jnp.where(g_mask, jnp.exp(logits - g_top), 0.0), axis=1, keepdims=True)
    g_w = 1.0 / g_den

    lo_lane = ROUTE_LANE0 + EXPERTS_PER_GROUP * g_sel
    in_group = (lane >= lo_lane) & (lane < lo_lane + EXPERTS_PER_GROUP)
    v1, i1 = first_max(in_group, logits)
    v2, i2 = first_max(in_group & (lane != i1), logits)
    ex = jnp.exp(v2 - v1)
    w1 = g_w / (1.0 + ex)
    w2 = w1 * ex

    m1 = (lane == i1).astype(F32)
    m2 = (lane == i2).astype(F32)
    both = m1 + m2
    r = lax.broadcasted_iota(jnp.int32, (tm, tm), 0)
    c = lax.broadcasted_iota(jnp.int32, (tm, tm), 1)
    earlier = (c < r).astype(BF16)
    before = _dot(earlier, both.astype(BF16)) + cnt_ref[0:1, :]
    rank1 = jnp.sum(m1 * before, axis=1, keepdims=True)
    rank2 = jnp.sum(m2 * before, axis=1, keepdims=True)
    cnt_ref[...] = cnt_ref[...] + jnp.sum(both, axis=0, keepdims=True)

    info = jnp.where(lane == 0, i1 - ROUTE_LANE0, 0.0)
    info = jnp.where(lane == 1, i2 - ROUTE_LANE0, info)
    info = jnp.where(lane == 2, rank1, info)
    info = jnp.where(lane == 3, rank2, info)
    info = jnp.where(lane == 4, w1, info)
    info = jnp.where(lane == 5, w2, info)
    info_ref[...] = info


def _route(h, w_route):
    n, d = h.shape
    tm = _tile(n, 512)
    return pl.pallas_call(
        functools.partial(_route_kernel, tm=tm),
        out_shape=(jax.ShapeDtypeStruct((n, LANES), F32), jax.ShapeDtypeStruct((8, LANES), F32)),
        grid=(n // tm,),
        in_specs=[pl.BlockSpec((tm, d), lambda i: (i, 0)),
                  pl.BlockSpec((d, LANES), lambda i: (0, 0))],
        out_specs=(pl.BlockSpec((tm, LANES), lambda i: (i, 0)),
                   pl.BlockSpec((8, LANES), lambda i: (0, 0))),
        compiler_params=_params(("arbitrary",)),
        name="route",
    )(h, w_route)


def _dispatch_kernel(slot_ref, h_ref, xs_in_hbm, xs_hbm, sem, *, tm):
    del xs_in_hbm
    base = pl.program_id(0) * tm

    def row_copy(r, s):
        return pltpu.make_async_copy(h_ref.at[pl.ds(r, 1), :], xs_hbm.at[pl.ds(s, 1), :], sem)

    def issue(r, _):
        t = base + r
        row_copy(r, slot_ref[2 * t]).start()
        row_copy(r, slot_ref[2 * t + 1]).start()
        return 0

    def drain(r, _):
        row_copy(0, 0).wait()
        row_copy(0, 0).wait()
        return 0

    lax.fori_loop(0, tm, issue, 0)
    lax.fori_loop(0, tm, drain, 0)


def _dispatch(h, slots, n_rows):
    n, d = h.shape
    tm = _tile(n, 256)
    return pl.pallas_call(
        functools.partial(_dispatch_kernel, tm=tm),
        out_shape=jax.ShapeDtypeStruct((n_rows, d), h.dtype),
        grid_spec=pltpu.PrefetchScalarGridSpec(
            num_scalar_prefetch=1,
            grid=(n // tm,),
            in_specs=[pl.BlockSpec((tm, d), lambda i, s: (i, 0)), pl.BlockSpec(memory_space=pl.ANY)],
            out_specs=pl.BlockSpec(memory_space=pl.ANY),
            scratch_shapes=[pltpu.SemaphoreType.DMA(())],
        ),
        input_output_aliases={2: 0},
        compiler_params=_params(("arbitrary",)),
        name="moe_dispatch",
    )(slots, h, jnp.zeros((n_rows, d), h.dtype))


def _moe_up_kernel(te_ref, fresh_ref, valid_ref, x_ref, wg_ref, wu_ref, o_ref, wg_bf, wu_bf):
    i = pl.program_id(1)

    @pl.when(fresh_ref[i] == 1)
    def _():
        wg_bf[...] = wg_ref[...].astype(BF16)
        wu_bf[...] = wu_ref[...].astype(BF16)

    @pl.when(valid_ref[i] == 1)
    def _():
        x = x_ref[...].astype(BF16)
        g = _dot(x, wg_bf[...])
        u = _dot(x, wu_bf[...])
        o_ref[...] = (g * (1.0 / (1.0 + jnp.exp(-g))) * u).astype(o_ref.dtype)

    @pl.when(valid_ref[i] == 0)
    def _():
        o_ref[...] = jnp.zeros_like(o_ref)


def _moe_up(xs, w_gate, w_up, tile_expert, fresh, valid, tm):
    p, d = xs.shape
    f = w_gate.shape[2]
    tn = _tile(f, 512)
    return pl.pallas_call(
        _moe_up_kernel,
        out_shape=jax.ShapeDtypeStruct((p, f), BF16),
        grid_spec=pltpu.PrefetchScalarGridSpec(
            num_scalar_prefetch=3,
            grid=(f // tn, p // tm),
            in_specs=[pl.BlockSpec((tm, d), lambda j, i, te, fr, va: (i, 0)),
                      pl.BlockSpec((None, d, tn), lambda j, i, te, fr, va: (te[i], 0, j)),
                      pl.BlockSpec((None, d, tn), lambda j, i, te, fr, va: (te[i], 0, j))],
            out_specs=pl.BlockSpec((tm, tn), lambda j, i, te, fr, va: (i, j)),
            scratch_shapes=[pltpu.VMEM((d, tn), BF16), pltpu.VMEM((d, tn), BF16)],
        ),
        compiler_params=_params(("arbitrary", "arbitrary")),
        name="moe_up",
    )(tile_expert, fresh, valid, xs, w_gate, w_up)


def _moe_down_kernel(te_ref, fresh_ref, valid_ref, a_ref, wd_ref, o_ref, wd_bf):
    i = pl.program_id(1)

    @pl.when(fresh_ref[i] == 1)
    def _():
        wd_bf[...] = wd_ref[...].astype(BF16)

    @pl.when(valid_ref[i] == 1)
    def _():
        o_ref[...] = _dot(a_ref[...], wd_bf[...])

    @pl.when(valid_ref[i] == 0)
    def _():
        o_ref[...] = jnp.zeros_like(o_ref)


def _moe_down(act, w_down, tile_expert, fresh, valid, tm):
    p, f = act.shape
    d = w_down.shape[2]
    tn = _tile(d, 1024)
    return pl.pallas_call(
        _moe_down_kernel,
        out_shape=jax.ShapeDtypeStruct((p, d), F32),
        grid_spec=pltpu.PrefetchScalarGridSpec(
            num_scalar_prefetch=3,
            grid=(d // tn, p // tm),
            in_specs=[pl.BlockSpec((tm, f), lambda j, i, te, fr, va: (i, 0)),
                      pl.BlockSpec((None, f, tn), lambda j, i, te, fr, va: (te[i], 0, j))],
            out_specs=pl.BlockSpec((tm, tn), lambda j, i, te, fr, va: (i, j)),
            scratch_shapes=[pltpu.VMEM((f, tn), BF16)],
        ),
        compiler_params=_params(("arbitrary", "arbitrary")),
        name="moe_down",
    )(tile_expert, fresh, valid, act, w_down)


def _combine_ln_kernel(slot_ref, h_ref, info_ref, g_ref, b_ref, ys_hbm, o_ref, ob_ref, buf, sem, *, tm, alpha):
    base = pl.program_id(0) * tm

    def row_copy(s, k, r):
        return pltpu.make_async_copy(ys_hbm.at[pl.ds(s, 1), :], buf.at[k, pl.ds(r, 1), :], sem)

    def issue(r, _):
        t = base + r
        row_copy(slot_ref[2 * t], 0, r).start()
        row_copy(slot_ref[2 * t + 1], 1, r).start()
        return 0

    def drain(r, _):
        row_copy(0, 0, 0).wait()
        row_copy(0, 0, 0).wait()
        return 0

    lax.fori_loop(0, tm, issue, 0)
    lax.fori_loop(0, tm, drain, 0)

    info = info_ref[...]
    ffn = info[:, 4:5] * buf[0] + info[:, 5:6] * buf[1]
    y = _layer_norm_rows(alpha * h_ref[...] + ffn, g_ref[...], b_ref[...])
    o_ref[...] = y
    ob_ref[...] = y.astype(BF16)


def _combine_ln(h, info, ys, slots, g, b, alpha):
    n, d = h.shape
    tm = _tile(n, 256)
    return pl.pallas_call(
        functools.partial(_combine_ln_kernel, tm=tm, alpha=alpha),
        out_shape=(jax.ShapeDtypeStruct((n, d), F32), jax.ShapeDtypeStruct((n, d), BF16)),
        grid_spec=pltpu.PrefetchScalarGridSpec(
            num_scalar_prefetch=1,
            grid=(n // tm,),
            in_specs=[pl.BlockSpec((tm, d), lambda i, s: (i, 0)),
                      pl.BlockSpec((tm, LANES), lambda i, s: (i, 0)),
                      pl.BlockSpec((1, d), lambda i, s: (0, 0)),
                      pl.BlockSpec((1, d), lambda i, s: (0, 0)),
                      pl.BlockSpec(memory_space=pl.ANY)],
            out_specs=(pl.BlockSpec((tm, d), lambda i, s: (i, 0)),
                       pl.BlockSpec((tm, d), lambda i, s: (i, 0))),
            scratch_shapes=[pltpu.VMEM((2, tm, d), F32), pltpu.SemaphoreType.DMA(())],
        ),
        compiler_params=_params(("arbitrary",)),
        name="moe_combine_ln",
    )(slots, h, info, g.reshape(1, d), b.reshape(1, d), ys)


def _moe_ln(h, w_group, w_inner, w_gate, w_up, w_down, ln_g, ln_b, alpha):
    n, d = h.shape
    tm = _tile(2 * n // N_EXPERTS, 256)
    n_rows = 2 * n + N_EXPERTS * tm
    n_tiles = n_rows // tm

    w_route = jnp.concatenate(
        [w_group, w_inner.transpose(1, 0, 2).reshape(d, N_EXPERTS),
         jnp.zeros((d, LANES - N_GROUPS - N_EXPERTS), F32)], axis=1)
    info, cnt = _route(h, w_route)

    expert = info[:, 0:2].astype(jnp.int32)
    rank = info[:, 2:4].astype(jnp.int32)
    counts = cnt[0, ROUTE_LANE0:ROUTE_LANE0 + N_EXPERTS].astype(jnp.int32)
    padded = (counts + tm - 1) // tm * tm
    ends = jnp.cumsum(padded)
    slots = ((ends - padded)[expert] + rank).reshape(-1)
    tile_ids = jnp.arange(n_tiles, dtype=jnp.int32)
    tile_expert = jnp.minimum(jnp.sum((tile_ids[:, None] >= (ends // tm)[None, :]).astype(jnp.int32), axis=1),
                              N_EXPERTS - 1)
    valid = (tile_ids < ends[-1] // tm).astype(jnp.int32)
    fresh = jnp.concatenate([jnp.ones((1,), jnp.int32),
                             (tile_expert[1:] != tile_expert[:-1]).astype(jnp.int32)])

    xs = _dispatch(h, slots, n_rows)
    act = _moe_up(xs, w_gate, w_up, tile_expert, fresh, valid, tm)
    ys = _moe_down(act, w_down, tile_expert, fresh, valid, tm)
    return _combine_ln(h, info, ys, slots, ln_g, ln_b, alpha)


def _rope_tables(seq):
    half = DIFF_QK_DIM // 2
    lane = jnp.arange(LANES)
    inv_freq = ROPE_THETA ** (-(lane % half).astype(F32) / half)
    ang = jnp.arange(seq, dtype=F32)[:, None] * inv_freq[None, :]
    sign = jnp.where((lane % DIFF_QK_DIM) < half, -1.0, 1.0)
    return jnp.cos(ang), jnp.sin(ang) * sign


def kernel(x, sb_w_qkv, sb_w_o, shared_w_kv, diff_w_q, diff_w_o, diff_lambda, diff_subln_g, ln_mix_g, ln_mix_b,
           ln_ffn_g, ln_ffn_b, moe_w_group, moe_w_inner, moe_w_gate, moe_w_up, moe_w_down):
    batch, seq, d = x.shape
    n = batch * seq
    depth = ln_mix_g.shape[0]
    n_sb = sb_w_qkv.shape[0]
    alpha = (2.0 * depth) ** 0.25
    rope_tabs = _rope_tables(seq)

    h = x.reshape(n, d)
    hb = h.astype(BF16)
    kv = None
    for layer in range(depth):
        if layer < n_sb:
            qkv = _proj(hb, sb_w_qkv[layer].astype(BF16), scale=HEAD_DIM ** -0.5, scale_cols=d)
            o = _sb_attention(qkv, batch, seq)
            w_o = sb_w_o[layer]
        else:
            i = layer - n_sb
            if kv is None:
                kv = _proj(hb, shared_w_kv.astype(BF16), rope_cols=d, rope_tabs=rope_tabs, seq=seq)
            qd = _proj(hb, diff_w_q[i].astype(BF16), scale=DIFF_QK_DIM ** -0.5, scale_cols=d,
                       rope_cols=d, rope_tabs=rope_tabs, seq=seq)
            lam_init = 0.8 - 0.6 * math.exp(-0.3 * layer)
            o = _diff_attention(qd, kv, diff_lambda[i], diff_subln_g[i], lam_init, batch, seq)
            w_o = diff_w_o[i]
        h, hb = _proj_ln(o, w_o.astype(BF16), h, ln_mix_g[layer], ln_mix_b[layer], alpha)
        h, hb = _moe_ln(h, moe_w_group[layer], moe_w_inner[layer], moe_w_gate[layer], moe_w_up[layer],
                        moe_w_down[layer], ln_ffn_g[layer], ln_ffn_b[layer], alpha)
    return h.reshape(batch, seq, d)
```

```python
import functools
import math

import jax
import jax.numpy as jnp
from jax import lax
from jax.experimental import pallas as pl
from jax.experimental.pallas import tpu as pltpu

HEAD_DIM = 128
DIFF_QK_DIM = HEAD_DIM // 2
N_GROUPS = 4
EXPERTS_PER_GROUP = 4
N_EXPERTS = N_GROUPS * EXPERTS_PER_GROUP
ROPE_THETA = 10000.0
LN_EPS = 1e-5
SUBLN_EPS = 1e-5
LANES = 128
ROUTE_LANE0 = N_GROUPS
NEG_BIG = -1e30
SB_SKIP_BELOW = -104.0
VMEM_LIMIT = 56 * 1024 * 1024

F32 = jnp.float32
BF16 = jnp.bfloat16


def _tile(n, pref):
    t = min(n, pref)
    assert n % t == 0, (n, pref)
    return t


def _params(sem):
    return pltpu.CompilerParams(dimension_semantics=sem, vmem_limit_bytes=VMEM_LIMIT)


def _dot(a, b):
    return jnp.dot(a, b, preferred_element_type=F32)


def _dot_nt(a, b):
    return lax.dot_general(a, b, (((1,), (1,)), ((), ())), preferred_element_type=F32)


def _rope_tile(acc, cos, sin):
    lane = lax.broadcasted_iota(jnp.int32, cos.shape, 1)
    first_half = (lane % DIFF_QK_DIM) < (DIFF_QK_DIM // 2)
    outs = []
    for g in range(acc.shape[1] // LANES):
        xg = acc[:, g * LANES:(g + 1) * LANES]
        up = pltpu.roll(xg, LANES - DIFF_QK_DIM // 2, axis=1)
        dn = pltpu.roll(xg, DIFF_QK_DIM // 2, axis=1)
        partner = jnp.where(first_half, up, dn)
        outs.append(xg * cos + partner * sin)
    return jnp.concatenate(outs, axis=1)


def _proj_kernel(x_ref, w_ref, *rest, scale, scale_tiles, rope_tiles):
    if rope_tiles:
        cos_ref, sin_ref, o_ref = rest
    else:
        (o_ref,) = rest
    j = pl.program_id(0)
    acc = _dot(x_ref[...], w_ref[...])
    if scale_tiles:
        acc = acc * jnp.where(j < scale_tiles, jnp.float32(scale), jnp.float32(1.0))
    if rope_tiles:
        @pl.when(j < rope_tiles)
        def _():
            o_ref[...] = _rope_tile(acc, cos_ref[...], sin_ref[...]).astype(o_ref.dtype)

        @pl.when(j >= rope_tiles)
        def _():
            o_ref[...] = acc.astype(o_ref.dtype)
    else:
        o_ref[...] = acc.astype(o_ref.dtype)


def _proj(x, w, *, scale=1.0, scale_cols=0, rope_cols=0, rope_tabs=None, seq=None):
    n, k = x.shape
    m = w.shape[1]
    tm = _tile(n, 512)
    tn = math.gcd(1024, m, scale_cols, rope_cols)
    assert tn % LANES == 0
    in_specs = [pl.BlockSpec((tm, k), lambda j, i: (i, 0)),
                pl.BlockSpec((k, tn), lambda j, i: (0, j))]
    args = [x, w]
    if rope_cols:
        assert seq % tm == 0
        spb = seq // tm
        in_specs += [pl.BlockSpec((tm, LANES), lambda j, i: (i % spb, 0))] * 2
        args += list(rope_tabs)
    kern = functools.partial(_proj_kernel, scale=scale, scale_tiles=scale_cols // tn,
                             rope_tiles=rope_cols // tn)
    return pl.pallas_call(
        kern,
        out_shape=jax.ShapeDtypeStruct((n, m), BF16),
        grid=(m // tn, n // tm),
        in_specs=in_specs,
        out_specs=pl.BlockSpec((tm, tn), lambda j, i: (i, j)),
        compiler_params=_params(("arbitrary", "arbitrary")),
        name="proj",
    )(*args)


def _layer_norm_rows(v, g, b):
    mu = jnp.mean(v, axis=-1, keepdims=True)
    c = v - mu
    var = jnp.mean(c * c, axis=-1, keepdims=True)
    return c * lax.rsqrt(var + LN_EPS) * g + b


def _pack_bf16_pairs(y):
    c = y.shape[1] // 2
    bits = lax.bitcast_convert_type(y.astype(BF16).astype(F32), jnp.uint32)
    return bits[:, :c] | (bits[:, c:] >> 16)


def _unpack_bf16_pairs(w):
    return (lax.bitcast_convert_type(w & jnp.uint32(0xFFFF0000), F32),
            lax.bitcast_convert_type(w << 16, F32))


def _proj_ln_kernel(x_ref, w_ref, res_ref, g_ref, b_ref, o_ref, op_ref, *, alpha):
    acc = _dot(x_ref[...], w_ref[...])
    y = _layer_norm_rows(alpha * res_ref[...] + acc, g_ref[...], b_ref[...])
    o_ref[...] = y
    op_ref[...] = _pack_bf16_pairs(y)


def _proj_ln(x, w, res, g, b, alpha):
    n, k = x.shape
    d = w.shape[1]
    tm = _tile(n, 256)
    return pl.pallas_call(
        functools.partial(_proj_ln_kernel, alpha=alpha),
        out_shape=(jax.ShapeDtypeStruct((n, d), F32), jax.ShapeDtypeStruct((n, d // 2), jnp.uint32)),
        grid=(n // tm,),
        in_specs=[pl.BlockSpec((tm, k), lambda i: (i, 0)),
                  pl.BlockSpec((k, d), lambda i: (0, 0)),
                  pl.BlockSpec((tm, d), lambda i: (i, 0)),
                  pl.BlockSpec((1, d), lambda i: (0, 0)),
                  pl.BlockSpec((1, d), lambda i: (0, 0))],
        out_specs=(pl.BlockSpec((tm, d), lambda i: (i, 0)),
                   pl.BlockSpec((tm, d // 2), lambda i: (i, 0))),
        compiler_params=_params(("arbitrary",)),
        name="proj_ln",
    )(x, w, res, g.reshape(1, d), b.reshape(1, d))


def _sb_attn_kernel(q_ref, k_ref, v_ref, o_ref, *, t, hb):
    qi = pl.program_id(2)
    row = lax.broadcasted_iota(jnp.int32, (2 * t, t), 0) % t
    col = lax.broadcasted_iota(jnp.int32, (2 * t, t), 1)
    later = (row > col).astype(BF16)
    past = (lax.broadcasted_iota(jnp.int32, (t, t), 1)
            < lax.broadcasted_iota(jnp.int32, (t, t), 0))

    def step(kc, accs, tails, masked):
        start = pl.multiple_of(kc * t, t)
        cols = [slice(h * HEAD_DIM, (h + 1) * HEAD_DIM) for h in range(hb)]
        zs = [_dot_nt(q_ref[:, c], k_ref[pl.ds(start, t), c]) for c in cols]
        log_betas, log_keeps, splits = [], [], []
        for z in zs:
            nl = -jnp.log(1.0 + jnp.exp(-jnp.abs(z)))
            log_betas.append(nl + jnp.minimum(z, 0.0))
            log_keep = nl - jnp.maximum(z, 0.0)
            if masked:
                log_keep = jnp.where(past, log_keep, 0.0)
            hi = log_keep.astype(BF16)
            lo = (log_keep - hi.astype(F32)).astype(BF16)
            log_keeps.append(log_keep)
            splits.append(jnp.concatenate([hi, lo], axis=1))
        tail_ins = [_dot(s, later) for s in splits]
        ws = []
        for h in range(hb):
            w = jnp.exp(log_betas[h] + tail_ins[h] + tails[h])
            if masked:
                w = jnp.where(past, w, 0.0)
            ws.append(w.astype(BF16))
        outs = [_dot(ws[h], v_ref[pl.ds(start, t), cols[h]]) for h in range(hb)]
        accs = tuple(a + o for a, o in zip(accs, outs))
        tails = tuple(s + jnp.sum(lk, axis=1, keepdims=True) for s, lk in zip(tails, log_keeps))
        return accs, tails

    def any_weight_left(tails):
        worst = functools.reduce(jnp.maximum, tails)
        return (jnp.max(worst) >= SB_SKIP_BELOW).astype(jnp.int32)

    accs = tuple(jnp.zeros((t, HEAD_DIM), F32) for _ in range(hb))
    tails = tuple(jnp.zeros((t, 1), F32) for _ in range(hb))
    accs, tails = step(qi, accs, tails, True)

    def cond(c):
        return jnp.logical_and(c[0] >= 0, c[1] > 0)

    def body(c):
        kc, _, accs, tails = c
        accs, tails = step(kc, accs, tails, False)
        return kc - 1, any_weight_left(tails), accs, tails

    _, _, accs, _ = lax.while_loop(cond, body, (qi - 1, any_weight_left(tails), accs, tails))
    for h in range(hb):
        o_ref[:, h * HEAD_DIM:(h + 1) * HEAD_DIM] = accs[h].astype(o_ref.dtype)


def _sb_attention(qkv, batch, seq):
    n, three_d = qkv.shape
    d = three_d // 3
    heads = d // HEAD_DIM
    hb = math.gcd(heads, 8)
    t = _tile(seq, 128)
    nq = seq // t
    nh = heads // hb
    w = hb * HEAD_DIM
    return pl.pallas_call(
        functools.partial(_sb_attn_kernel, t=t, hb=hb),
        out_shape=jax.ShapeDtypeStruct((n, d), BF16),
        grid=(batch, nh, nq),
        in_specs=[pl.BlockSpec((t, w), lambda b, h, i: (b * nq + i, h)),
                  pl.BlockSpec((seq, w), lambda b, h, i: (b, nh + h)),
                  pl.BlockSpec((seq, w), lambda b, h, i: (b, 2 * nh + h))],
        out_specs=pl.BlockSpec((t, w), lambda b, h, i: (b * nq + i, h)),
        compiler_params=_params(("arbitrary", "arbitrary", "arbitrary")),
        name="sb_attn",
    )(qkv, qkv, qkv)


def _diff_attn_kernel(q_ref, k_ref, v_ref, lam_ref, g_ref, o_ref, *, t, lam_init):
    qi = pl.program_id(2)
    q = q_ref[...]
    lane = lax.broadcasted_iota(jnp.int32, (t, HEAD_DIM), 1)
    zero = jnp.zeros_like(q)
    q1 = jnp.where(lane < DIFF_QK_DIM, q, zero)
    q2 = jnp.where(lane >= DIFF_QK_DIM, q, zero)
    q_pos = lax.broadcasted_iota(jnp.int32, (t, t), 0)
    k_pos = lax.broadcasted_iota(jnp.int32, (t, t), 1)
    causal = k_pos <= q_pos

    def stream(qx, k, v, state, masked):
        m, l, acc = state
        s = _dot_nt(qx, k)
        if masked:
            s = jnp.where(causal, s, NEG_BIG)
        m_new = jnp.maximum(m, jnp.max(s, axis=1, keepdims=True))
        p = jnp.exp(s - m_new)
        corr = jnp.exp(m - m_new)
        l = corr * l + jnp.sum(p, axis=1, keepdims=True)
        acc = corr * acc + _dot(p.astype(BF16), v)
        return m_new, l, acc

    def step(kc, carry, masked):
        start = pl.multiple_of(kc * t, t)
        k = k_ref[pl.ds(start, t), :]
        v = v_ref[pl.ds(start, t), :]
        return (stream(q1, k, v, carry[0], masked), stream(q2, k, v, carry[1], masked))

    init = (jnp.full((t, 1), NEG_BIG, F32), jnp.zeros((t, 1), F32), jnp.zeros((t, HEAD_DIM), F32))
    carry = step(qi, (init, init), True)
    carry = lax.fori_loop(0, qi, lambda c, s: step(c, s, False), carry)
    (_, l1, a1), (_, l2, a2) = carry

    lp = lam_ref[...]
    lam = (jnp.exp(jnp.sum(lp[0:1] * lp[1:2], axis=1, keepdims=True))
           - jnp.exp(jnp.sum(lp[2:3] * lp[3:4], axis=1, keepdims=True)) + lam_init)
    o = a1 / l1 - lam * (a2 / l2)
    o = o * lax.rsqrt(jnp.mean(o * o, axis=1, keepdims=True) + SUBLN_EPS)
    o_ref[...] = (o * g_ref[...] * (1.0 - lam_init)).astype(o_ref.dtype)


def _diff_attention(qd, kv, lam_params, subln_g, lam_init, batch, seq):
    n, d = qd.shape
    heads = d // HEAD_DIM
    t = _tile(seq, 512)
    nq = seq // t
    return pl.pallas_call(
        functools.partial(_diff_attn_kernel, t=t, lam_init=lam_init),
        out_shape=jax.ShapeDtypeStruct((n, d), BF16),
        grid=(batch, heads, nq),
        in_specs=[pl.BlockSpec((t, HEAD_DIM), lambda b, h, i: (b * nq + i, h)),
                  pl.BlockSpec((seq, HEAD_DIM), lambda b, h, i: (b, h)),
                  pl.BlockSpec((seq, HEAD_DIM), lambda b, h, i: (b, heads + h)),
                  pl.BlockSpec(lam_params.shape, lambda b, h, i: (0, 0)),
                  pl.BlockSpec((1, HEAD_DIM), lambda b, h, i: (0, 0))],
        out_specs=pl.BlockSpec((t, HEAD_DIM), lambda b, h, i: (b * nq + i, h)),
        compiler_params=_params(("arbitrary", "arbitrary", "arbitrary")),
        name="diff_attn",
    )(qd, kv, kv, lam_params, subln_g.reshape(1, HEAD_DIM))


def _split_bf16(x):
    hi = x.astype(BF16)
    return hi, (x - hi.astype(F32)).astype(BF16)


def _route_kernel(h_ref, w_ref, info_ref, cnt_ref, *, tm):
    i = pl.program_id(0)

    @pl.when(i == 0)
    def _():
        cnt_ref[...] = jnp.zeros_like(cnt_ref)

    h_hi, h_lo = _split_bf16(h_ref[...])
    w_hi, w_lo = _split_bf16(w_ref[...])
    logits = _dot(h_hi, w_hi) + (_dot(h_hi, w_lo) + _dot(h_lo, w_hi))

    lane = lax.broadcasted_iota(jnp.int32, (tm, LANES), 1).astype(F32)

    def first_max(mask, vals):
        top = jnp.max(jnp.where(mask, vals, NEG_BIG), axis=1, keepdims=True)
        idx = jnp.min(jnp.where(mask & (vals == top), lane, float(LANES)), axis=1, keepdims=True)
        return top, idx

    g_mask = lane < N_GROUPS
    g_top, g_sel = first_max(g_mask, logits)
    g_den = jnp.sum(jnp.where(g_mask, jnp.exp(logits - g_top), 0.0), axis=1, keepdims=True)
    g_w = 1.0 / g_den

    lo_lane = ROUTE_LANE0 + EXPERTS_PER_GROUP * g_sel
    in_group = (lane >= lo_lane) & (lane < lo_lane + EXPERTS_PER_GROUP)
    v1, i1 = first_max(in_group, logits)
    v2, i2 = first_max(in_group & (lane != i1), logits)
    ex = jnp.exp(v2 - v1)
    w1 = g_w / (1.0 + ex)
    w2 = w1 * ex

    m1 = (lane == i1).astype(F32)
    m2 = (lane == i2).astype(F32)
    both = m1 + m2
    r = lax.broadcasted_iota(jnp.int32, (tm, tm), 0)
    c = lax.broadcasted_iota(jnp.int32, (tm, tm), 1)
    earlier = (c < r).astype(BF16)
    before = _dot(earlier, both.astype(BF16)) + cnt_ref[0:1, :]
    rank1 = jnp.sum(m1 * before, axis=1, keepdims=True)
    rank2 = jnp.sum(m2 * before, axis=1, keepdims=True)
    cnt_ref[...] = cnt_ref[...] + jnp.sum(both, axis=0, keepdims=True)

    info = jnp.where(lane == 0, i1 - ROUTE_LANE0, 0.0)
    info = jnp.where(lane == 1, i2 - ROUTE_LANE0, info)
    info = jnp.where(lane == 2, rank1, info)
    info = jnp.where(lane == 3, rank2, info)
    info = jnp.where(lane == 4, w1, info)
    info = jnp.where(lane == 5, w2, info)
    info_ref[...] = info


def _route(h, w_route):
    n, d = h.shape
    tm = _tile(n, 512)
    return pl.pallas_call(
        functools.partial(_route_kernel, tm=tm),
        out_shape=(jax.ShapeDtypeStruct((n, LANES), F32), jax.ShapeDtypeStruct((8, LANES), F32)),
        grid=(n // tm,),
        in_specs=[pl.BlockSpec((tm, d), lambda i: (i, 0)),
                  pl.BlockSpec((d, LANES), lambda i: (0, 0))],
        out_specs=(pl.BlockSpec((tm, LANES), lambda i: (i, 0)),
                   pl.BlockSpec((8, LANES), lambda i: (0, 0))),
        compiler_params=_params(("arbitrary",)),
        name="route",
    )(h, w_route)


def _dispatch_kernel(slot_ref, pad_ref, used_ref, h_ref, xs_hbm, zrow, sem, pad_sem, *, tm, n_pad, n_tiles):
    i = pl.program_id(0)
    base = i * tm

    def row_copy(r, s):
        return pltpu.make_async_copy(h_ref.at[pl.ds(r, 1), :], xs_hbm.at[pl.ds(s, 1), :], sem)

    def issue(r, _):
        t = base + r
        row_copy(r, slot_ref[2 * t]).start()
        row_copy(r, slot_ref[2 * t + 1]).start()
        return 0

    lax.fori_loop(0, tm, issue, 0)

    def pad_copy(s):
        return pltpu.make_async_copy(zrow.at[pl.ds(0, 1), :], xs_hbm.at[pl.ds(s, 1), :], pad_sem)

    @pl.when(i == 0)
    def _():
        zrow[...] = jnp.zeros_like(zrow)

        def issue_pad(r, _):
            @pl.when(pad_ref[r] >= 0)
            def _():
                pad_copy(pad_ref[r]).start()
            return 0

        def drain_pad(r, _):
            @pl.when(pad_ref[r] >= 0)
            def _():
                pad_copy(0).wait()
            return 0

        def tile_copy(t):
            return pltpu.make_async_copy(zrow, xs_hbm.at[pl.ds(pl.multiple_of(t * tm, tm), tm), :], pad_sem)

        def issue_tile(t, _):
            tile_copy(t).start()
            return 0

        def drain_tile(t, _):
            tile_copy(t).wait()
            return 0

        lax.fori_loop(0, n_pad, issue_pad, 0)
        lax.fori_loop(used_ref[0], n_tiles, issue_tile, 0)
        lax.fori_loop(0, n_pad, drain_pad, 0)
        lax.fori_loop(used_ref[0], n_tiles, drain_tile, 0)

    whole_tile = pltpu.make_async_copy(h_ref, xs_hbm.at[pl.ds(0, tm), :], sem)
    whole_tile.wait()
    whole_tile.wait()


def _dispatch(hp, slots, pad_slots, tiles_used, tm):
    n, c = hp.shape
    n_tiles = 2 * n // tm + N_EXPERTS
    assert n % tm == 0
    return pl.pallas_call(
        functools.partial(_dispatch_kernel, tm=tm, n_pad=pad_slots.shape[0], n_tiles=n_tiles),
        out_shape=jax.ShapeDtypeStruct((n_tiles * tm, c), hp.dtype),
        grid_spec=pltpu.PrefetchScalarGridSpec(
            num_scalar_prefetch=3,
            grid=(n // tm,),
            in_specs=[pl.BlockSpec((tm, c), lambda i, s, p, u: (i, 0))],
            out_specs=pl.BlockSpec(memory_space=pl.ANY),
            scratch_shapes=[pltpu.VMEM((tm, c), hp.dtype), pltpu.SemaphoreType.DMA(()),
                            pltpu.SemaphoreType.DMA(())],
        ),
        compiler_params=_params(("arbitrary",)),
        name="moe_dispatch",
    )(slots, pad_slots, tiles_used, hp)


def _moe_up_kernel(te_ref, fresh_ref, valid_ref, x_ref, wg_ref, wu_ref, o_ref, wg_bf, wu_bf):
    i = pl.program_id(0)
    c = x_ref.shape[1]

    @pl.when(fresh_ref[i] == 1)
    def _():
        wg_bf[...] = wg_ref[...].astype(BF16)
        wu_bf[...] = wu_ref[...].astype(BF16)

    @pl.when(valid_ref[i] == 1)
    def _():
        xa, xb = _unpack_bf16_pairs(x_ref[...])
        xa, xb = xa.astype(BF16), xb.astype(BF16)
        g = _dot(xa, wg_bf[:c, :]) + _dot(xb, wg_bf[c:, :])
        u = _dot(xa, wu_bf[:c, :]) + _dot(xb, wu_bf[c:, :])
        o_ref[...] = (g * (1.0 / (1.0 + jnp.exp(-g))) * u).astype(o_ref.dtype)

    @pl.when(valid_ref[i] == 0)
    def _():
        o_ref[...] = jnp.zeros_like(o_ref)


def _tile_or_first(i, valid_ref):
    return jnp.where(valid_ref[i] == 1, i, 0)


def _moe_up(xs, w_gate, w_up, layer, tile_expert, fresh, valid, tm):
    p, c = xs.shape
    _, _, d, f = w_gate.shape
    w_spec = pl.BlockSpec((None, None, d, f), lambda i, te, fr, va: (layer, te[i], 0, 0))
    return pl.pallas_call(
        _moe_up_kernel,
        out_shape=jax.ShapeDtypeStruct((p, f), BF16),
        grid_spec=pltpu.PrefetchScalarGridSpec(
            num_scalar_prefetch=3,
            grid=(p // tm,),
            in_specs=[pl.BlockSpec((tm, c), lambda i, te, fr, va: (_tile_or_first(i, va), 0)), w_spec, w_spec],
            out_specs=pl.BlockSpec((tm, f), lambda i, te, fr, va: (i, 0)),
            scratch_shapes=[pltpu.VMEM((d, f), BF16), pltpu.VMEM((d, f), BF16)],
        ),
        compiler_params=_params(("arbitrary",)),
        name="moe_up",
    )(tile_expert, fresh, valid, xs, w_gate, w_up)


def _moe_down_kernel(te_ref, fresh_ref, valid_ref, a_ref, wd_ref, o_ref, wd_bf):
    i = pl.program_id(0)

    @pl.when(fresh_ref[i] == 1)
    def _():
        wd_bf[...] = wd_ref[...].astype(BF16)

    @pl.when(valid_ref[i] == 1)
    def _():
        o_ref[...] = _pack_bf16_pairs(_dot(a_ref[...], wd_bf[...]))

    @pl.when(valid_ref[i] == 0)
    def _():
        o_ref[...] = jnp.zeros_like(o_ref)


def _moe_down(act, w_down, layer, tile_expert, fresh, valid, tm):
    p, f = act.shape
    d = w_down.shape[3]
    return pl.pallas_call(
        _moe_down_kernel,
        out_shape=jax.ShapeDtypeStruct((p, d // 2), jnp.uint32),
        grid_spec=pltpu.PrefetchScalarGridSpec(
            num_scalar_prefetch=3,
            grid=(p // tm,),
            in_specs=[pl.BlockSpec((tm, f), lambda i, te, fr, va: (_tile_or_first(i, va), 0)),
                      pl.BlockSpec((None, None, f, d), lambda i, te, fr, va: (layer, te[i], 0, 0))],
            out_specs=pl.BlockSpec((tm, d // 2), lambda i, te, fr, va: (i, 0)),
            scratch_shapes=[pltpu.VMEM((f, d), BF16)],
        ),
        compiler_params=_params(("arbitrary",)),
        name="moe_down",
    )(tile_expert, fresh, valid, act, w_down)


def _combine_ln_kernel(slot_ref, h_ref, info_ref, g_ref, b_ref, ys_hbm, o_ref, ob_ref, buf, sems, *, tm, alpha):
    i = pl.program_id(0)

    def row_copy(s, par, r):
        return pltpu.make_async_copy(ys_hbm.at[pl.ds(s, 1), :], buf.at[par, pl.ds(r, 1), :], sems.at[par])

    def gather_tile(tile, par):
        def issue(r, _):
            t = tile * tm + r
            row_copy(slot_ref[2 * t], par, r).start()
            row_copy(slot_ref[2 * t + 1], par, tm + r).start()
            return 0
        lax.fori_loop(0, tm, issue, 0)

    @pl.when(i == 0)
    def _():
        gather_tile(0, 0)

    @pl.when(i + 1 < pl.num_programs(0))
    def _():
        gather_tile(i + 1, (i + 1) % 2)

    par = i % 2
    pltpu.make_async_copy(ys_hbm.at[pl.ds(0, 2 * tm), :], buf.at[par], sems.at[par]).wait()

    info = info_ref[...]
    w1, w2 = info[:, 4:5], info[:, 5:6]
    a1, b1 = _unpack_bf16_pairs(buf[par, :tm, :])
    a2, b2 = _unpack_bf16_pairs(buf[par, tm:, :])
    ffn = jnp.concatenate([w1 * a1 + w2 * a2, w1 * b1 + w2 * b2], axis=1)
    y = _layer_norm_rows(alpha * h_ref[...] + ffn, g_ref[...], b_ref[...])
    o_ref[...] = y
    ob_ref[...] = y.astype(BF16)


def _combine_ln(h, info, ys, slots, g, b, alpha):
    n, d = h.shape
    tm = _tile(n, 256)
    return pl.pallas_call(
        functools.partial(_combine_ln_kernel, tm=tm, alpha=alpha),
        out_shape=(jax.ShapeDtypeStruct((n, d), F32), jax.ShapeDtypeStruct((n, d), BF16)),
        grid_spec=pltpu.PrefetchScalarGridSpec(
            num_scalar_prefetch=1,
            grid=(n // tm,),
            in_specs=[pl.BlockSpec((tm, d), lambda i, s: (i, 0)),
                      pl.BlockSpec((tm, LANES), lambda i, s: (i, 0)),
                      pl.BlockSpec((1, d), lambda i, s: (0, 0)),
                      pl.BlockSpec((1, d), lambda i, s: (0, 0)),
                      pl.BlockSpec(memory_space=pl.ANY)],
            out_specs=(pl.BlockSpec((tm, d), lambda i, s: (i, 0)),
                       pl.BlockSpec((tm, d), lambda i, s: (i, 0))),
            scratch_shapes=[pltpu.VMEM((2, 2 * tm, d // 2), jnp.uint32), pltpu.SemaphoreType.DMA((2,))],
        ),
        compiler_params=_params(("arbitrary",)),
        name="moe_combine_ln",
    )(slots, h, info, g.reshape(1, d), b.reshape(1, d), ys)


def _moe_ln(h, hp, layer, w_group, w_inner, w_gate, w_up, w_down, ln_g, ln_b, alpha):
    n, d = h.shape
    tm = _tile(2 * n // N_EXPERTS, 256)
    n_rows = 2 * n + N_EXPERTS * tm
    n_tiles = n_rows // tm

    w_route = jnp.concatenate(
        [w_group, w_inner.transpose(1, 0, 2).reshape(d, N_EXPERTS),
         jnp.zeros((d, LANES - N_GROUPS - N_EXPERTS), F32)], axis=1)
    info, cnt = _route(h, w_route)

    expert = info[:, 0:2].astype(jnp.int32)
    rank = info[:, 2:4].astype(jnp.int32)
    counts = cnt[0, ROUTE_LANE0:ROUTE_LANE0 + N_EXPERTS].astype(jnp.int32)
    padded = (counts + tm - 1) // tm * tm
    ends = jnp.cumsum(padded)
    starts = ends - padded
    slots = (starts[expert] + rank).reshape(-1)
    pad_r = jnp.arange(tm, dtype=jnp.int32)[None, :]
    pad_slots = jnp.where(pad_r < (padded - counts)[:, None], (starts + counts)[:, None] + pad_r, -1).reshape(-1)
    tile_ids = jnp.arange(n_tiles, dtype=jnp.int32)
    tile_expert = jnp.minimum(jnp.sum((tile_ids[:, None] >= (ends // tm)[None, :]).astype(jnp.int32), axis=1),
                              N_EXPERTS - 1)
    valid = (tile_ids < ends[-1] // tm).astype(jnp.int32)
    fresh = jnp.concatenate([jnp.ones((1,), jnp.int32),
                             (tile_expert[1:] != tile_expert[:-1]).astype(jnp.int32)])

    xs = _dispatch(hp, slots, pad_slots, (ends[-1:] // tm).astype(jnp.int32), tm)
    act = _moe_up(xs, w_gate, w_up, layer, tile_expert, fresh, valid, tm)
    ys = _moe_down(act, w_down, layer, tile_expert, fresh, valid, tm)
    return _combine_ln(h, info, ys, slots, ln_g, ln_b, alpha)


def _rope_tables(seq):
    half = DIFF_QK_DIM // 2
    lane = jnp.arange(LANES)
    inv_freq = ROPE_THETA ** (-(lane % half).astype(F32) / half)
    ang = jnp.arange(seq, dtype=F32)[:, None] * inv_freq[None, :]
    sign = jnp.where((lane % DIFF_QK_DIM) < half, -1.0, 1.0)
    return jnp.cos(ang), jnp.sin(ang) * sign


def kernel(x, sb_w_qkv, sb_w_o, shared_w_kv, diff_w_q, diff_w_o, diff_lambda, diff_subln_g, ln_mix_g, ln_mix_b,
           ln_ffn_g, ln_ffn_b, moe_w_group, moe_w_inner, moe_w_gate, moe_w_up, moe_w_down):
    batch, seq, d = x.shape
    n = batch * seq
    depth = ln_mix_g.shape[0]
    n_sb = sb_w_qkv.shape[0]
    alpha = (2.0 * depth) ** 0.25
    rope_tabs = _rope_tables(seq)

    h = x.reshape(n, d)
    hb = h.astype(BF16)
    kv = None
    for layer in range(depth):
        if layer < n_sb:
            qkv = _proj(hb, sb_w_qkv[layer].astype(BF16), scale=HEAD_DIM ** -0.5, scale_cols=d)
            o = _sb_attention(qkv, batch, seq)
            w_o = sb_w_o[layer]
        else:
            i = layer - n_sb
            if kv is None:
                kv = _proj(hb, shared_w_kv.astype(BF16), rope_cols=d, rope_tabs=rope_tabs, seq=seq)
            qd = _proj(hb, diff_w_q[i].astype(BF16), scale=DIFF_QK_DIM ** -0.5, scale_cols=d,
                       rope_cols=d, rope_tabs=rope_tabs, seq=seq)
            lam_init = 0.8 - 0.6 * math.exp(-0.3 * layer)
            o = _diff_attention(qd, kv, diff_lambda[i], diff_subln_g[i], lam_init, batch, seq)
            w_o = diff_w_o[i]
        h, hp = _proj_ln(o, w_o.astype(BF16), h, ln_mix_g[layer], ln_mix_b[layer], alpha)
        h, hb = _moe_ln(h, hp, layer, moe_w_group[layer], moe_w_inner[layer], moe_w_gate, moe_w_up,
                        moe_w_down, ln_ffn_g[layer], ln_ffn_b[layer], alpha)
    return h.reshape(batch, seq, d)
```

```python
import functools
import math

import jax
import jax.numpy as jnp
from jax import lax
from jax.experimental import pallas as pl
from jax.experimental.pallas import tpu as pltpu

HEAD_DIM = 128
DIFF_QK_DIM = HEAD_DIM // 2
N_GROUPS = 4
EXPERTS_PER_GROUP = 4
N_EXPERTS = N_GROUPS * EXPERTS_PER_GROUP
ROPE_THETA = 10000.0
LN_EPS = 1e-5
SUBLN_EPS = 1e-5
LANES = 128
ROUTE_LANE0 = N_GROUPS
NEG_BIG = -1e30
SB_SKIP_BELOW = -150.0
LOG2_E = 1.0 / math.log(2.0)
DMA_ISSUE_UNROLL = 8
VMEM_LIMIT = 56 * 1024 * 1024

F32 = jnp.float32
BF16 = jnp.bfloat16


def _tile(n, pref):
    t = min(n, pref)
    assert n % t == 0, (n, pref)
    return t


def _params(sem):
    return pltpu.CompilerParams(dimension_semantics=sem, vmem_limit_bytes=VMEM_LIMIT)


def _dot(a, b):
    return jnp.dot(a, b, preferred_element_type=F32)


def _dot_nt(a, b):
    return lax.dot_general(a, b, (((1,), (1,)), ((), ())), preferred_element_type=F32)


def _rope_tile(acc, cos, sin):
    lane = lax.broadcasted_iota(jnp.int32, cos.shape, 1)
    first_half = (lane % DIFF_QK_DIM) < (DIFF_QK_DIM // 2)
    outs = []
    for g in range(acc.shape[1] // LANES):
        xg = acc[:, g * LANES:(g + 1) * LANES]
        up = pltpu.roll(xg, LANES - DIFF_QK_DIM // 2, axis=1)
        dn = pltpu.roll(xg, DIFF_QK_DIM // 2, axis=1)
        partner = jnp.where(first_half, up, dn)
        outs.append(xg * cos + partner * sin)
    return jnp.concatenate(outs, axis=1)


def _proj_kernel(x_ref, w_ref, *rest, scale, scale_tiles, rope_tiles):
    if rope_tiles:
        cos_ref, sin_ref, o_ref = rest
    else:
        (o_ref,) = rest
    j = pl.program_id(0)
    acc = _dot(x_ref[...], w_ref[...])
    if scale_tiles:
        acc = acc * jnp.where(j < scale_tiles, jnp.float32(scale), jnp.float32(1.0))
    if rope_tiles:
        @pl.when(j < rope_tiles)
        def _():
            o_ref[...] = _rope_tile(acc, cos_ref[...], sin_ref[...]).astype(o_ref.dtype)

        @pl.when(j >= rope_tiles)
        def _():
            o_ref[...] = acc.astype(o_ref.dtype)
    else:
        o_ref[...] = acc.astype(o_ref.dtype)


def _proj(x, w, *, scale=1.0, scale_cols=0, rope_cols=0, rope_tabs=None, seq=None):
    n, k = x.shape
    m = w.shape[1]
    tm = _tile(n, 512)
    tn = math.gcd(1024, m, scale_cols, rope_cols)
    assert tn % LANES == 0
    in_specs = [pl.BlockSpec((tm, k), lambda j, i: (i, 0)),
                pl.BlockSpec((k, tn), lambda j, i: (0, j))]
    args = [x, w]
    if rope_cols:
        assert seq % tm == 0
        spb = seq // tm
        in_specs += [pl.BlockSpec((tm, LANES), lambda j, i: (i % spb, 0))] * 2
        args += list(rope_tabs)
    kern = functools.partial(_proj_kernel, scale=scale, scale_tiles=scale_cols // tn,
                             rope_tiles=rope_cols // tn)
    return pl.pallas_call(
        kern,
        out_shape=jax.ShapeDtypeStruct((n, m), BF16),
        grid=(m // tn, n // tm),
        in_specs=in_specs,
        out_specs=pl.BlockSpec((tm, tn), lambda j, i: (i, j)),
        compiler_params=_params(("arbitrary", "arbitrary")),
        name="proj",
    )(*args)


def _layer_norm_rows(v, g, b):
    mu = jnp.mean(v, axis=-1, keepdims=True)
    c = v - mu
    var = jnp.mean(c * c, axis=-1, keepdims=True)
    return c * lax.rsqrt(var + LN_EPS) * g + b


def _pack_bf16_pairs(y):
    c = y.shape[1] // 2
    bits = lax.bitcast_convert_type(y.astype(BF16).astype(F32), jnp.uint32)
    return bits[:, :c] | (bits[:, c:] >> 16)


def _unpack_bf16_pairs(w):
    return (lax.bitcast_convert_type(w & jnp.uint32(0xFFFF0000), F32),
            lax.bitcast_convert_type(w << 16, F32))


def _proj_ln_kernel(x_ref, w_ref, res_ref, g_ref, b_ref, o_ref, op_ref, *, alpha):
    acc = _dot(x_ref[...], w_ref[...])
    y = _layer_norm_rows(alpha * res_ref[...] + acc, g_ref[...], b_ref[...])
    o_ref[...] = y
    op_ref[...] = _pack_bf16_pairs(y)


def _proj_ln(x, w, res, g, b, alpha):
    n, k = x.shape
    d = w.shape[1]
    tm = _tile(n, 256)
    return pl.pallas_call(
        functools.partial(_proj_ln_kernel, alpha=alpha),
        out_shape=(jax.ShapeDtypeStruct((n, d), F32), jax.ShapeDtypeStruct((n, d // 2), jnp.uint32)),
        grid=(n // tm,),
        in_specs=[pl.BlockSpec((tm, k), lambda i: (i, 0)),
                  pl.BlockSpec((k, d), lambda i: (0, 0)),
                  pl.BlockSpec((tm, d), lambda i: (i, 0)),
                  pl.BlockSpec((1, d), lambda i: (0, 0)),
                  pl.BlockSpec((1, d), lambda i: (0, 0))],
        out_specs=(pl.BlockSpec((tm, d), lambda i: (i, 0)),
                   pl.BlockSpec((tm, d // 2), lambda i: (i, 0))),
        compiler_params=_params(("arbitrary",)),
        name="proj_ln",
    )(x, w, res, g.reshape(1, d), b.reshape(1, d))


def _sb_attn_kernel(q_ref, k_ref, v_ref, o_ref, *, t, hb):
    qi = pl.program_id(2)
    row = lax.broadcasted_iota(jnp.int32, (2 * t, t), 0) % t
    col = lax.broadcasted_iota(jnp.int32, (2 * t, t), 1)
    later = (row > col).astype(BF16)
    past = (lax.broadcasted_iota(jnp.int32, (t, t), 1)
            < lax.broadcasted_iota(jnp.int32, (t, t), 0))

    def step(kc, accs, tails, masked):
        start = pl.multiple_of(kc * t, t)
        cols = [slice(h * HEAD_DIM, (h + 1) * HEAD_DIM) for h in range(hb)]
        zs = [_dot_nt(q_ref[:, c], k_ref[pl.ds(start, t), c]) for c in cols]
        log_betas, log_keeps, splits = [], [], []
        for z in zs:
            nl = -jnp.log2(1.0 + jnp.exp2(-jnp.abs(z)))
            log_betas.append(nl + jnp.minimum(z, 0.0))
            log_keep = nl - jnp.maximum(z, 0.0)
            if masked:
                log_keep = jnp.where(past, log_keep, 0.0)
            hi = log_keep.astype(BF16)
            lo = (log_keep - hi.astype(F32)).astype(BF16)
            log_keeps.append(log_keep)
            splits.append(jnp.concatenate([hi, lo], axis=1))
        tail_ins = [_dot(s, later) for s in splits]
        ws = []
        for h in range(hb):
            w = jnp.exp2(log_betas[h] + tail_ins[h] + tails[h])
            if masked:
                w = jnp.where(past, w, 0.0)
            ws.append(w.astype(BF16))
        outs = [_dot(ws[h], v_ref[pl.ds(start, t), cols[h]]) for h in range(hb)]
        accs = tuple(a + o for a, o in zip(accs, outs))
        tails = tuple(s + jnp.sum(lk, axis=1, keepdims=True) for s, lk in zip(tails, log_keeps))
        return accs, tails

    def any_weight_left(tails):
        worst = functools.reduce(jnp.maximum, tails)
        return (jnp.max(worst) >= SB_SKIP_BELOW).astype(jnp.int32)

    accs = tuple(jnp.zeros((t, HEAD_DIM), F32) for _ in range(hb))
    tails = tuple(jnp.zeros((t, 1), F32) for _ in range(hb))
    accs, tails = step(qi, accs, tails, True)

    def cond(c):
        return jnp.logical_and(c[0] >= 0, c[1] > 0)

    def body(c):
        kc, _, accs, tails = c
        accs, tails = step(kc, accs, tails, False)
        return kc - 1, any_weight_left(tails), accs, tails

    _, _, accs, _ = lax.while_loop(cond, body, (qi - 1, any_weight_left(tails), accs, tails))
    for h in range(hb):
        o_ref[:, h * HEAD_DIM:(h + 1) * HEAD_DIM] = accs[h].astype(o_ref.dtype)


def _sb_attention(qkv, batch, seq):
    n, three_d = qkv.shape
    d = three_d // 3
    heads = d // HEAD_DIM
    hb = math.gcd(heads, 8)
    t = _tile(seq, 128)
    nq = seq // t
    nh = heads // hb
    w = hb * HEAD_DIM
    return pl.pallas_call(
        functools.partial(_sb_attn_kernel, t=t, hb=hb),
        out_shape=jax.ShapeDtypeStruct((n, d), BF16),
        grid=(batch, nh, nq),
        in_specs=[pl.BlockSpec((t, w), lambda b, h, i: (b * nq + i, h)),
                  pl.BlockSpec((seq, w), lambda b, h, i: (b, nh + h)),
                  pl.BlockSpec((seq, w), lambda b, h, i: (b, 2 * nh + h))],
        out_specs=pl.BlockSpec((t, w), lambda b, h, i: (b * nq + i, h)),
        compiler_params=_params(("arbitrary", "arbitrary", "arbitrary")),
        name="sb_attn",
    )(qkv, qkv, qkv)


def _diff_attn_kernel(q_ref, k_ref, v_ref, lam_ref, g_ref, o_ref, *, t, lam_init):
    qi = pl.program_id(2)
    q = q_ref[...]
    lane = lax.broadcasted_iota(jnp.int32, (t, HEAD_DIM), 1)
    zero = jnp.zeros_like(q)
    q1 = jnp.where(lane < DIFF_QK_DIM, q, zero)
    q2 = jnp.where(lane >= DIFF_QK_DIM, q, zero)
    q_pos = lax.broadcasted_iota(jnp.int32, (t, t), 0)
    k_pos = lax.broadcasted_iota(jnp.int32, (t, t), 1)
    causal = k_pos <= q_pos

    def step(kc, carry, masked):
        start = pl.multiple_of(kc * t, t)
        k = k_ref[pl.ds(start, t), :]
        v = v_ref[pl.ds(start, t), :]
        scores = [_dot_nt(qx, k) for qx in (q1, q2)]
        out = []
        for s, (m, l, acc) in zip(scores, carry):
            if masked:
                s = jnp.where(causal, s, NEG_BIG)
            m_new = jnp.maximum(m, jnp.max(s, axis=1, keepdims=True))
            p = jnp.exp2(s - m_new)
            corr = jnp.exp2(m - m_new)
            l = corr * l + jnp.sum(p, axis=1, keepdims=True)
            acc = corr * acc + _dot(p.astype(BF16), v)
            out.append((m_new, l, acc))
        return tuple(out)

    init = (jnp.full((t, 1), NEG_BIG, F32), jnp.zeros((t, 1), F32), jnp.zeros((t, HEAD_DIM), F32))
    carry = step(qi, (init, init), True)
    carry = lax.fori_loop(0, qi, lambda c, s: step(c, s, False), carry)
    (_, l1, a1), (_, l2, a2) = carry

    lp = lam_ref[...]
    lam = (jnp.exp(jnp.sum(lp[0:1] * lp[1:2], axis=1, keepdims=True))
           - jnp.exp(jnp.sum(lp[2:3] * lp[3:4], axis=1, keepdims=True)) + lam_init)
    o = a1 / l1 - lam * (a2 / l2)
    o = o * lax.rsqrt(jnp.mean(o * o, axis=1, keepdims=True) + SUBLN_EPS)
    o_ref[...] = (o * g_ref[...] * (1.0 - lam_init)).astype(o_ref.dtype)


def _diff_attention(qd, kv, lam_params, subln_g, lam_init, batch, seq):
    n, d = qd.shape
    heads = d // HEAD_DIM
    t = _tile(seq, 512)
    nq = seq // t
    return pl.pallas_call(
        functools.partial(_diff_attn_kernel, t=t, lam_init=lam_init),
        out_shape=jax.ShapeDtypeStruct((n, d), BF16),
        grid=(batch, heads, nq),
        in_specs=[pl.BlockSpec((t, HEAD_DIM), lambda b, h, i: (b * nq + i, h)),
                  pl.BlockSpec((seq, HEAD_DIM), lambda b, h, i: (b, h)),
                  pl.BlockSpec((seq, HEAD_DIM), lambda b, h, i: (b, heads + h)),
                  pl.BlockSpec(lam_params.shape, lambda b, h, i: (0, 0)),
                  pl.BlockSpec((1, HEAD_DIM), lambda b, h, i: (0, 0))],
        out_specs=pl.BlockSpec((t, HEAD_DIM), lambda b, h, i: (b * nq + i, h)),
        compiler_params=_params(("arbitrary", "arbitrary", "arbitrary")),
        name="diff_attn",
    )(qd, kv, kv, lam_params, subln_g.reshape(1, HEAD_DIM))


def _split_bf16(x):
    hi = x.astype(BF16)
    return hi, (x - hi.astype(F32)).astype(BF16)


def _route_kernel(h_ref, w_ref, info_ref, cnt_ref, *, tm):
    i = pl.program_id(0)

    @pl.when(i == 0)
    def _():
        cnt_ref[...] = jnp.zeros_like(cnt_ref)

    h_hi, h_lo = _split_bf16(h_ref[...])
    w_hi, w_lo = _split_bf16(w_ref[...])
    logits = _dot(h_hi, w_hi) + (_dot(h_hi, w_lo) + _dot(h_lo, w_hi))

    lane = lax.broadcasted_iota(jnp.int32, (tm, LANES), 1).astype(F32)

    def first_max(mask, vals):
        top = jnp.max(jnp.where(mask, vals, NEG_BIG), axis=1, keepdims=True)
        idx = jnp.min(jnp.where(mask & (vals == top), lane, float(LANES)), axis=1, keepdims=True)
        return top, idx

    g_mask = lane < N_GROUPS
    g_top, g_sel = first_max(g_mask, logits)
    g_den = jnp.sum(jnp.where(g_mask, jnp.exp(logits - g_top), 0.0), axis=1, keepdims=True)
    g_w = 1.0 / g_den

    lo_lane = ROUTE_LANE0 + EXPERTS_PER_GROUP * g_sel
    in_group = (lane >= lo_lane) & (lane < lo_lane + EXPERTS_PER_GROUP)
    v1, i1 = first_max(in_group, logits)
    v2, i2 = first_max(in_group & (lane != i1), logits)
    ex = jnp.exp(v2 - v1)
    w1 = g_w / (1.0 + ex)
    w2 = w1 * ex

    m1 = (lane == i1).astype(F32)
    m2 = (lane == i2).astype(F32)
    both = m1 + m2
    r = lax.broadcasted_iota(jnp.int32, (tm, tm), 0)
    c = lax.broadcasted_iota(jnp.int32, (tm, tm), 1)
    earlier = (c < r).astype(BF16)
    before = _dot(earlier, both.astype(BF16)) + cnt_ref[0:1, :]
    rank1 = jnp.sum(m1 * before, axis=1, keepdims=True)
    rank2 = jnp.sum(m2 * before, axis=1, keepdims=True)
    cnt_ref[...] = cnt_ref[...] + jnp.sum(both, axis=0, keepdims=True)

    info = jnp.where(lane == 0, i1 - ROUTE_LANE0, 0.0)
    info = jnp.where(lane == 1, i2 - ROUTE_LANE0, info)
    info = jnp.where(lane == 2, rank1, info)
    info = jnp.where(lane == 3, rank2, info)
    info = jnp.where(lane == 4, w1, info)
    info = jnp.where(lane == 5, w2, info)
    info_ref[...] = info


def _route(h, w_route):
    n, d = h.shape
    tm = _tile(n, 512)
    return pl.pallas_call(
        functools.partial(_route_kernel, tm=tm),
        out_shape=(jax.ShapeDtypeStruct((n, LANES), F32), jax.ShapeDtypeStruct((8, LANES), F32)),
        grid=(n // tm,),
        in_specs=[pl.BlockSpec((tm, d), lambda i: (i, 0)),
                  pl.BlockSpec((d, LANES), lambda i: (0, 0))],
        out_specs=(pl.BlockSpec((tm, LANES), lambda i: (i, 0)),
                   pl.BlockSpec((8, LANES), lambda i: (0, 0))),
        compiler_params=_params(("arbitrary",)),
        name="route",
    )(h, w_route)


def _dispatch_kernel(slot_ref, pad_ref, used_ref, h_ref, xs_hbm, zrow, sem, pad_sem, *, tm, n_pad, n_tiles):
    i = pl.program_id(0)
    base = i * tm

    def row_copy(r, s):
        return pltpu.make_async_copy(h_ref.at[pl.ds(r, 1), :], xs_hbm.at[pl.ds(s, 1), :], sem)

    def issue(r, _):
        t = base + r
        row_copy(r, slot_ref[2 * t]).start()
        row_copy(r, slot_ref[2 * t + 1]).start()
        return 0

    lax.fori_loop(0, tm, issue, 0, unroll=DMA_ISSUE_UNROLL)

    def pad_copy(s):
        return pltpu.make_async_copy(zrow.at[pl.ds(0, 1), :], xs_hbm.at[pl.ds(s, 1), :], pad_sem)

    @pl.when(i == 0)
    def _():
        zrow[...] = jnp.zeros_like(zrow)

        def issue_pad(r, _):
            @pl.when(pad_ref[r] >= 0)
            def _():
                pad_copy(pad_ref[r]).start()
            return 0

        def drain_pad(r, _):
            @pl.when(pad_ref[r] >= 0)
            def _():
                pad_copy(0).wait()
            return 0

        def tile_copy(t):
            return pltpu.make_async_copy(zrow, xs_hbm.at[pl.ds(pl.multiple_of(t * tm, tm), tm), :], pad_sem)

        def issue_tile(t, _):
            tile_copy(t).start()
            return 0

        def drain_tile(t, _):
            tile_copy(t).wait()
            return 0

        lax.fori_loop(0, n_pad, issue_pad, 0)
        lax.fori_loop(used_ref[0], n_tiles, issue_tile, 0)
        lax.fori_loop(0, n_pad, drain_pad, 0)
        lax.fori_loop(used_ref[0], n_tiles, drain_tile, 0)

    whole_tile = pltpu.make_async_copy(h_ref, xs_hbm.at[pl.ds(0, tm), :], sem)
    whole_tile.wait()
    whole_tile.wait()


def _dispatch(hp, slots, pad_slots, tiles_used, tm):
    n, c = hp.shape
    n_tiles = 2 * n // tm + N_EXPERTS
    assert n % tm == 0
    return pl.pallas_call(
        functools.partial(_dispatch_kernel, tm=tm, n_pad=pad_slots.shape[0], n_tiles=n_tiles),
        out_shape=jax.ShapeDtypeStruct((n_tiles * tm, c), hp.dtype),
        grid_spec=pltpu.PrefetchScalarGridSpec(
            num_scalar_prefetch=3,
            grid=(n // tm,),
            in_specs=[pl.BlockSpec((tm, c), lambda i, s, p, u: (i, 0))],
            out_specs=pl.BlockSpec(memory_space=pl.ANY),
            scratch_shapes=[pltpu.VMEM((tm, c), hp.dtype), pltpu.SemaphoreType.DMA(()),
                            pltpu.SemaphoreType.DMA(())],
        ),
        compiler_params=_params(("arbitrary",)),
        name="moe_dispatch",
    )(slots, pad_slots, tiles_used, hp)


def _moe_up_kernel(te_ref, fresh_ref, valid_ref, x_ref, wg_ref, wu_ref, o_ref, wg_bf, wu_bf):
    i = pl.program_id(0)
    c = x_ref.shape[1]

    @pl.when(fresh_ref[i] == 1)
    def _():
        wg_bf[...] = wg_ref[...].astype(BF16)
        wu_bf[...] = wu_ref[...].astype(BF16)

    @pl.when(valid_ref[i] == 1)
    def _():
        xa, xb = _unpack_bf16_pairs(x_ref[...])
        xa, xb = xa.astype(BF16), xb.astype(BF16)
        g = _dot(xa, wg_bf[:c, :]) + _dot(xb, wg_bf[c:, :])
        u = _dot(xa, wu_bf[:c, :]) + _dot(xb, wu_bf[c:, :])
        o_ref[...] = (g * (1.0 / (1.0 + jnp.exp(-g))) * u).astype(o_ref.dtype)

    @pl.when(valid_ref[i] == 0)
    def _():
        o_ref[...] = jnp.zeros_like(o_ref)


def _tile_or_first(i, valid_ref):
    return jnp.where(valid_ref[i] == 1, i, 0)


def _moe_up(xs, w_gate, w_up, layer, tile_expert, fresh, valid, tm):
    p, c = xs.shape
    _, _, d, f = w_gate.shape
    w_spec = pl.BlockSpec((None, None, d, f), lambda i, te, fr, va: (layer, te[i], 0, 0))
    return pl.pallas_call(
        _moe_up_kernel,
        out_shape=jax.ShapeDtypeStruct((p, f), BF16),
        grid_spec=pltpu.PrefetchScalarGridSpec(
            num_scalar_prefetch=3,
            grid=(p // tm,),
            in_specs=[pl.BlockSpec((tm, c), lambda i, te, fr, va: (_tile_or_first(i, va), 0)), w_spec, w_spec],
            out_specs=pl.BlockSpec((tm, f), lambda i, te, fr, va: (i, 0)),
            scratch_shapes=[pltpu.VMEM((d, f), BF16), pltpu.VMEM((d, f), BF16)],
        ),
        compiler_params=_params(("arbitrary",)),
        name="moe_up",
    )(tile_expert, fresh, valid, xs, w_gate, w_up)


def _moe_down_kernel(te_ref, fresh_ref, valid_ref, a_ref, wd_ref, o_ref, wd_bf):
    i = pl.program_id(0)

    @pl.when(fresh_ref[i] == 1)
    def _():
        wd_bf[...] = wd_ref[...].astype(BF16)

    @pl.when(valid_ref[i] == 1)
    def _():
        o_ref[...] = _pack_bf16_pairs(_dot(a_ref[...], wd_bf[...]))

    @pl.when(valid_ref[i] == 0)
    def _():
        o_ref[...] = jnp.zeros_like(o_ref)


def _moe_down(act, w_down, layer, tile_expert, fresh, valid, tm):
    p, f = act.shape
    d = w_down.shape[3]
    return pl.pallas_call(
        _moe_down_kernel,
        out_shape=jax.ShapeDtypeStruct((p, d // 2), jnp.uint32),
        grid_spec=pltpu.PrefetchScalarGridSpec(
            num_scalar_prefetch=3,
            grid=(p // tm,),
            in_specs=[pl.BlockSpec((tm, f), lambda i, te, fr, va: (_tile_or_first(i, va), 0)),
                      pl.BlockSpec((None, None, f, d), lambda i, te, fr, va: (layer, te[i], 0, 0))],
            out_specs=pl.BlockSpec((tm, d // 2), lambda i, te, fr, va: (i, 0)),
            scratch_shapes=[pltpu.VMEM((f, d), BF16)],
        ),
        compiler_params=_params(("arbitrary",)),
        name="moe_down",
    )(tile_expert, fresh, valid, act, w_down)


def _combine_ln_kernel(slot_ref, h_ref, info_ref, g_ref, b_ref, ys_hbm, o_ref, ob_ref, buf, sems, *, tm, alpha):
    i = pl.program_id(0)

    def row_copy(s, par, r):
        return pltpu.make_async_copy(ys_hbm.at[pl.ds(s, 1), :], buf.at[par, pl.ds(r, 1), :], sems.at[par])

    def gather_tile(tile, par):
        def issue(r, _):
            t = tile * tm + r
            row_copy(slot_ref[2 * t], par, r).start()
            row_copy(slot_ref[2 * t + 1], par, tm + r).start()
            return 0
        lax.fori_loop(0, tm, issue, 0, unroll=DMA_ISSUE_UNROLL)

    @pl.when(i == 0)
    def _():
        gather_tile(0, 0)

    @pl.when(i + 1 < pl.num_programs(0))
    def _():
        gather_tile(i + 1, (i + 1) % 2)

    par = i % 2
    pltpu.make_async_copy(ys_hbm.at[pl.ds(0, 2 * tm), :], buf.at[par], sems.at[par]).wait()

    info = info_ref[...]
    w1, w2 = info[:, 4:5], info[:, 5:6]
    a1, b1 = _unpack_bf16_pairs(buf[par, :tm, :])
    a2, b2 = _unpack_bf16_pairs(buf[par, tm:, :])
    ffn = jnp.concatenate([w1 * a1 + w2 * a2, w1 * b1 + w2 * b2], axis=1)
    y = _layer_norm_rows(alpha * h_ref[...] + ffn, g_ref[...], b_ref[...])
    o_ref[...] = y
    ob_ref[...] = y.astype(BF16)


def _combine_ln(h, info, ys, slots, g, b, alpha):
    n, d = h.shape
    tm = _tile(n, 256)
    return pl.pallas_call(
        functools.partial(_combine_ln_kernel, tm=tm, alpha=alpha),
        out_shape=(jax.ShapeDtypeStruct((n, d), F32), jax.ShapeDtypeStruct((n, d), BF16)),
        grid_spec=pltpu.PrefetchScalarGridSpec(
            num_scalar_prefetch=1,
            grid=(n // tm,),
            in_specs=[pl.BlockSpec((tm, d), lambda i, s: (i, 0)),
                      pl.BlockSpec((tm, LANES), lambda i, s: (i, 0)),
                      pl.BlockSpec((1, d), lambda i, s: (0, 0)),
                      pl.BlockSpec((1, d), lambda i, s: (0, 0)),
                      pl.BlockSpec(memory_space=pl.ANY)],
            out_specs=(pl.BlockSpec((tm, d), lambda i, s: (i, 0)),
                       pl.BlockSpec((tm, d), lambda i, s: (i, 0))),
            scratch_shapes=[pltpu.VMEM((2, 2 * tm, d // 2), jnp.uint32), pltpu.SemaphoreType.DMA((2,))],
        ),
        compiler_params=_params(("arbitrary",)),
        name="moe_combine_ln",
    )(slots, h, info, g.reshape(1, d), b.reshape(1, d), ys)


def _moe_ln(h, hp, layer, w_group, w_inner, w_gate, w_up, w_down, ln_g, ln_b, alpha):
    n, d = h.shape
    tm = _tile(2 * n // N_EXPERTS, 256)
    n_rows = 2 * n + N_EXPERTS * tm
    n_tiles = n_rows // tm

    w_route = jnp.concatenate(
        [w_group, w_inner.transpose(1, 0, 2).reshape(d, N_EXPERTS),
         jnp.zeros((d, LANES - N_GROUPS - N_EXPERTS), F32)], axis=1)
    info, cnt = _route(h, w_route)

    expert = info[:, 0:2].astype(jnp.int32)
    rank = info[:, 2:4].astype(jnp.int32)
    counts = cnt[0, ROUTE_LANE0:ROUTE_LANE0 + N_EXPERTS].astype(jnp.int32)
    padded = (counts + tm - 1) // tm * tm
    ends = jnp.cumsum(padded)
    starts = ends - padded
    slots = (starts[expert] + rank).reshape(-1)
    pad_r = jnp.arange(tm, dtype=jnp.int32)[None, :]
    pad_slots = jnp.where(pad_r < (padded - counts)[:, None], (starts + counts)[:, None] + pad_r, -1).reshape(-1)
    tile_ids = jnp.arange(n_tiles, dtype=jnp.int32)
    tile_expert = jnp.minimum(jnp.sum((tile_ids[:, None] >= (ends // tm)[None, :]).astype(jnp.int32), axis=1),
                              N_EXPERTS - 1)
    valid = (tile_ids < ends[-1] // tm).astype(jnp.int32)
    fresh = jnp.concatenate([jnp.ones((1,), jnp.int32),
                             (tile_expert[1:] != tile_expert[:-1]).astype(jnp.int32)])

    xs = _dispatch(hp, slots, pad_slots, (ends[-1:] // tm).astype(jnp.int32), tm)
    act = _moe_up(xs, w_gate, w_up, layer, tile_expert, fresh, valid, tm)
    ys = _moe_down(act, w_down, layer, tile_expert, fresh, valid, tm)
    return _combine_ln(h, info, ys, slots, ln_g, ln_b, alpha)


def _rope_tables(seq):
    half = DIFF_QK_DIM // 2
    lane = jnp.arange(LANES)
    inv_freq = ROPE_THETA ** (-(lane % half).astype(F32) / half)
    ang = jnp.arange(seq, dtype=F32)[:, None] * inv_freq[None, :]
    sign = jnp.where((lane % DIFF_QK_DIM) < half, -1.0, 1.0)
    return jnp.cos(ang), jnp.sin(ang) * sign


def kernel(x, sb_w_qkv, sb_w_o, shared_w_kv, diff_w_q, diff_w_o, diff_lambda, diff_subln_g, ln_mix_g, ln_mix_b,
           ln_ffn_g, ln_ffn_b, moe_w_group, moe_w_inner, moe_w_gate, moe_w_up, moe_w_down):
    batch, seq, d = x.shape
    n = batch * seq
    depth = ln_mix_g.shape[0]
    n_sb = sb_w_qkv.shape[0]
    alpha = (2.0 * depth) ** 0.25
    rope_tabs = _rope_tables(seq)

    h = x.reshape(n, d)
    hb = h.astype(BF16)
    kv = None
    for layer in range(depth):
        if layer < n_sb:
            qkv = _proj(hb, sb_w_qkv[layer].astype(BF16), scale=HEAD_DIM ** -0.5 * LOG2_E, scale_cols=d)
            o = _sb_attention(qkv, batch, seq)
            w_o = sb_w_o[layer]
        else:
            i = layer - n_sb
            if kv is None:
                kv = _proj(hb, shared_w_kv.astype(BF16), rope_cols=d, rope_tabs=rope_tabs, seq=seq)
            qd = _proj(hb, diff_w_q[i].astype(BF16), scale=DIFF_QK_DIM ** -0.5 * LOG2_E, scale_cols=d,
                       rope_cols=d, rope_tabs=rope_tabs, seq=seq)
            lam_init = 0.8 - 0.6 * math.exp(-0.3 * layer)
            o = _diff_attention(qd, kv, diff_lambda[i], diff_subln_g[i], lam_init, batch, seq)
            w_o = diff_w_o[i]
        h, hp = _proj_ln(o, w_o.astype(BF16), h, ln_mix_g[layer], ln_mix_b[layer], alpha)
        h, hb = _moe_ln(h, hp, layer, moe_w_group[layer], moe_w_inner[layer], moe_w_gate, moe_w_up,
                        moe_w_down, ln_ffn_g[layer], ln_ffn_b[layer], alpha)
    return h.reshape(batch, seq, d)
```

```python
import functools
import math

import jax
import jax.numpy as jnp
from jax import lax
from jax.experimental import pallas as pl
from jax.experimental.pallas import tpu as pltpu

HEAD_DIM = 128
DIFF_QK_DIM = HEAD_DIM // 2
N_GROUPS = 4
EXPERTS_PER_GROUP = 4
N_EXPERTS = N_GROUPS * EXPERTS_PER_GROUP
ROPE_THETA = 10000.0
LN_EPS = 1e-5
SUBLN_EPS = 1e-5
LANES = 128
ROUTE_LANE0 = N_GROUPS
NEG_BIG = -1e30
SB_SKIP_BELOW = -150.0
LOG2_E = 1.0 / math.log(2.0)
DMA_ISSUE_UNROLL = 8
VMEM_LIMIT = 56 * 1024 * 1024

F32 = jnp.float32
BF16 = jnp.bfloat16


def _tile(n, pref):
    t = min(n, pref)
    assert n % t == 0, (n, pref)
    return t


def _params(sem):
    return pltpu.CompilerParams(dimension_semantics=sem, vmem_limit_bytes=VMEM_LIMIT)


def _dot(a, b):
    return jnp.dot(a, b, preferred_element_type=F32)


def _dot_nt(a, b):
    return lax.dot_general(a, b, (((1,), (1,)), ((), ())), preferred_element_type=F32)


def _rope_tile(acc, cos, sin):
    lane = lax.broadcasted_iota(jnp.int32, cos.shape, 1)
    first_half = (lane % DIFF_QK_DIM) < (DIFF_QK_DIM // 2)
    outs = []
    for g in range(acc.shape[1] // LANES):
        xg = acc[:, g * LANES:(g + 1) * LANES]
        up = pltpu.roll(xg, LANES - DIFF_QK_DIM // 2, axis=1)
        dn = pltpu.roll(xg, DIFF_QK_DIM // 2, axis=1)
        partner = jnp.where(first_half, up, dn)
        outs.append(xg * cos + partner * sin)
    return jnp.concatenate(outs, axis=1)


def _proj_kernel(x_ref, w_ref, *rest, scale, scale_tiles, rope_tiles):
    if rope_tiles:
        cos_ref, sin_ref, o_ref = rest
    else:
        (o_ref,) = rest
    j = pl.program_id(0)
    acc = _dot(x_ref[...], w_ref[...])
    if scale_tiles:
        acc = acc * jnp.where(j < scale_tiles, jnp.float32(scale), jnp.float32(1.0))
    if rope_tiles:
        @pl.when(j < rope_tiles)
        def _():
            o_ref[...] = _rope_tile(acc, cos_ref[...], sin_ref[...]).astype(o_ref.dtype)

        @pl.when(j >= rope_tiles)
        def _():
            o_ref[...] = acc.astype(o_ref.dtype)
    else:
        o_ref[...] = acc.astype(o_ref.dtype)


def _proj(x, w, *, scale=1.0, scale_cols=0, rope_cols=0, rope_tabs=None, seq=None):
    n, k = x.shape
    m = w.shape[1]
    tm = _tile(n, 512)
    tn = math.gcd(1024, m, scale_cols, rope_cols)
    assert tn % LANES == 0
    in_specs = [pl.BlockSpec((tm, k), lambda j, i: (i, 0)),
                pl.BlockSpec((k, tn), lambda j, i: (0, j))]
    args = [x, w]
    if rope_cols:
        assert seq % tm == 0
        spb = seq // tm
        in_specs += [pl.BlockSpec((tm, LANES), lambda j, i: (i % spb, 0))] * 2
        args += list(rope_tabs)
    kern = functools.partial(_proj_kernel, scale=scale, scale_tiles=scale_cols // tn,
                             rope_tiles=rope_cols // tn)
    return pl.pallas_call(
        kern,
        out_shape=jax.ShapeDtypeStruct((n, m), BF16),
        grid=(m // tn, n // tm),
        in_specs=in_specs,
        out_specs=pl.BlockSpec((tm, tn), lambda j, i: (i, j)),
        compiler_params=_params(("arbitrary", "arbitrary")),
        name="proj",
    )(*args)


def _layer_norm_rows(v, g, b):
    mu = jnp.mean(v, axis=-1, keepdims=True)
    c = v - mu
    var = jnp.mean(c * c, axis=-1, keepdims=True)
    return c * lax.rsqrt(var + LN_EPS) * g + b


def _pack_bf16_pairs(y):
    c = y.shape[1] // 2
    bits = lax.bitcast_convert_type(y.astype(BF16).astype(F32), jnp.uint32)
    return bits[:, :c] | (bits[:, c:] >> 16)


def _unpack_bf16_pairs(w):
    return (lax.bitcast_convert_type(w & jnp.uint32(0xFFFF0000), F32),
            lax.bitcast_convert_type(w << 16, F32))


def _store_row_tiles(ref, packed):
    m, c = packed.shape
    r = c // LANES
    for s in range(r):
        ref[pl.ds(s, m, stride=r), :] = packed[:, s * LANES:(s + 1) * LANES]


def _load_row_tiles(ref, m, r, first=0):
    return jnp.concatenate([ref[pl.ds(first * r + s, m, stride=r), :] for s in range(r)], axis=1)


def _proj_ln_kernel(x_ref, w_ref, res_ref, g_ref, b_ref, o_ref, op_ref, *, alpha):
    acc = _dot(x_ref[...], w_ref[...])
    y = _layer_norm_rows(alpha * res_ref[...] + acc, g_ref[...], b_ref[...])
    o_ref[...] = y
    _store_row_tiles(op_ref, _pack_bf16_pairs(y))


def _proj_ln(x, w, res, g, b, alpha):
    n, k = x.shape
    d = w.shape[1]
    tm = _tile(n, 256)
    r = d // 2 // LANES
    return pl.pallas_call(
        functools.partial(_proj_ln_kernel, alpha=alpha),
        out_shape=(jax.ShapeDtypeStruct((n, d), F32), jax.ShapeDtypeStruct((n * r, LANES), jnp.uint32)),
        grid=(n // tm,),
        in_specs=[pl.BlockSpec((tm, k), lambda i: (i, 0)),
                  pl.BlockSpec((k, d), lambda i: (0, 0)),
                  pl.BlockSpec((tm, d), lambda i: (i, 0)),
                  pl.BlockSpec((1, d), lambda i: (0, 0)),
                  pl.BlockSpec((1, d), lambda i: (0, 0))],
        out_specs=(pl.BlockSpec((tm, d), lambda i: (i, 0)),
                   pl.BlockSpec((tm * r, LANES), lambda i: (i, 0))),
        compiler_params=_params(("arbitrary",)),
        name="proj_ln",
    )(x, w, res, g.reshape(1, d), b.reshape(1, d))


def _sb_attn_kernel(q_ref, k_ref, v_ref, o_ref, *, t, hb):
    qi = pl.program_id(2)
    row = lax.broadcasted_iota(jnp.int32, (2 * t, t), 0) % t
    col = lax.broadcasted_iota(jnp.int32, (2 * t, t), 1)
    later = (row > col).astype(BF16)
    past = (lax.broadcasted_iota(jnp.int32, (t, t), 1)
            < lax.broadcasted_iota(jnp.int32, (t, t), 0))

    def step(kc, accs, tails, masked):
        start = pl.multiple_of(kc * t, t)
        cols = [slice(h * HEAD_DIM, (h + 1) * HEAD_DIM) for h in range(hb)]
        zs = [_dot_nt(q_ref[:, c], k_ref[pl.ds(start, t), c]) for c in cols]
        log_betas, log_keeps, splits = [], [], []
        for z in zs:
            nl = -jnp.log2(1.0 + jnp.exp2(-jnp.abs(z)))
            log_betas.append(nl + jnp.minimum(z, 0.0))
            log_keep = nl - jnp.maximum(z, 0.0)
            if masked:
                log_keep = jnp.where(past, log_keep, 0.0)
            hi = log_keep.astype(BF16)
            lo = (log_keep - hi.astype(F32)).astype(BF16)
            log_keeps.append(log_keep)
            splits.append(jnp.concatenate([hi, lo], axis=1))
        tail_ins = [_dot(s, later) for s in splits]
        ws = []
        for h in range(hb):
            w = jnp.exp2(log_betas[h] + tail_ins[h] + tails[h])
            if masked:
                w = jnp.where(past, w, 0.0)
            ws.append(w.astype(BF16))
        outs = [_dot(ws[h], v_ref[pl.ds(start, t), cols[h]]) for h in range(hb)]
        accs = tuple(a + o for a, o in zip(accs, outs))
        tails = tuple(s + jnp.sum(lk, axis=1, keepdims=True) for s, lk in zip(tails, log_keeps))
        return accs, tails

    def any_weight_left(tails):
        worst = functools.reduce(jnp.maximum, tails)
        return (jnp.max(worst) >= SB_SKIP_BELOW).astype(jnp.int32)

    accs = tuple(jnp.zeros((t, HEAD_DIM), F32) for _ in range(hb))
    tails = tuple(jnp.zeros((t, 1), F32) for _ in range(hb))
    accs, tails = step(qi, accs, tails, True)

    def cond(c):
        return jnp.logical_and(c[0] >= 0, c[1] > 0)

    def body(c):
        kc, _, accs, tails = c
        accs, tails = step(kc, accs, tails, False)
        return kc - 1, any_weight_left(tails), accs, tails

    _, _, accs, _ = lax.while_loop(cond, body, (qi - 1, any_weight_left(tails), accs, tails))
    for h in range(hb):
        o_ref[:, h * HEAD_DIM:(h + 1) * HEAD_DIM] = accs[h].astype(o_ref.dtype)


def _sb_attention(qkv, batch, seq):
    n, three_d = qkv.shape
    d = three_d // 3
    heads = d // HEAD_DIM
    hb = math.gcd(heads, 8)
    t = _tile(seq, 128)
    nq = seq // t
    nh = heads // hb
    w = hb * HEAD_DIM
    return pl.pallas_call(
        functools.partial(_sb_attn_kernel, t=t, hb=hb),
        out_shape=jax.ShapeDtypeStruct((n, d), BF16),
        grid=(batch, nh, nq),
        in_specs=[pl.BlockSpec((t, w), lambda b, h, i: (b * nq + i, h)),
                  pl.BlockSpec((seq, w), lambda b, h, i: (b, nh + h)),
                  pl.BlockSpec((seq, w), lambda b, h, i: (b, 2 * nh + h))],
        out_specs=pl.BlockSpec((t, w), lambda b, h, i: (b * nq + i, h)),
        compiler_params=_params(("arbitrary", "arbitrary", "arbitrary")),
        name="sb_attn",
    )(qkv, qkv, qkv)


def _diff_attn_kernel(q_ref, k_ref, v_ref, lam_ref, g_ref, o_ref, *, t, lam_init):
    qi = pl.program_id(2)
    q = q_ref[...]
    lane = lax.broadcasted_iota(jnp.int32, (t, HEAD_DIM), 1)
    zero = jnp.zeros_like(q)
    q1 = jnp.where(lane < DIFF_QK_DIM, q, zero)
    q2 = jnp.where(lane >= DIFF_QK_DIM, q, zero)
    q_pos = lax.broadcasted_iota(jnp.int32, (t, t), 0)
    k_pos = lax.broadcasted_iota(jnp.int32, (t, t), 1)
    causal = k_pos <= q_pos

    def step(kc, carry, masked):
        start = pl.multiple_of(kc * t, t)
        k = k_ref[pl.ds(start, t), :]
        v = v_ref[pl.ds(start, t), :]
        scores = [_dot_nt(qx, k) for qx in (q1, q2)]
        out = []
        for s, (m, l, acc) in zip(scores, carry):
            if masked:
                s = jnp.where(causal, s, NEG_BIG)
            m_new = jnp.maximum(m, jnp.max(s, axis=1, keepdims=True))
            p = jnp.exp2(s - m_new)
            corr = jnp.exp2(m - m_new)
            l = corr * l + jnp.sum(p, axis=1, keepdims=True)
            acc = corr * acc + _dot(p.astype(BF16), v)
            out.append((m_new, l, acc))
        return tuple(out)

    init = (jnp.full((t, 1), NEG_BIG, F32), jnp.zeros((t, 1), F32), jnp.zeros((t, HEAD_DIM), F32))
    carry = step(qi, (init, init), True)
    carry = lax.fori_loop(0, qi, lambda c, s: step(c, s, False), carry)
    (_, l1, a1), (_, l2, a2) = carry

    lp = lam_ref[...]
    lam = (jnp.exp(jnp.sum(lp[0:1] * lp[1:2], axis=1, keepdims=True))
           - jnp.exp(jnp.sum(lp[2:3] * lp[3:4], axis=1, keepdims=True)) + lam_init)
    o = a1 / l1 - lam * (a2 / l2)
    o = o * lax.rsqrt(jnp.mean(o * o, axis=1, keepdims=True) + SUBLN_EPS)
    o_ref[...] = (o * g_ref[...] * (1.0 - lam_init)).astype(o_ref.dtype)


def _diff_attention(qd, kv, lam_params, subln_g, lam_init, batch, seq):
    n, d = qd.shape
    heads = d // HEAD_DIM
    t = _tile(seq, 512)
    nq = seq // t
    return pl.pallas_call(
        functools.partial(_diff_attn_kernel, t=t, lam_init=lam_init),
        out_shape=jax.ShapeDtypeStruct((n, d), BF16),
        grid=(batch, heads, nq),
        in_specs=[pl.BlockSpec((t, HEAD_DIM), lambda b, h, i: (b * nq + i, h)),
                  pl.BlockSpec((seq, HEAD_DIM), lambda b, h, i: (b, h)),
                  pl.BlockSpec((seq, HEAD_DIM), lambda b, h, i: (b, heads + h)),
                  pl.BlockSpec(lam_params.shape, lambda b, h, i: (0, 0)),
                  pl.BlockSpec((1, HEAD_DIM), lambda b, h, i: (0, 0))],
        out_specs=pl.BlockSpec((t, HEAD_DIM), lambda b, h, i: (b * nq + i, h)),
        compiler_params=_params(("arbitrary", "arbitrary", "arbitrary")),
        name="diff_attn",
    )(qd, kv, kv, lam_params, subln_g.reshape(1, HEAD_DIM))


def _split_bf16(x):
    hi = x.astype(BF16)
    return hi, (x - hi.astype(F32)).astype(BF16)


def _route_kernel(h_ref, w_ref, info_ref, cnt_ref, *, tm):
    i = pl.program_id(0)

    @pl.when(i == 0)
    def _():
        cnt_ref[...] = jnp.zeros_like(cnt_ref)

    h_hi, h_lo = _split_bf16(h_ref[...])
    w_hi, w_lo = _split_bf16(w_ref[...])
    logits = _dot(h_hi, w_hi) + (_dot(h_hi, w_lo) + _dot(h_lo, w_hi))

    lane = lax.broadcasted_iota(jnp.int32, (tm, LANES), 1).astype(F32)

    def first_max(mask, vals):
        top = jnp.max(jnp.where(mask, vals, NEG_BIG), axis=1, keepdims=True)
        idx = jnp.min(jnp.where(mask & (vals == top), lane, float(LANES)), axis=1, keepdims=True)
        return top, idx

    g_mask = lane < N_GROUPS
    g_top, g_sel = first_max(g_mask, logits)
    g_den = jnp.sum(jnp.where(g_mask, jnp.exp(logits - g_top), 0.0), axis=1, keepdims=True)
    g_w = 1.0 / g_den

    lo_lane = ROUTE_LANE0 + EXPERTS_PER_GROUP * g_sel
    in_group = (lane >= lo_lane) & (lane < lo_lane + EXPERTS_PER_GROUP)
    v1, i1 = first_max(in_group, logits)
    v2, i2 = first_max(in_group & (lane != i1), logits)
    ex = jnp.exp(v2 - v1)
    w1 = g_w / (1.0 + ex)
    w2 = w1 * ex

    m1 = (lane == i1).astype(F32)
    m2 = (lane == i2).astype(F32)
    both = m1 + m2
    r = lax.broadcasted_iota(jnp.int32, (tm, tm), 0)
    c = lax.broadcasted_iota(jnp.int32, (tm, tm), 1)
    earlier = (c < r).astype(BF16)
    before = _dot(earlier, both.astype(BF16)) + cnt_ref[0:1, :]
    rank1 = jnp.sum(m1 * before, axis=1, keepdims=True)
    rank2 = jnp.sum(m2 * before, axis=1, keepdims=True)
    cnt_ref[...] = cnt_ref[...] + jnp.sum(both, axis=0, keepdims=True)

    info = jnp.where(lane == 0, i1 - ROUTE_LANE0, 0.0)
    info = jnp.where(lane == 1, i2 - ROUTE_LANE0, info)
    info = jnp.where(lane == 2, rank1, info)
    info = jnp.where(lane == 3, rank2, info)
    info = jnp.where(lane == 4, w1, info)
    info = jnp.where(lane == 5, w2, info)
    info_ref[...] = info


def _route(h, w_route):
    n, d = h.shape
    tm = _tile(n, 512)
    return pl.pallas_call(
        functools.partial(_route_kernel, tm=tm),
        out_shape=(jax.ShapeDtypeStruct((n, LANES), F32), jax.ShapeDtypeStruct((8, LANES), F32)),
        grid=(n // tm,),
        in_specs=[pl.BlockSpec((tm, d), lambda i: (i, 0)),
                  pl.BlockSpec((d, LANES), lambda i: (0, 0))],
        out_specs=(pl.BlockSpec((tm, LANES), lambda i: (i, 0)),
                   pl.BlockSpec((8, LANES), lambda i: (0, 0))),
        compiler_params=_params(("arbitrary",)),
        name="route",
    )(h, w_route)


def _token_rows(ref, token, rt, count=1):
    return ref.at[pl.ds(pl.multiple_of(token * rt, rt), rt * count), :]


def _dispatch_kernel(slot_ref, pad_ref, used_ref, h_ref, xs_hbm, zrow, sem, pad_sem, *, tm, rt, n_pad, n_tiles):
    i = pl.program_id(0)
    base = i * tm

    def row_copy(r, s):
        return pltpu.make_async_copy(_token_rows(h_ref, r, rt), _token_rows(xs_hbm, s, rt), sem)

    def issue(r, _):
        t = base + r
        row_copy(r, slot_ref[2 * t]).start()
        row_copy(r, slot_ref[2 * t + 1]).start()
        return 0

    lax.fori_loop(0, tm, issue, 0, unroll=DMA_ISSUE_UNROLL)

    def pad_copy(s):
        return pltpu.make_async_copy(_token_rows(zrow, 0, rt), _token_rows(xs_hbm, s, rt), pad_sem)

    @pl.when(i == 0)
    def _():
        zrow[...] = jnp.zeros_like(zrow)

        def issue_pad(r, _):
            @pl.when(pad_ref[r] >= 0)
            def _():
                pad_copy(pad_ref[r]).start()
            return 0

        def drain_pad(r, _):
            @pl.when(pad_ref[r] >= 0)
            def _():
                pad_copy(0).wait()
            return 0

        def tile_copy(t):
            return pltpu.make_async_copy(zrow, _token_rows(xs_hbm, t * tm, rt, tm), pad_sem)

        def issue_tile(t, _):
            tile_copy(t).start()
            return 0

        def drain_tile(t, _):
            tile_copy(t).wait()
            return 0

        lax.fori_loop(0, n_pad, issue_pad, 0)
        lax.fori_loop(used_ref[0], n_tiles, issue_tile, 0)
        lax.fori_loop(0, n_pad, drain_pad, 0)
        lax.fori_loop(used_ref[0], n_tiles, drain_tile, 0)

    whole_tile = pltpu.make_async_copy(h_ref, _token_rows(xs_hbm, 0, rt, tm), sem)
    whole_tile.wait()
    whole_tile.wait()


def _dispatch(hp, slots, pad_slots, tiles_used, n, tm):
    rt = hp.shape[0] // n
    n_tiles = 2 * n // tm + N_EXPERTS
    assert n % tm == 0
    return pl.pallas_call(
        functools.partial(_dispatch_kernel, tm=tm, rt=rt, n_pad=pad_slots.shape[0], n_tiles=n_tiles),
        out_shape=jax.ShapeDtypeStruct((n_tiles * tm * rt, LANES), hp.dtype),
        grid_spec=pltpu.PrefetchScalarGridSpec(
            num_scalar_prefetch=3,
            grid=(n // tm,),
            in_specs=[pl.BlockSpec((tm * rt, LANES), lambda i, s, p, u: (i, 0))],
            out_specs=pl.BlockSpec(memory_space=pl.ANY),
            scratch_shapes=[pltpu.VMEM((tm * rt, LANES), hp.dtype), pltpu.SemaphoreType.DMA(()),
                            pltpu.SemaphoreType.DMA(())],
        ),
        compiler_params=_params(("arbitrary",)),
        name="moe_dispatch",
    )(slots, pad_slots, tiles_used, hp)


def _moe_up_kernel(te_ref, fresh_ref, valid_ref, x_ref, wg_ref, wu_ref, o_ref, wg_bf, wu_bf):
    i = pl.program_id(0)
    tm = o_ref.shape[0]
    rt = x_ref.shape[0] // tm
    c = rt * LANES

    @pl.when(fresh_ref[i] == 1)
    def _():
        wg_bf[...] = wg_ref[...].astype(BF16)
        wu_bf[...] = wu_ref[...].astype(BF16)

    @pl.when(valid_ref[i] == 1)
    def _():
        xa, xb = _unpack_bf16_pairs(_load_row_tiles(x_ref, tm, rt))
        xa, xb = xa.astype(BF16), xb.astype(BF16)
        g = _dot(xa, wg_bf[:c, :]) + _dot(xb, wg_bf[c:, :])
        u = _dot(xa, wu_bf[:c, :]) + _dot(xb, wu_bf[c:, :])
        o_ref[...] = (g * (1.0 / (1.0 + jnp.exp(-g))) * u).astype(o_ref.dtype)

    @pl.when(valid_ref[i] == 0)
    def _():
        o_ref[...] = jnp.zeros_like(o_ref)


def _tile_or_first(i, valid_ref):
    return jnp.where(valid_ref[i] == 1, i, 0)


def _moe_up(xs, w_gate, w_up, layer, tile_expert, fresh, valid, tm):
    _, _, d, f = w_gate.shape
    rt = d // 2 // LANES
    p = xs.shape[0] // rt
    w_spec = pl.BlockSpec((None, None, d, f), lambda i, te, fr, va: (layer, te[i], 0, 0))
    return pl.pallas_call(
        _moe_up_kernel,
        out_shape=jax.ShapeDtypeStruct((p, f), BF16),
        grid_spec=pltpu.PrefetchScalarGridSpec(
            num_scalar_prefetch=3,
            grid=(p // tm,),
            in_specs=[pl.BlockSpec((tm * rt, LANES), lambda i, te, fr, va: (_tile_or_first(i, va), 0)),
                      w_spec, w_spec],
            out_specs=pl.BlockSpec((tm, f), lambda i, te, fr, va: (i, 0)),
            scratch_shapes=[pltpu.VMEM((d, f), BF16), pltpu.VMEM((d, f), BF16)],
        ),
        compiler_params=_params(("arbitrary",)),
        name="moe_up",
    )(tile_expert, fresh, valid, xs, w_gate, w_up)


def _moe_down_kernel(te_ref, fresh_ref, valid_ref, a_ref, wd_ref, o_ref, wd_bf):
    i = pl.program_id(0)

    @pl.when(fresh_ref[i] == 1)
    def _():
        wd_bf[...] = wd_ref[...].astype(BF16)

    @pl.when(valid_ref[i] == 1)
    def _():
        _store_row_tiles(o_ref, _pack_bf16_pairs(_dot(a_ref[...], wd_bf[...])))

    @pl.when(valid_ref[i] == 0)
    def _():
        o_ref[...] = jnp.zeros_like(o_ref)


def _moe_down(act, w_down, layer, tile_expert, fresh, valid, tm):
    p, f = act.shape
    d = w_down.shape[3]
    rt = d // 2 // LANES
    return pl.pallas_call(
        _moe_down_kernel,
        out_shape=jax.ShapeDtypeStruct((p * rt, LANES), jnp.uint32),
        grid_spec=pltpu.PrefetchScalarGridSpec(
            num_scalar_prefetch=3,
            grid=(p // tm,),
            in_specs=[pl.BlockSpec((tm, f), lambda i, te, fr, va: (_tile_or_first(i, va), 0)),
                      pl.BlockSpec((None, None, f, d), lambda i, te, fr, va: (layer, te[i], 0, 0))],
            out_specs=pl.BlockSpec((tm * rt, LANES), lambda i, te, fr, va: (i, 0)),
            scratch_shapes=[pltpu.VMEM((f, d), BF16)],
        ),
        compiler_params=_params(("arbitrary",)),
        name="moe_down",
    )(tile_expert, fresh, valid, act, w_down)


def _combine_ln_kernel(slot_ref, h_ref, info_ref, g_ref, b_ref, ys_hbm, o_ref, ob_ref, buf, sems, *, tm, rt, alpha):
    i = pl.program_id(0)

    def row_copy(s, par, r):
        return pltpu.make_async_copy(_token_rows(ys_hbm, s, rt), _token_rows(buf.at[par], r, rt), sems.at[par])

    def gather_tile(tile, par):
        def issue(r, _):
            t = tile * tm + r
            row_copy(slot_ref[2 * t], par, r).start()
            row_copy(slot_ref[2 * t + 1], par, tm + r).start()
            return 0
        lax.fori_loop(0, tm, issue, 0, unroll=DMA_ISSUE_UNROLL)

    @pl.when(i == 0)
    def _():
        gather_tile(0, 0)

    @pl.when(i + 1 < pl.num_programs(0))
    def _():
        gather_tile(i + 1, (i + 1) % 2)

    par = i % 2
    pltpu.make_async_copy(_token_rows(ys_hbm, 0, rt, 2 * tm), buf.at[par], sems.at[par]).wait()

    info = info_ref[...]
    w1, w2 = info[:, 4:5], info[:, 5:6]
    a1, b1 = _unpack_bf16_pairs(_load_row_tiles(buf.at[par], tm, rt))
    a2, b2 = _unpack_bf16_pairs(_load_row_tiles(buf.at[par], tm, rt, first=tm))
    ffn = jnp.concatenate([w1 * a1 + w2 * a2, w1 * b1 + w2 * b2], axis=1)
    y = _layer_norm_rows(alpha * h_ref[...] + ffn, g_ref[...], b_ref[...])
    o_ref[...] = y
    ob_ref[...] = y.astype(BF16)


def _combine_ln(h, info, ys, slots, g, b, alpha):
    n, d = h.shape
    tm = _tile(n, 256)
    rt = d // 2 // LANES
    return pl.pallas_call(
        functools.partial(_combine_ln_kernel, tm=tm, rt=rt, alpha=alpha),
        out_shape=(jax.ShapeDtypeStruct((n, d), F32), jax.ShapeDtypeStruct((n, d), BF16)),
        grid_spec=pltpu.PrefetchScalarGridSpec(
            num_scalar_prefetch=1,
            grid=(n // tm,),
            in_specs=[pl.BlockSpec((tm, d), lambda i, s: (i, 0)),
                      pl.BlockSpec((tm, LANES), lambda i, s: (i, 0)),
                      pl.BlockSpec((1, d), lambda i, s: (0, 0)),
                      pl.BlockSpec((1, d), lambda i, s: (0, 0)),
                      pl.BlockSpec(memory_space=pl.ANY)],
            out_specs=(pl.BlockSpec((tm, d), lambda i, s: (i, 0)),
                       pl.BlockSpec((tm, d), lambda i, s: (i, 0))),
            scratch_shapes=[pltpu.VMEM((2, 2 * tm * rt, LANES), jnp.uint32), pltpu.SemaphoreType.DMA((2,))],
        ),
        compiler_params=_params(("arbitrary",)),
        name="moe_combine_ln",
    )(slots, h, info, g.reshape(1, d), b.reshape(1, d), ys)


def _moe_ln(h, hp, layer, w_group, w_inner, w_gate, w_up, w_down, ln_g, ln_b, alpha):
    n, d = h.shape
    tm = _tile(2 * n // N_EXPERTS, 256)
    n_rows = 2 * n + N_EXPERTS * tm
    n_tiles = n_rows // tm

    w_route = jnp.concatenate(
        [w_group, w_inner.transpose(1, 0, 2).reshape(d, N_EXPERTS),
         jnp.zeros((d, LANES - N_GROUPS - N_EXPERTS), F32)], axis=1)
    info, cnt = _route(h, w_route)

    expert = info[:, 0:2].astype(jnp.int32)
    rank = info[:, 2:4].astype(jnp.int32)
    counts = cnt[0, ROUTE_LANE0:ROUTE_LANE0 + N_EXPERTS].astype(jnp.int32)
    padded = (counts + tm - 1) // tm * tm
    ends = jnp.cumsum(padded)
    starts = ends - padded
    slots = (starts[expert] + rank).reshape(-1)
    pad_r = jnp.arange(tm, dtype=jnp.int32)[None, :]
    pad_slots = jnp.where(pad_r < (padded - counts)[:, None], (starts + counts)[:, None] + pad_r, -1).reshape(-1)
    tile_ids = jnp.arange(n_tiles, dtype=jnp.int32)
    tile_expert = jnp.minimum(jnp.sum((tile_ids[:, None] >= (ends // tm)[None, :]).astype(jnp.int32), axis=1),
                              N_EXPERTS - 1)
    valid = (tile_ids < ends[-1] // tm).astype(jnp.int32)
    fresh = jnp.concatenate([jnp.ones((1,), jnp.int32),
                             (tile_expert[1:] != tile_expert[:-1]).astype(jnp.int32)])

    xs = _dispatch(hp, slots, pad_slots, (ends[-1:] // tm).astype(jnp.int32), n, tm)
    act = _moe_up(xs, w_gate, w_up, layer, tile_expert, fresh, valid, tm)
    ys = _moe_down(act, w_down, layer, tile_expert, fresh, valid, tm)
    return _combine_ln(h, info, ys, slots, ln_g, ln_b, alpha)


def _rope_tables(seq):
    half = DIFF_QK_DIM // 2
    lane = jnp.arange(LANES)
    inv_freq = ROPE_THETA ** (-(lane % half).astype(F32) / half)
    ang = jnp.arange(seq, dtype=F32)[:, None] * inv_freq[None, :]
    sign = jnp.where((lane % DIFF_QK_DIM) < half, -1.0, 1.0)
    return jnp.cos(ang), jnp.sin(ang) * sign


def kernel(x, sb_w_qkv, sb_w_o, shared_w_kv, diff_w_q, diff_w_o, diff_lambda, diff_subln_g, ln_mix_g, ln_mix_b,
           ln_ffn_g, ln_ffn_b, moe_w_group, moe_w_inner, moe_w_gate, moe_w_up, moe_w_down):
    batch, seq, d = x.shape
    n = batch * seq
    depth = ln_mix_g.shape[0]
    n_sb = sb_w_qkv.shape[0]
    alpha = (2.0 * depth) ** 0.25
    rope_tabs = _rope_tables(seq)

    h = x.reshape(n, d)
    hb = h.astype(BF16)
    kv = None
    for layer in range(depth):
        if layer < n_sb:
            qkv = _proj(hb, sb_w_qkv[layer].astype(BF16), scale=HEAD_DIM ** -0.5 * LOG2_E, scale_cols=d)
            o = _sb_attention(qkv, batch, seq)
            w_o = sb_w_o[layer]
        else:
            i = layer - n_sb
            if kv is None:
                kv = _proj(hb, shared_w_kv.astype(BF16), rope_cols=d, rope_tabs=rope_tabs, seq=seq)
            qd = _proj(hb, diff_w_q[i].astype(BF16), scale=DIFF_QK_DIM ** -0.5 * LOG2_E, scale_cols=d,
                       rope_cols=d, rope_tabs=rope_tabs, seq=seq)
            lam_init = 0.8 - 0.6 * math.exp(-0.3 * layer)
            o = _diff_attention(qd, kv, diff_lambda[i], diff_subln_g[i], lam_init, batch, seq)
            w_o = diff_w_o[i]
        h, hp = _proj_ln(o, w_o.astype(BF16), h, ln_mix_g[layer], ln_mix_b[layer], alpha)
        h, hb = _moe_ln(h, hp, layer, moe_w_group[layer], moe_w_inner[layer], moe_w_gate, moe_w_up,
                        moe_w_down, ln_ffn_g[layer], ln_ffn_b[layer], alpha)
    return h.reshape(batch, seq, d)
```

```python
import functools
import math

import jax
import jax.numpy as jnp
from jax import lax
from jax.experimental import pallas as pl
from jax.experimental.pallas import tpu as pltpu

HEAD_DIM = 128
DIFF_QK_DIM = HEAD_DIM // 2
N_GROUPS = 4
EXPERTS_PER_GROUP = 4
N_EXPERTS = N_GROUPS * EXPERTS_PER_GROUP
ROPE_THETA = 10000.0
LN_EPS = 1e-5
SUBLN_EPS = 1e-5
LANES = 128
ROUTE_LANE0 = N_GROUPS
NEG_BIG = -1e30
SB_SKIP_BELOW = -150.0
LOG2_E = 1.0 / math.log(2.0)
DMA_ISSUE_UNROLL = 8
VMEM_LIMIT = 56 * 1024 * 1024

F32 = jnp.float32
BF16 = jnp.bfloat16


def _tile(n, pref):
    t = min(n, pref)
    assert n % t == 0, (n, pref)
    return t


def _params(sem):
    return pltpu.CompilerParams(dimension_semantics=sem, vmem_limit_bytes=VMEM_LIMIT)


def _dot(a, b):
    return jnp.dot(a, b, preferred_element_type=F32)


def _dot_nt(a, b):
    return lax.dot_general(a, b, (((1,), (1,)), ((), ())), preferred_element_type=F32)


def _rope_tile(acc, cos, sin):
    lane = lax.broadcasted_iota(jnp.int32, cos.shape, 1)
    first_half = (lane % DIFF_QK_DIM) < (DIFF_QK_DIM // 2)
    outs = []
    for g in range(acc.shape[1] // LANES):
        xg = acc[:, g * LANES:(g + 1) * LANES]
        up = pltpu.roll(xg, LANES - DIFF_QK_DIM // 2, axis=1)
        dn = pltpu.roll(xg, DIFF_QK_DIM // 2, axis=1)
        partner = jnp.where(first_half, up, dn)
        outs.append(xg * cos + partner * sin)
    return jnp.concatenate(outs, axis=1)


def _proj_kernel(x_ref, w_ref, *rest, scale, scale_tiles, rope_tiles):
    if rope_tiles:
        cos_ref, sin_ref, o_ref, w_bf = rest
    else:
        o_ref, w_bf = rest
    j = pl.program_id(0)

    @pl.when(pl.program_id(1) == 0)
    def _():
        w_bf[...] = w_ref[...].astype(BF16)

    acc = _dot(x_ref[...], w_bf[...])
    if scale_tiles:
        acc = acc * jnp.where(j < scale_tiles, jnp.float32(scale), jnp.float32(1.0))
    if rope_tiles:
        @pl.when(j < rope_tiles)
        def _():
            o_ref[...] = _rope_tile(acc, cos_ref[...], sin_ref[...]).astype(o_ref.dtype)

        @pl.when(j >= rope_tiles)
        def _():
            o_ref[...] = acc.astype(o_ref.dtype)
    else:
        o_ref[...] = acc.astype(o_ref.dtype)


def _proj(x, w, layer=None, *, scale=1.0, scale_cols=0, rope_cols=0, rope_tabs=None, seq=None):
    n, k = x.shape
    m = w.shape[-1]
    tm = _tile(n, 512)
    tn = math.gcd(1024, m, scale_cols, rope_cols)
    assert tn % LANES == 0
    if layer is None:
        w_spec = pl.BlockSpec((k, tn), lambda j, i: (0, j))
    else:
        w_spec = pl.BlockSpec((None, k, tn), lambda j, i: (layer, 0, j))
    in_specs = [pl.BlockSpec((tm, k), lambda j, i: (i, 0)), w_spec]
    args = [x, w]
    if rope_cols:
        assert seq % tm == 0
        spb = seq // tm
        in_specs += [pl.BlockSpec((tm, LANES), lambda j, i: (i % spb, 0))] * 2
        args += list(rope_tabs)
    kern = functools.partial(_proj_kernel, scale=scale, scale_tiles=scale_cols // tn,
                             rope_tiles=rope_cols // tn)
    return pl.pallas_call(
        kern,
        out_shape=jax.ShapeDtypeStruct((n, m), BF16),
        grid=(m // tn, n // tm),
        in_specs=in_specs,
        out_specs=pl.BlockSpec((tm, tn), lambda j, i: (i, j)),
        scratch_shapes=[pltpu.VMEM((k, tn), BF16)],
        compiler_params=_params(("arbitrary", "arbitrary")),
        name="proj",
    )(*args)


def _layer_norm_rows(v, g, b):
    mu = jnp.mean(v, axis=-1, keepdims=True)
    c = v - mu
    var = jnp.mean(c * c, axis=-1, keepdims=True)
    return c * lax.rsqrt(var + LN_EPS) * g + b


def _pack_bf16_pairs(y):
    c = y.shape[1] // 2
    bits = lax.bitcast_convert_type(y.astype(BF16).astype(F32), jnp.uint32)
    return bits[:, :c] | (bits[:, c:] >> 16)


def _unpack_bf16_pairs(w):
    return (lax.bitcast_convert_type(w & jnp.uint32(0xFFFF0000), F32),
            lax.bitcast_convert_type(w << 16, F32))


def _store_row_tiles(ref, packed):
    m, c = packed.shape
    r = c // LANES
    for s in range(r):
        ref[pl.ds(s, m, stride=r), :] = packed[:, s * LANES:(s + 1) * LANES]


def _load_row_tiles(ref, m, r, first=0):
    return jnp.concatenate([ref[pl.ds(first * r + s, m, stride=r), :] for s in range(r)], axis=1)


def _proj_ln_kernel(x_ref, w_ref, res_ref, g_ref, b_ref, o_ref, op_ref, *, alpha):
    acc = _dot(x_ref[...], w_ref[...])
    y = _layer_norm_rows(alpha * res_ref[...] + acc, g_ref[...], b_ref[...])
    o_ref[...] = y
    _store_row_tiles(op_ref, _pack_bf16_pairs(y))


def _proj_ln(x, w, res, g, b, alpha):
    n, k = x.shape
    d = w.shape[1]
    tm = _tile(n, 256)
    r = d // 2 // LANES
    return pl.pallas_call(
        functools.partial(_proj_ln_kernel, alpha=alpha),
        out_shape=(jax.ShapeDtypeStruct((n, d), F32), jax.ShapeDtypeStruct((n * r, LANES), jnp.uint32)),
        grid=(n // tm,),
        in_specs=[pl.BlockSpec((tm, k), lambda i: (i, 0)),
                  pl.BlockSpec((k, d), lambda i: (0, 0)),
                  pl.BlockSpec((tm, d), lambda i: (i, 0)),
                  pl.BlockSpec((1, d), lambda i: (0, 0)),
                  pl.BlockSpec((1, d), lambda i: (0, 0))],
        out_specs=(pl.BlockSpec((tm, d), lambda i: (i, 0)),
                   pl.BlockSpec((tm * r, LANES), lambda i: (i, 0))),
        compiler_params=_params(("arbitrary",)),
        name="proj_ln",
    )(x, w, res, g.reshape(1, d), b.reshape(1, d))


def _sb_attn_kernel(q_ref, k_ref, v_ref, o_ref, *, t, hb):
    qi = pl.program_id(2)
    row = lax.broadcasted_iota(jnp.int32, (2 * t, t), 0) % t
    col = lax.broadcasted_iota(jnp.int32, (2 * t, t), 1)
    later = (row > col).astype(BF16)
    past = (lax.broadcasted_iota(jnp.int32, (t, t), 1)
            < lax.broadcasted_iota(jnp.int32, (t, t), 0))

    def step(kc, accs, tails, masked):
        start = pl.multiple_of(kc * t, t)
        cols = [slice(h * HEAD_DIM, (h + 1) * HEAD_DIM) for h in range(hb)]
        zs = [_dot_nt(q_ref[:, c], k_ref[pl.ds(start, t), c]) for c in cols]
        log_betas, log_keeps, splits = [], [], []
        for z in zs:
            nl = -jnp.log2(1.0 + jnp.exp2(-jnp.abs(z)))
            log_betas.append(nl + jnp.minimum(z, 0.0))
            log_keep = nl - jnp.maximum(z, 0.0)
            if masked:
                log_keep = jnp.where(past, log_keep, 0.0)
            hi = log_keep.astype(BF16)
            lo = (log_keep - hi.astype(F32)).astype(BF16)
            log_keeps.append(log_keep)
            splits.append(jnp.concatenate([hi, lo], axis=1))
        tail_ins = [_dot(s, later) for s in splits]
        ws = []
        for h in range(hb):
            w = jnp.exp2(log_betas[h] + tail_ins[h] + tails[h])
            if masked:
                w = jnp.where(past, w, 0.0)
            ws.append(w.astype(BF16))
        outs = [_dot(ws[h], v_ref[pl.ds(start, t), cols[h]]) for h in range(hb)]
        accs = tuple(a + o for a, o in zip(accs, outs))
        tails = tuple(s + jnp.sum(lk, axis=1, keepdims=True) for s, lk in zip(tails, log_keeps))
        return accs, tails

    def any_weight_left(tails):
        worst = functools.reduce(jnp.maximum, tails)
        return (jnp.max(worst) >= SB_SKIP_BELOW).astype(jnp.int32)

    accs = tuple(jnp.zeros((t, HEAD_DIM), F32) for _ in range(hb))
    tails = tuple(jnp.zeros((t, 1), F32) for _ in range(hb))
    accs, tails = step(qi, accs, tails, True)

    def cond(c):
        return jnp.logical_and(c[0] >= 0, c[1] > 0)

    def body(c):
        kc, _, accs, tails = c
        accs, tails = step(kc, accs, tails, False)
        return kc - 1, any_weight_left(tails), accs, tails

    _, _, accs, _ = lax.while_loop(cond, body, (qi - 1, any_weight_left(tails), accs, tails))
    for h in range(hb):
        o_ref[:, h * HEAD_DIM:(h + 1) * HEAD_DIM] = accs[h].astype(o_ref.dtype)


def _sb_attention(qkv, batch, seq):
    n, three_d = qkv.shape
    d = three_d // 3
    heads = d // HEAD_DIM
    hb = math.gcd(heads, 16)
    t = _tile(seq, 128)
    nq = seq // t
    nh = heads // hb
    w = hb * HEAD_DIM
    return pl.pallas_call(
        functools.partial(_sb_attn_kernel, t=t, hb=hb),
        out_shape=jax.ShapeDtypeStruct((n, d), BF16),
        grid=(batch, nh, nq),
        in_specs=[pl.BlockSpec((t, w), lambda b, h, i: (b * nq + i, h)),
                  pl.BlockSpec((seq, w), lambda b, h, i: (b, nh + h)),
                  pl.BlockSpec((seq, w), lambda b, h, i: (b, 2 * nh + h))],
        out_specs=pl.BlockSpec((t, w), lambda b, h, i: (b * nq + i, h)),
        compiler_params=_params(("arbitrary", "arbitrary", "arbitrary")),
        name="sb_attn",
    )(qkv, qkv, qkv)


def _diff_attn_kernel(q_ref, k_ref, v_ref, lam_ref, g_ref, o_ref, *, t, lam_init):
    qi = pl.program_id(2)
    q = q_ref[...]
    lane = lax.broadcasted_iota(jnp.int32, (t, HEAD_DIM), 1)
    zero = jnp.zeros_like(q)
    q1 = jnp.where(lane < DIFF_QK_DIM, q, zero)
    q2 = jnp.where(lane >= DIFF_QK_DIM, q, zero)
    q_pos = lax.broadcasted_iota(jnp.int32, (t, t), 0)
    k_pos = lax.broadcasted_iota(jnp.int32, (t, t), 1)
    causal = k_pos <= q_pos

    def step(kc, carry, masked):
        start = pl.multiple_of(kc * t, t)
        k = k_ref[pl.ds(start, t), :]
        v = v_ref[pl.ds(start, t), :]
        scores = [_dot_nt(qx, k) for qx in (q1, q2)]
        out = []
        for s, (m, l, acc) in zip(scores, carry):
            if masked:
                s = jnp.where(causal, s, NEG_BIG)
            m_new = jnp.maximum(m, jnp.max(s, axis=1, keepdims=True))
            p = jnp.exp2(s - m_new)
            corr = jnp.exp2(m - m_new)
            l = corr * l + jnp.sum(p, axis=1, keepdims=True)
            acc = corr * acc + _dot(p.astype(BF16), v)
            out.append((m_new, l, acc))
        return tuple(out)

    init = (jnp.full((t, 1), NEG_BIG, F32), jnp.zeros((t, 1), F32), jnp.zeros((t, HEAD_DIM), F32))
    carry = step(qi, (init, init), True)
    carry = lax.fori_loop(0, qi, lambda c, s: step(c, s, False), carry)
    (_, l1, a1), (_, l2, a2) = carry

    lp = lam_ref[...]
    lam = (jnp.exp(jnp.sum(lp[0:1] * lp[1:2], axis=1, keepdims=True))
           - jnp.exp(jnp.sum(lp[2:3] * lp[3:4], axis=1, keepdims=True)) + lam_init)
    o = a1 / l1 - lam * (a2 / l2)
    o = o * lax.rsqrt(jnp.mean(o * o, axis=1, keepdims=True) + SUBLN_EPS)
    o_ref[...] = (o * g_ref[...] * (1.0 - lam_init)).astype(o_ref.dtype)


def _diff_attention(qd, kv, lam_params, subln_g, lam_init, batch, seq):
    n, d = qd.shape
    heads = d // HEAD_DIM
    t = _tile(seq, 512)
    nq = seq // t
    return pl.pallas_call(
        functools.partial(_diff_attn_kernel, t=t, lam_init=lam_init),
        out_shape=jax.ShapeDtypeStruct((n, d), BF16),
        grid=(batch, heads, nq),
        in_specs=[pl.BlockSpec((t, HEAD_DIM), lambda b, h, i: (b * nq + i, h)),
                  pl.BlockSpec((seq, HEAD_DIM), lambda b, h, i: (b, h)),
                  pl.BlockSpec((seq, HEAD_DIM), lambda b, h, i: (b, heads + h)),
                  pl.BlockSpec(lam_params.shape, lambda b, h, i: (0, 0)),
                  pl.BlockSpec((1, HEAD_DIM), lambda b, h, i: (0, 0))],
        out_specs=pl.BlockSpec((t, HEAD_DIM), lambda b, h, i: (b * nq + i, h)),
        compiler_params=_params(("arbitrary", "arbitrary", "arbitrary")),
        name="diff_attn",
    )(qd, kv, kv, lam_params, subln_g.reshape(1, HEAD_DIM))


def _split_bf16(x):
    hi = x.astype(BF16)
    return hi, (x - hi.astype(F32)).astype(BF16)


def _route_rows(h, w_ref, info_ref, cnt_ref):
    tm = h.shape[0]
    i = pl.program_id(0)

    @pl.when(i == 0)
    def _():
        cnt_ref[...] = jnp.zeros_like(cnt_ref)

    h_hi, h_lo = _split_bf16(h)
    w_hi, w_lo = _split_bf16(w_ref[...])
    both_w = _dot(h_hi, jnp.concatenate([w_hi, w_lo], axis=1))
    logits = both_w[:, :LANES] + (both_w[:, LANES:] + _dot(h_lo, w_hi))

    lane = lax.broadcasted_iota(jnp.int32, (tm, LANES), 1).astype(F32)

    def first_max(mask, vals):
        top = jnp.max(jnp.where(mask, vals, NEG_BIG), axis=1, keepdims=True)
        idx = jnp.min(jnp.where(mask & (vals == top), lane, float(LANES)), axis=1, keepdims=True)
        return top, idx

    g_mask = lane < N_GROUPS
    g_top, g_sel = first_max(g_mask, logits)
    g_den = jnp.sum(jnp.where(g_mask, jnp.exp(logits - g_top), 0.0), axis=1, keepdims=True)
    g_w = 1.0 / g_den

    lo_lane = ROUTE_LANE0 + EXPERTS_PER_GROUP * g_sel
    in_group = (lane >= lo_lane) & (lane < lo_lane + EXPERTS_PER_GROUP)
    v1, i1 = first_max(in_group, logits)
    v2, i2 = first_max(in_group & (lane != i1), logits)
    ex = jnp.exp(v2 - v1)
    w1 = g_w / (1.0 + ex)
    w2 = w1 * ex

    m1 = (lane == i1).astype(F32)
    m2 = (lane == i2).astype(F32)
    both = m1 + m2
    r = lax.broadcasted_iota(jnp.int32, (tm, tm), 0)
    c = lax.broadcasted_iota(jnp.int32, (tm, tm), 1)
    earlier = (c < r).astype(BF16)
    before = _dot(earlier, both.astype(BF16)) + cnt_ref[0:1, :]
    rank1 = jnp.sum(m1 * before, axis=1, keepdims=True)
    rank2 = jnp.sum(m2 * before, axis=1, keepdims=True)
    cnt_ref[...] = cnt_ref[...] + jnp.sum(both, axis=0, keepdims=True)

    info = jnp.where(lane == 0, i1 - ROUTE_LANE0, 0.0)
    info = jnp.where(lane == 1, i2 - ROUTE_LANE0, info)
    info = jnp.where(lane == 2, rank1, info)
    info = jnp.where(lane == 3, rank2, info)
    info = jnp.where(lane == 4, w1, info)
    info = jnp.where(lane == 5, w2, info)
    info_ref[...] = info


def _route_kernel(h_ref, w_ref, info_ref, cnt_ref):
    _route_rows(h_ref[...], w_ref, info_ref, cnt_ref)


def _route(h, w_route):
    n, d = h.shape
    tm = _tile(n, 512)
    return pl.pallas_call(
        _route_kernel,
        out_shape=(jax.ShapeDtypeStruct((n, LANES), F32), jax.ShapeDtypeStruct((8, LANES), F32)),
        grid=(n // tm,),
        in_specs=[pl.BlockSpec((tm, d), lambda i: (i, 0)),
                  pl.BlockSpec((d, LANES), lambda i: (0, 0))],
        out_specs=(pl.BlockSpec((tm, LANES), lambda i: (i, 0)),
                   pl.BlockSpec((8, LANES), lambda i: (0, 0))),
        compiler_params=_params(("arbitrary",)),
        name="route",
    )(h, w_route)


def _token_rows(ref, token, rt, count=1):
    return ref.at[pl.ds(pl.multiple_of(token * rt, rt), rt * count), :]


def _dispatch_kernel(slot_ref, pad_ref, used_ref, h_ref, xs_hbm, zrow, sem, pad_sem, *, tb, tm, rt, n_pad, n_tiles):
    i = pl.program_id(0)
    base = i * tb

    def row_copy(r, s):
        return pltpu.make_async_copy(_token_rows(h_ref, r, rt), _token_rows(xs_hbm, s, rt), sem)

    def issue(r, _):
        t = base + r
        row_copy(r, slot_ref[2 * t]).start()
        row_copy(r, slot_ref[2 * t + 1]).start()
        return 0

    lax.fori_loop(0, tb, issue, 0, unroll=DMA_ISSUE_UNROLL)

    def pad_copy(s):
        return pltpu.make_async_copy(_token_rows(zrow, 0, rt), _token_rows(xs_hbm, s, rt), pad_sem)

    @pl.when(i == 0)
    def _():
        zrow[...] = jnp.zeros_like(zrow)

        def issue_pad(r, _):
            @pl.when(pad_ref[r] >= 0)
            def _():
                pad_copy(pad_ref[r]).start()
            return 0

        def drain_pad(r, _):
            @pl.when(pad_ref[r] >= 0)
            def _():
                pad_copy(0).wait()
            return 0

        def tile_copy(t):
            return pltpu.make_async_copy(zrow, _token_rows(xs_hbm, t * tm, rt, tm), pad_sem)

        def issue_tile(t, _):
            tile_copy(t).start()
            return 0

        def drain_tile(t, _):
            tile_copy(t).wait()
            return 0

        lax.fori_loop(0, n_pad, issue_pad, 0)
        lax.fori_loop(used_ref[0], n_tiles, issue_tile, 0)
        lax.fori_loop(0, n_pad, drain_pad, 0)
        lax.fori_loop(used_ref[0], n_tiles, drain_tile, 0)

    whole_tile = pltpu.make_async_copy(h_ref, _token_rows(xs_hbm, 0, rt, tb), sem)
    whole_tile.wait()
    whole_tile.wait()


def _dispatch(hp, slots, pad_slots, tiles_used, n, tm):
    rt = hp.shape[0] // n
    n_tiles = 2 * n // tm + N_EXPERTS
    tb = _tile(n, 1024)
    return pl.pallas_call(
        functools.partial(_dispatch_kernel, tb=tb, tm=tm, rt=rt, n_pad=pad_slots.shape[0], n_tiles=n_tiles),
        out_shape=jax.ShapeDtypeStruct((n_tiles * tm * rt, LANES), hp.dtype),
        grid_spec=pltpu.PrefetchScalarGridSpec(
            num_scalar_prefetch=3,
            grid=(n // tb,),
            in_specs=[pl.BlockSpec((tb * rt, LANES), lambda i, s, p, u: (i, 0))],
            out_specs=pl.BlockSpec(memory_space=pl.ANY),
            scratch_shapes=[pltpu.VMEM((tm * rt, LANES), hp.dtype), pltpu.SemaphoreType.DMA(()),
                            pltpu.SemaphoreType.DMA(())],
        ),
        compiler_params=_params(("arbitrary",)),
        name="moe_dispatch",
    )(slots, pad_slots, tiles_used, hp)


def _moe_up_kernel(te_ref, fresh_ref, valid_ref, x_ref, wg_ref, wu_ref, o_ref, wg_bf, wu_bf):
    i = pl.program_id(0)
    tm = o_ref.shape[0]
    rt = x_ref.shape[0] // tm
    c = rt * LANES

    @pl.when(fresh_ref[i] == 1)
    def _():
        wg_bf[...] = wg_ref[...].astype(BF16)
        wu_bf[...] = wu_ref[...].astype(BF16)

    @pl.when(valid_ref[i] == 1)
    def _():
        xa, xb = _unpack_bf16_pairs(_load_row_tiles(x_ref, tm, rt))
        xa, xb = xa.astype(BF16), xb.astype(BF16)
        g = _dot(xa, wg_bf[:c, :]) + _dot(xb, wg_bf[c:, :])
        u = _dot(xa, wu_bf[:c, :]) + _dot(xb, wu_bf[c:, :])
        o_ref[...] = (g * (1.0 / (1.0 + jnp.exp(-g))) * u).astype(o_ref.dtype)

    @pl.when(valid_ref[i] == 0)
    def _():
        o_ref[...] = jnp.zeros_like(o_ref)


def _tile_or_first(i, valid_ref):
    return jnp.where(valid_ref[i] == 1, i, 0)


def _moe_up(xs, w_gate, w_up, layer, tile_expert, fresh, valid, tm):
    _, _, d, f = w_gate.shape
    rt = d // 2 // LANES
    p = xs.shape[0] // rt
    w_spec = pl.BlockSpec((None, None, d, f), lambda i, te, fr, va: (layer, te[i], 0, 0))
    return pl.pallas_call(
        _moe_up_kernel,
        out_shape=jax.ShapeDtypeStruct((p, f), BF16),
        grid_spec=pltpu.PrefetchScalarGridSpec(
            num_scalar_prefetch=3,
            grid=(p // tm,),
            in_specs=[pl.BlockSpec((tm * rt, LANES), lambda i, te, fr, va: (_tile_or_first(i, va), 0)),
                      w_spec, w_spec],
            out_specs=pl.BlockSpec((tm, f), lambda i, te, fr, va: (i, 0)),
            scratch_shapes=[pltpu.VMEM((d, f), BF16), pltpu.VMEM((d, f), BF16)],
        ),
        compiler_params=_params(("arbitrary",)),
        name="moe_up",
    )(tile_expert, fresh, valid, xs, w_gate, w_up)


def _moe_down_kernel(te_ref, fresh_ref, valid_ref, a_ref, wd_ref, o_ref, wd_bf):
    i = pl.program_id(0)

    @pl.when(fresh_ref[i] == 1)
    def _():
        wd_bf[...] = wd_ref[...].astype(BF16)

    @pl.when(valid_ref[i] == 1)
    def _():
        _store_row_tiles(o_ref, _pack_bf16_pairs(_dot(a_ref[...], wd_bf[...])))

    @pl.when(valid_ref[i] == 0)
    def _():
        o_ref[...] = jnp.zeros_like(o_ref)


def _moe_down(act, w_down, layer, tile_expert, fresh, valid, tm):
    p, f = act.shape
    d = w_down.shape[3]
    rt = d // 2 // LANES
    return pl.pallas_call(
        _moe_down_kernel,
        out_shape=jax.ShapeDtypeStruct((p * rt, LANES), jnp.uint32),
        grid_spec=pltpu.PrefetchScalarGridSpec(
            num_scalar_prefetch=3,
            grid=(p // tm,),
            in_specs=[pl.BlockSpec((tm, f), lambda i, te, fr, va: (_tile_or_first(i, va), 0)),
                      pl.BlockSpec((None, None, f, d), lambda i, te, fr, va: (layer, te[i], 0, 0))],
            out_specs=pl.BlockSpec((tm * rt, LANES), lambda i, te, fr, va: (i, 0)),
            scratch_shapes=[pltpu.VMEM((f, d), BF16)],
        ),
        compiler_params=_params(("arbitrary",)),
        name="moe_down",
    )(tile_expert, fresh, valid, act, w_down)


def _combine_ln_kernel(slot_ref, h_ref, info_ref, g_ref, b_ref, ys_hbm, o_ref, ob_ref, buf, sems, *, tm, rt, alpha):
    i = pl.program_id(0)

    def row_copy(s, par, r):
        return pltpu.make_async_copy(_token_rows(ys_hbm, s, rt), _token_rows(buf.at[par], r, rt), sems.at[par])

    def gather_tile(tile, par):
        def issue(r, _):
            t = tile * tm + r
            row_copy(slot_ref[2 * t], par, r).start()
            row_copy(slot_ref[2 * t + 1], par, tm + r).start()
            return 0
        lax.fori_loop(0, tm, issue, 0, unroll=DMA_ISSUE_UNROLL)

    @pl.when(i == 0)
    def _():
        gather_tile(0, 0)

    @pl.when(i + 1 < pl.num_programs(0))
    def _():
        gather_tile(i + 1, (i + 1) % 2)

    par = i % 2
    pltpu.make_async_copy(_token_rows(ys_hbm, 0, rt, 2 * tm), buf.at[par], sems.at[par]).wait()

    info = info_ref[...]
    w1, w2 = info[:, 4:5], info[:, 5:6]
    a1, b1 = _unpack_bf16_pairs(_load_row_tiles(buf.at[par], tm, rt))
    a2, b2 = _unpack_bf16_pairs(_load_row_tiles(buf.at[par], tm, rt, first=tm))
    ffn = jnp.concatenate([w1 * a1 + w2 * a2, w1 * b1 + w2 * b2], axis=1)
    y = _layer_norm_rows(alpha * h_ref[...] + ffn, g_ref[...], b_ref[...])
    o_ref[...] = y
    ob_ref[...] = y.astype(BF16)


def _combine_ln(h, info, ys, slots, g, b, alpha):
    n, d = h.shape
    tm = _tile(n, 256)
    rt = d // 2 // LANES
    return pl.pallas_call(
        functools.partial(_combine_ln_kernel, tm=tm, rt=rt, alpha=alpha),
        out_shape=(jax.ShapeDtypeStruct((n, d), F32), jax.ShapeDtypeStruct((n, d), BF16)),
        grid_spec=pltpu.PrefetchScalarGridSpec(
            num_scalar_prefetch=1,
            grid=(n // tm,),
            in_specs=[pl.BlockSpec((tm, d), lambda i, s: (i, 0)),
                      pl.BlockSpec((tm, LANES), lambda i, s: (i, 0)),
                      pl.BlockSpec((1, d), lambda i, s: (0, 0)),
                      pl.BlockSpec((1, d), lambda i, s: (0, 0)),
                      pl.BlockSpec(memory_space=pl.ANY)],
            out_specs=(pl.BlockSpec((tm, d), lambda i, s: (i, 0)),
                       pl.BlockSpec((tm, d), lambda i, s: (i, 0))),
            scratch_shapes=[pltpu.VMEM((2, 2 * tm * rt, LANES), jnp.uint32), pltpu.SemaphoreType.DMA((2,))],
        ),
        compiler_params=_params(("arbitrary",)),
        name="moe_combine_ln",
    )(slots, h, info, g.reshape(1, d), b.reshape(1, d), ys)


def _route_weights(w_group, w_inner):
    d = w_group.shape[0]
    return jnp.concatenate(
        [w_group, w_inner.transpose(1, 0, 2).reshape(d, N_EXPERTS),
         jnp.zeros((d, LANES - N_GROUPS - N_EXPERTS), F32)], axis=1)


def _moe_ln(h, hp, info, cnt, layer, w_gate, w_up, w_down, ln_g, ln_b, alpha):
    n, d = h.shape
    tm = _tile(2 * n // N_EXPERTS, 256)
    n_rows = 2 * n + N_EXPERTS * tm
    n_tiles = n_rows // tm

    expert = info[:, 0:2].astype(jnp.int32)
    rank = info[:, 2:4].astype(jnp.int32)
    counts = cnt[0, ROUTE_LANE0:ROUTE_LANE0 + N_EXPERTS].astype(jnp.int32)
    padded = (counts + tm - 1) // tm * tm
    ends = jnp.cumsum(padded)
    starts = ends - padded
    slots = (starts[expert] + rank).reshape(-1)
    pad_r = jnp.arange(tm, dtype=jnp.int32)[None, :]
    pad_slots = jnp.where(pad_r < (padded - counts)[:, None], (starts + counts)[:, None] + pad_r, -1).reshape(-1)
    tile_ids = jnp.arange(n_tiles, dtype=jnp.int32)
    tile_expert = jnp.minimum(jnp.sum((tile_ids[:, None] >= (ends // tm)[None, :]).astype(jnp.int32), axis=1),
                              N_EXPERTS - 1)
    valid = (tile_ids < ends[-1] // tm).astype(jnp.int32)
    fresh = jnp.concatenate([jnp.ones((1,), jnp.int32),
                             (tile_expert[1:] != tile_expert[:-1]).astype(jnp.int32)])

    xs = _dispatch(hp, slots, pad_slots, (ends[-1:] // tm).astype(jnp.int32), n, tm)
    act = _moe_up(xs, w_gate, w_up, layer, tile_expert, fresh, valid, tm)
    ys = _moe_down(act, w_down, layer, tile_expert, fresh, valid, tm)
    return _combine_ln(h, info, ys, slots, ln_g, ln_b, alpha)


def _rope_tables(seq):
    half = DIFF_QK_DIM // 2
    lane = jnp.arange(LANES)
    inv_freq = ROPE_THETA ** (-(lane % half).astype(F32) / half)
    ang = jnp.arange(seq, dtype=F32)[:, None] * inv_freq[None, :]
    sign = jnp.where((lane % DIFF_QK_DIM) < half, -1.0, 1.0)
    return jnp.cos(ang), jnp.sin(ang) * sign


def kernel(x, sb_w_qkv, sb_w_o, shared_w_kv, diff_w_q, diff_w_o, diff_lambda, diff_subln_g, ln_mix_g, ln_mix_b,
           ln_ffn_g, ln_ffn_b, moe_w_group, moe_w_inner, moe_w_gate, moe_w_up, moe_w_down):
    batch, seq, d = x.shape
    n = batch * seq
    depth = ln_mix_g.shape[0]
    n_sb = sb_w_qkv.shape[0]
    alpha = (2.0 * depth) ** 0.25
    rope_tabs = _rope_tables(seq)

    h = x.reshape(n, d)
    hb = h.astype(BF16)
    kv = None
    for layer in range(depth):
        if layer < n_sb:
            qkv = _proj(hb, sb_w_qkv, layer, scale=HEAD_DIM ** -0.5 * LOG2_E, scale_cols=d)
            o = _sb_attention(qkv, batch, seq)
            w_o = sb_w_o[layer]
        else:
            i = layer - n_sb
            if kv is None:
                kv = _proj(hb, shared_w_kv, rope_cols=d, rope_tabs=rope_tabs, seq=seq)
            qd = _proj(hb, diff_w_q, i, scale=DIFF_QK_DIM ** -0.5 * LOG2_E, scale_cols=d,
                       rope_cols=d, rope_tabs=rope_tabs, seq=seq)
            lam_init = 0.8 - 0.6 * math.exp(-0.3 * layer)
            o = _diff_attention(qd, kv, diff_lambda[i], diff_subln_g[i], lam_init, batch, seq)
            w_o = diff_w_o[i]
        h, hp = _proj_ln(o, w_o.astype(BF16), h, ln_mix_g[layer], ln_mix_b[layer], alpha)
        info, cnt = _route(h, _route_weights(moe_w_group[layer], moe_w_inner[layer]))
        h, hb = _moe_ln(h, hp, info, cnt, layer, moe_w_gate, moe_w_up, moe_w_down,
                        ln_ffn_g[layer], ln_ffn_b[layer], alpha)
    return h.reshape(batch, seq, d)
```

```python
import functools
import math

import jax
import jax.numpy as jnp
from jax import lax
from jax.experimental import pallas as pl
from jax.experimental.pallas import tpu as pltpu

HEAD_DIM = 128
DIFF_QK_DIM = HEAD_DIM // 2
N_GROUPS = 4
EXPERTS_PER_GROUP = 4
N_EXPERTS = N_GROUPS * EXPERTS_PER_GROUP
ROPE_THETA = 10000.0
LN_EPS = 1e-5
SUBLN_EPS = 1e-5
LANES = 128
ROUTE_LANE0 = N_GROUPS
NEG_BIG = -1e30
SB_SKIP_BELOW = -150.0
LOG2_E = 1.0 / math.log(2.0)
DMA_ISSUE_UNROLL = 8
VMEM_LIMIT = 56 * 1024 * 1024

F32 = jnp.float32
BF16 = jnp.bfloat16


def _tile(n, pref):
    t = min(n, pref)
    assert n % t == 0, (n, pref)
    return t


def _params(sem):
    return pltpu.CompilerParams(dimension_semantics=sem, vmem_limit_bytes=VMEM_LIMIT)


def _dot(a, b):
    return jnp.dot(a, b, preferred_element_type=F32)


def _dot_nt(a, b):
    return lax.dot_general(a, b, (((1,), (1,)), ((), ())), preferred_element_type=F32)


def _rope_tile(acc, cos, sin):
    lane = lax.broadcasted_iota(jnp.int32, cos.shape, 1)
    first_half = (lane % DIFF_QK_DIM) < (DIFF_QK_DIM // 2)
    outs = []
    for g in range(acc.shape[1] // LANES):
        xg = acc[:, g * LANES:(g + 1) * LANES]
        up = pltpu.roll(xg, LANES - DIFF_QK_DIM // 2, axis=1)
        dn = pltpu.roll(xg, DIFF_QK_DIM // 2, axis=1)
        partner = jnp.where(first_half, up, dn)
        outs.append(xg * cos + partner * sin)
    return jnp.concatenate(outs, axis=1)


def _proj_kernel(x_ref, w_ref, *rest, scale, scale_tiles, rope_tiles):
    if rope_tiles:
        cos_ref, sin_ref, o_ref, w_bf = rest
    else:
        o_ref, w_bf = rest
    j = pl.program_id(0)

    @pl.when(pl.program_id(1) == 0)
    def _():
        w_bf[...] = w_ref[...].astype(BF16)

    acc = _dot(x_ref[...], w_bf[...])
    if scale_tiles:
        acc = acc * jnp.where(j < scale_tiles, jnp.float32(scale), jnp.float32(1.0))
    if rope_tiles:
        @pl.when(j < rope_tiles)
        def _():
            o_ref[...] = _rope_tile(acc, cos_ref[...], sin_ref[...]).astype(o_ref.dtype)

        @pl.when(j >= rope_tiles)
        def _():
            o_ref[...] = acc.astype(o_ref.dtype)
    else:
        o_ref[...] = acc.astype(o_ref.dtype)


def _proj(x, w, layer=None, *, scale=1.0, scale_cols=0, rope_cols=0, rope_tabs=None, seq=None):
    n, k = x.shape
    m = w.shape[-1]
    tm = _tile(n, 512)
    tn = math.gcd(1024, m, scale_cols, rope_cols)
    assert tn % LANES == 0
    if layer is None:
        w_spec = pl.BlockSpec((k, tn), lambda j, i: (0, j))
    else:
        w_spec = pl.BlockSpec((None, k, tn), lambda j, i: (layer, 0, j))
    in_specs = [pl.BlockSpec((tm, k), lambda j, i: (i, 0)), w_spec]
    args = [x, w]
    if rope_cols:
        assert seq % tm == 0
        spb = seq // tm
        in_specs += [pl.BlockSpec((tm, LANES), lambda j, i: (i % spb, 0))] * 2
        args += list(rope_tabs)
    kern = functools.partial(_proj_kernel, scale=scale, scale_tiles=scale_cols // tn,
                             rope_tiles=rope_cols // tn)
    return pl.pallas_call(
        kern,
        out_shape=jax.ShapeDtypeStruct((n, m), BF16),
        grid=(m // tn, n // tm),
        in_specs=in_specs,
        out_specs=pl.BlockSpec((tm, tn), lambda j, i: (i, j)),
        scratch_shapes=[pltpu.VMEM((k, tn), BF16)],
        compiler_params=_params(("arbitrary", "arbitrary")),
        name="proj",
    )(*args)


def _layer_norm_rows(v, g, b):
    mu = jnp.mean(v, axis=-1, keepdims=True)
    c = v - mu
    var = jnp.mean(c * c, axis=-1, keepdims=True)
    return c * lax.rsqrt(var + LN_EPS) * g + b


def _pack_bf16_pairs(y):
    c = y.shape[1] // 2
    bits = lax.bitcast_convert_type(y.astype(BF16).astype(F32), jnp.uint32)
    return bits[:, :c] | (bits[:, c:] >> 16)


def _unpack_bf16_pairs(w):
    return (lax.bitcast_convert_type(w & jnp.uint32(0xFFFF0000), F32),
            lax.bitcast_convert_type(w << 16, F32))


def _store_row_tiles(ref, packed):
    m, c = packed.shape
    r = c // LANES
    for s in range(r):
        ref[pl.ds(s, m, stride=r), :] = packed[:, s * LANES:(s + 1) * LANES]


def _load_row_tiles(ref, m, r, first=0):
    return jnp.concatenate([ref[pl.ds(first * r + s, m, stride=r), :] for s in range(r)], axis=1)


def _proj_ln_kernel(x_ref, w_ref, res_ref, g_ref, b_ref, o_ref, op_ref, *, alpha, parts):
    tm = x_ref.shape[0]
    rt = op_ref.shape[0] // tm
    ts = tm // parts
    accs = [_dot(x_ref[p * ts:(p + 1) * ts, :], w_ref[...]) for p in range(parts)]
    for p, acc in enumerate(accs):
        rows = slice(p * ts, (p + 1) * ts)
        y = _layer_norm_rows(alpha * res_ref[rows, :] + acc, g_ref[...], b_ref[...])
        o_ref[rows, :] = y
        _store_row_tiles(op_ref.at[p * ts * rt:(p + 1) * ts * rt, :], _pack_bf16_pairs(y))


def _proj_ln(x, w, res, g, b, alpha):
    n, k = x.shape
    d = w.shape[1]
    tm = _tile(n, 512)
    parts = 2 if tm % 512 == 0 else 1
    r = d // 2 // LANES
    return pl.pallas_call(
        functools.partial(_proj_ln_kernel, alpha=alpha, parts=parts),
        out_shape=(jax.ShapeDtypeStruct((n, d), F32), jax.ShapeDtypeStruct((n * r, LANES), jnp.uint32)),
        grid=(n // tm,),
        in_specs=[pl.BlockSpec((tm, k), lambda i: (i, 0)),
                  pl.BlockSpec((k, d), lambda i: (0, 0)),
                  pl.BlockSpec((tm, d), lambda i: (i, 0)),
                  pl.BlockSpec((1, d), lambda i: (0, 0)),
                  pl.BlockSpec((1, d), lambda i: (0, 0))],
        out_specs=(pl.BlockSpec((tm, d), lambda i: (i, 0)),
                   pl.BlockSpec((tm * r, LANES), lambda i: (i, 0))),
        compiler_params=_params(("arbitrary",)),
        name="proj_ln",
    )(x, w, res, g.reshape(1, d), b.reshape(1, d))


def _sb_attn_kernel(q_ref, k_ref, v_ref, o_ref, *, t, hb):
    qi = pl.program_id(2)
    row = lax.broadcasted_iota(jnp.int32, (2 * t, t), 0) % t
    col = lax.broadcasted_iota(jnp.int32, (2 * t, t), 1)
    later = (row > col).astype(BF16)
    past = (lax.broadcasted_iota(jnp.int32, (t, t), 1)
            < lax.broadcasted_iota(jnp.int32, (t, t), 0))

    def step(kc, accs, tails, masked):
        start = pl.multiple_of(kc * t, t)
        cols = [slice(h * HEAD_DIM, (h + 1) * HEAD_DIM) for h in range(hb)]
        zs = [_dot_nt(q_ref[:, c], k_ref[pl.ds(start, t), c]) for c in cols]
        log_betas, log_keeps, splits = [], [], []
        for z in zs:
            nl = -jnp.log2(1.0 + jnp.exp2(-jnp.abs(z)))
            log_betas.append(nl + jnp.minimum(z, 0.0))
            log_keep = nl - jnp.maximum(z, 0.0)
            if masked:
                log_keep = jnp.where(past, log_keep, 0.0)
            hi = log_keep.astype(BF16)
            lo = (log_keep - hi.astype(F32)).astype(BF16)
            log_keeps.append(log_keep)
            splits.append(jnp.concatenate([hi, lo], axis=1))
        tail_ins = [_dot(s, later) for s in splits]
        ws = []
        for h in range(hb):
            w = jnp.exp2(log_betas[h] + tail_ins[h] + tails[h])
            if masked:
                w = jnp.where(past, w, 0.0)
            ws.append(w.astype(BF16))
        outs = [_dot(ws[h], v_ref[pl.ds(start, t), cols[h]]) for h in range(hb)]
        accs = tuple(a + o for a, o in zip(accs, outs))
        tails = tuple(s + jnp.sum(lk, axis=1, keepdims=True) for s, lk in zip(tails, log_keeps))
        return accs, tails

    def any_weight_left(tails):
        worst = functools.reduce(jnp.maximum, tails)
        return (jnp.max(worst) >= SB_SKIP_BELOW).astype(jnp.int32)

    accs = tuple(jnp.zeros((t, HEAD_DIM), F32) for _ in range(hb))
    tails = tuple(jnp.zeros((t, 1), F32) for _ in range(hb))
    accs, tails = step(qi, accs, tails, True)

    def cond(c):
        return jnp.logical_and(c[0] >= 0, c[1] > 0)

    def body(c):
        kc, _, accs, tails = c
        accs, tails = step(kc, accs, tails, False)
        return kc - 1, any_weight_left(tails), accs, tails

    _, _, accs, _ = lax.while_loop(cond, body, (qi - 1, any_weight_left(tails), accs, tails))
    for h in range(hb):
        o_ref[:, h * HEAD_DIM:(h + 1) * HEAD_DIM] = accs[h].astype(o_ref.dtype)


def _sb_attention(qkv, batch, seq):
    n, three_d = qkv.shape
    d = three_d // 3
    heads = d // HEAD_DIM
    hb = math.gcd(heads, 16)
    t = _tile(seq, 128)
    nq = seq // t
    nh = heads // hb
    w = hb * HEAD_DIM
    return pl.pallas_call(
        functools.partial(_sb_attn_kernel, t=t, hb=hb),
        out_shape=jax.ShapeDtypeStruct((n, d), BF16),
        grid=(batch, nh, nq),
        in_specs=[pl.BlockSpec((t, w), lambda b, h, i: (b * nq + i, h)),
                  pl.BlockSpec((seq, w), lambda b, h, i: (b, nh + h)),
                  pl.BlockSpec((seq, w), lambda b, h, i: (b, 2 * nh + h))],
        out_specs=pl.BlockSpec((t, w), lambda b, h, i: (b * nq + i, h)),
        compiler_params=_params(("arbitrary", "arbitrary", "arbitrary")),
        name="sb_attn",
    )(qkv, qkv, qkv)


def _diff_attn_kernel(q_ref, k_ref, v_ref, lam_ref, g_ref, o_ref, *, t, hb, lam_init):
    qi = pl.program_id(2)
    lane = lax.broadcasted_iota(jnp.int32, (t, HEAD_DIM), 1)
    q_pos = lax.broadcasted_iota(jnp.int32, (t, t), 0)
    k_pos = lax.broadcasted_iota(jnp.int32, (t, t), 1)
    causal = k_pos <= q_pos
    cols = [slice(h * HEAD_DIM, (h + 1) * HEAD_DIM) for h in range(hb)]
    qs = []
    for c in cols:
        q = q_ref[:, c]
        zero = jnp.zeros_like(q)
        qs.append((jnp.where(lane < DIFF_QK_DIM, q, zero),
                   jnp.where(lane >= DIFF_QK_DIM, q, zero)))

    def step(kc, carry, masked):
        start = pl.multiple_of(kc * t, t)
        scores = [[_dot_nt(qx, k_ref[pl.ds(start, t), c]) for qx in qh] for qh, c in zip(qs, cols)]
        out = []
        for h in range(hb):
            v = v_ref[pl.ds(start, t), cols[h]]
            streams = []
            for s, (m, l, acc) in zip(scores[h], carry[h]):
                if masked:
                    s = jnp.where(causal, s, NEG_BIG)
                m_new = jnp.maximum(m, jnp.max(s, axis=1, keepdims=True))
                p = jnp.exp2(s - m_new)
                corr = jnp.exp2(m - m_new)
                l = corr * l + jnp.sum(p, axis=1, keepdims=True)
                acc = corr * acc + _dot(p.astype(BF16), v)
                streams.append((m_new, l, acc))
            out.append(tuple(streams))
        return tuple(out)

    init = (jnp.full((t, 1), NEG_BIG, F32), jnp.zeros((t, 1), F32), jnp.zeros((t, HEAD_DIM), F32))
    carry = step(qi, ((init, init),) * hb, True)
    carry = lax.fori_loop(0, qi, lambda c, s: step(c, s, False), carry)

    lp = lam_ref[...]
    lam = (jnp.exp(jnp.sum(lp[0:1] * lp[1:2], axis=1, keepdims=True))
           - jnp.exp(jnp.sum(lp[2:3] * lp[3:4], axis=1, keepdims=True)) + lam_init)
    for h in range(hb):
        (_, l1, a1), (_, l2, a2) = carry[h]
        o = a1 / l1 - lam * (a2 / l2)
        o = o * lax.rsqrt(jnp.mean(o * o, axis=1, keepdims=True) + SUBLN_EPS)
        o_ref[:, cols[h]] = (o * g_ref[...] * (1.0 - lam_init)).astype(o_ref.dtype)


def _diff_attention(qd, kv, lam_params, subln_g, lam_init, batch, seq):
    n, d = qd.shape
    heads = d // HEAD_DIM
    t = _tile(seq, 512)
    nq = seq // t
    hb = math.gcd(heads, 2)
    nh = heads // hb
    w = hb * HEAD_DIM
    return pl.pallas_call(
        functools.partial(_diff_attn_kernel, t=t, hb=hb, lam_init=lam_init),
        out_shape=jax.ShapeDtypeStruct((n, d), BF16),
        grid=(batch, nh, nq),
        in_specs=[pl.BlockSpec((t, w), lambda b, h, i: (b * nq + i, h)),
                  pl.BlockSpec((seq, w), lambda b, h, i: (b, h)),
                  pl.BlockSpec((seq, w), lambda b, h, i: (b, nh + h)),
                  pl.BlockSpec(lam_params.shape, lambda b, h, i: (0, 0)),
                  pl.BlockSpec((1, HEAD_DIM), lambda b, h, i: (0, 0))],
        out_specs=pl.BlockSpec((t, w), lambda b, h, i: (b * nq + i, h)),
        compiler_params=_params(("arbitrary", "arbitrary", "arbitrary")),
        name="diff_attn",
    )(qd, kv, kv, lam_params, subln_g.reshape(1, HEAD_DIM))


def _split_bf16(x):
    hi = x.astype(BF16)
    return hi, (x - hi.astype(F32)).astype(BF16)


def _route_rows(h, w_ref, info_ref, cnt_ref):
    tm = h.shape[0]
    i = pl.program_id(0)

    @pl.when(i == 0)
    def _():
        cnt_ref[...] = jnp.zeros_like(cnt_ref)

    h_hi, h_lo = _split_bf16(h)
    w_hi, w_lo = _split_bf16(w_ref[...])
    both_w = _dot(h_hi, jnp.concatenate([w_hi, w_lo], axis=1))
    logits = both_w[:, :LANES] + (both_w[:, LANES:] + _dot(h_lo, w_hi))

    lane = lax.broadcasted_iota(jnp.int32, (tm, LANES), 1).astype(F32)

    def first_max(mask, vals):
        top = jnp.max(jnp.where(mask, vals, NEG_BIG), axis=1, keepdims=True)
        idx = jnp.min(jnp.where(mask & (vals == top), lane, float(LANES)), axis=1, keepdims=True)
        return top, idx

    g_mask = lane < N_GROUPS
    g_top, g_sel = first_max(g_mask, logits)
    g_den = jnp.sum(jnp.where(g_mask, jnp.exp(logits - g_top), 0.0), axis=1, keepdims=True)
    g_w = 1.0 / g_den

    lo_lane = ROUTE_LANE0 + EXPERTS_PER_GROUP * g_sel
    in_group = (lane >= lo_lane) & (lane < lo_lane + EXPERTS_PER_GROUP)
    v1, i1 = first_max(in_group, logits)
    v2, i2 = first_max(in_group & (lane != i1), logits)
    ex = jnp.exp(v2 - v1)
    w1 = g_w / (1.0 + ex)
    w2 = w1 * ex

    m1 = (lane == i1).astype(F32)
    m2 = (lane == i2).astype(F32)
    both = m1 + m2
    r = lax.broadcasted_iota(jnp.int32, (tm, tm), 0)
    c = lax.broadcasted_iota(jnp.int32, (tm, tm), 1)
    earlier = (c < r).astype(BF16)
    before = _dot(earlier, both.astype(BF16)) + cnt_ref[0:1, :]
    rank1 = jnp.sum(m1 * before, axis=1, keepdims=True)
    rank2 = jnp.sum(m2 * before, axis=1, keepdims=True)
    cnt_ref[...] = cnt_ref[...] + jnp.sum(both, axis=0, keepdims=True)

    info = jnp.where(lane == 0, i1 - ROUTE_LANE0, 0.0)
    info = jnp.where(lane == 1, i2 - ROUTE_LANE0, info)
    info = jnp.where(lane == 2, rank1, info)
    info = jnp.where(lane == 3, rank2, info)
    info = jnp.where(lane == 4, w1, info)
    info = jnp.where(lane == 5, w2, info)
    info_ref[...] = info


def _route_kernel(h_ref, w_ref, info_ref, cnt_ref):
    _route_rows(h_ref[...], w_ref, info_ref, cnt_ref)


def _route(h, w_route):
    n, d = h.shape
    tm = _tile(n, 512)
    return pl.pallas_call(
        _route_kernel,
        out_shape=(jax.ShapeDtypeStruct((n, LANES), F32), jax.ShapeDtypeStruct((8, LANES), F32)),
        grid=(n // tm,),
        in_specs=[pl.BlockSpec((tm, d), lambda i: (i, 0)),
                  pl.BlockSpec((d, LANES), lambda i: (0, 0))],
        out_specs=(pl.BlockSpec((tm, LANES), lambda i: (i, 0)),
                   pl.BlockSpec((8, LANES), lambda i: (0, 0))),
        compiler_params=_params(("arbitrary",)),
        name="route",
    )(h, w_route)


def _token_rows(ref, token, rt, count=1):
    return ref.at[pl.ds(pl.multiple_of(token * rt, rt), rt * count), :]


def _dispatch_kernel(slot_ref, pad_ref, used_ref, h_ref, xs_hbm, zrow, sem, pad_sem, *, tb, tm, rt, n_pad, n_tiles):
    i = pl.program_id(0)
    base = i * tb

    def row_copy(r, s):
        return pltpu.make_async_copy(_token_rows(h_ref, r, rt), _token_rows(xs_hbm, s, rt), sem)

    def issue(r, _):
        t = base + r
        row_copy(r, slot_ref[2 * t]).start()
        row_copy(r, slot_ref[2 * t + 1]).start()
        return 0

    lax.fori_loop(0, tb, issue, 0, unroll=DMA_ISSUE_UNROLL)

    def pad_copy(s):
        return pltpu.make_async_copy(_token_rows(zrow, 0, rt), _token_rows(xs_hbm, s, rt), pad_sem)

    @pl.when(i == 0)
    def _():
        zrow[...] = jnp.zeros_like(zrow)

        def issue_pad(r, _):
            @pl.when(pad_ref[r] >= 0)
            def _():
                pad_copy(pad_ref[r]).start()
            return 0

        def drain_pad(r, _):
            @pl.when(pad_ref[r] >= 0)
            def _():
                pad_copy(0).wait()
            return 0

        def tile_copy(t):
            return pltpu.make_async_copy(zrow, _token_rows(xs_hbm, t * tm, rt, tm), pad_sem)

        def issue_tile(t, _):
            tile_copy(t).start()
            return 0

        def drain_tile(t, _):
            tile_copy(t).wait()
            return 0

        lax.fori_loop(0, n_pad, issue_pad, 0)
        lax.fori_loop(used_ref[0], n_tiles, issue_tile, 0)
        lax.fori_loop(0, n_pad, drain_pad, 0)
        lax.fori_loop(used_ref[0], n_tiles, drain_tile, 0)

    whole_tile = pltpu.make_async_copy(h_ref, _token_rows(xs_hbm, 0, rt, tb), sem)
    whole_tile.wait()
    whole_tile.wait()


def _dispatch(hp, slots, pad_slots, tiles_used, n, tm):
    rt = hp.shape[0] // n
    n_tiles = 2 * n // tm + N_EXPERTS
    tb = _tile(n, 1024)
    return pl.pallas_call(
        functools.partial(_dispatch_kernel, tb=tb, tm=tm, rt=rt, n_pad=pad_slots.shape[0], n_tiles=n_tiles),
        out_shape=jax.ShapeDtypeStruct((n_tiles * tm * rt, LANES), hp.dtype),
        grid_spec=pltpu.PrefetchScalarGridSpec(
            num_scalar_prefetch=3,
            grid=(n // tb,),
            in_specs=[pl.BlockSpec((tb * rt, LANES), lambda i, s, p, u: (i, 0))],
            out_specs=pl.BlockSpec(memory_space=pl.ANY),
            scratch_shapes=[pltpu.VMEM((tm * rt, LANES), hp.dtype), pltpu.SemaphoreType.DMA(()),
                            pltpu.SemaphoreType.DMA(())],
        ),
        compiler_params=_params(("arbitrary",)),
        name="moe_dispatch",
    )(slots, pad_slots, tiles_used, hp)


def _moe_up_kernel(te_ref, fresh_ref, valid_ref, x_ref, wg_ref, wu_ref, o_ref, wg_bf, wu_bf):
    i = pl.program_id(0)
    tm = o_ref.shape[0]
    rt = x_ref.shape[0] // tm
    c = rt * LANES

    @pl.when(fresh_ref[i] == 1)
    def _():
        wg_bf[...] = wg_ref[...].astype(BF16)
        wu_bf[...] = wu_ref[...].astype(BF16)

    @pl.when(valid_ref[i] == 1)
    def _():
        xa, xb = _unpack_bf16_pairs(_load_row_tiles(x_ref, tm, rt))
        xa, xb = xa.astype(BF16), xb.astype(BF16)
        g = _dot(xa, wg_bf[:c, :]) + _dot(xb, wg_bf[c:, :])
        u = _dot(xa, wu_bf[:c, :]) + _dot(xb, wu_bf[c:, :])
        o_ref[...] = (g * (1.0 / (1.0 + jnp.exp(-g))) * u).astype(o_ref.dtype)

    @pl.when(valid_ref[i] == 0)
    def _():
        o_ref[...] = jnp.zeros_like(o_ref)


def _tile_or_first(i, valid_ref):
    return jnp.where(valid_ref[i] == 1, i, 0)


def _moe_up(xs, w_gate, w_up, layer, tile_expert, fresh, valid, tm):
    _, _, d, f = w_gate.shape
    rt = d // 2 // LANES
    p = xs.shape[0] // rt
    w_spec = pl.BlockSpec((None, None, d, f), lambda i, te, fr, va: (layer, te[i], 0, 0))
    return pl.pallas_call(
        _moe_up_kernel,
        out_shape=jax.ShapeDtypeStruct((p, f), BF16),
        grid_spec=pltpu.PrefetchScalarGridSpec(
            num_scalar_prefetch=3,
            grid=(p // tm,),
            in_specs=[pl.BlockSpec((tm * rt, LANES), lambda i, te, fr, va: (_tile_or_first(i, va), 0)),
                      w_spec, w_spec],
            out_specs=pl.BlockSpec((tm, f), lambda i, te, fr, va: (i, 0)),
            scratch_shapes=[pltpu.VMEM((d, f), BF16), pltpu.VMEM((d, f), BF16)],
        ),
        compiler_params=_params(("arbitrary",)),
        name="moe_up",
    )(tile_expert, fresh, valid, xs, w_gate, w_up)


def _moe_down_kernel(te_ref, fresh_ref, valid_ref, a_ref, wd_ref, o_ref, wd_bf):
    i = pl.program_id(0)

    @pl.when(fresh_ref[i] == 1)
    def _():
        wd_bf[...] = wd_ref[...].astype(BF16)

    @pl.when(valid_ref[i] == 1)
    def _():
        _store_row_tiles(o_ref, _pack_bf16_pairs(_dot(a_ref[...], wd_bf[...])))

    @pl.when(valid_ref[i] == 0)
    def _():
        o_ref[...] = jnp.zeros_like(o_ref)


def _moe_down(act, w_down, layer, tile_expert, fresh, valid, tm):
    p, f = act.shape
    d = w_down.shape[3]
    rt = d // 2 // LANES
    return pl.pallas_call(
        _moe_down_kernel,
        out_shape=jax.ShapeDtypeStruct((p * rt, LANES), jnp.uint32),
        grid_spec=pltpu.PrefetchScalarGridSpec(
            num_scalar_prefetch=3,
            grid=(p // tm,),
            in_specs=[pl.BlockSpec((tm, f), lambda i, te, fr, va: (_tile_or_first(i, va), 0)),
                      pl.BlockSpec((None, None, f, d), lambda i, te, fr, va: (layer, te[i], 0, 0))],
            out_specs=pl.BlockSpec((tm * rt, LANES), lambda i, te, fr, va: (i, 0)),
            scratch_shapes=[pltpu.VMEM((f, d), BF16)],
        ),
        compiler_params=_params(("arbitrary",)),
        name="moe_down",
    )(tile_expert, fresh, valid, act, w_down)


def _combine_ln_kernel(slot_ref, h_ref, info_ref, g_ref, b_ref, ys_hbm, o_ref, ob_ref, buf, sems, *, tm, rt, alpha):
    i = pl.program_id(0)

    def row_copy(s, par, r):
        return pltpu.make_async_copy(_token_rows(ys_hbm, s, rt), _token_rows(buf.at[par], r, rt), sems.at[par])

    def gather_tile(tile, par):
        def issue(r, _):
            t = tile * tm + r
            row_copy(slot_ref[2 * t], par, r).start()
            row_copy(slot_ref[2 * t + 1], par, tm + r).start()
            return 0
        lax.fori_loop(0, tm, issue, 0, unroll=DMA_ISSUE_UNROLL)

    @pl.when(i == 0)
    def _():
        gather_tile(0, 0)

    @pl.when(i + 1 < pl.num_programs(0))
    def _():
        gather_tile(i + 1, (i + 1) % 2)

    par = i % 2
    pltpu.make_async_copy(_token_rows(ys_hbm, 0, rt, 2 * tm), buf.at[par], sems.at[par]).wait()

    info = info_ref[...]
    w1, w2 = info[:, 4:5], info[:, 5:6]
    a1, b1 = _unpack_bf16_pairs(_load_row_tiles(buf.at[par], tm, rt))
    a2, b2 = _unpack_bf16_pairs(_load_row_tiles(buf.at[par], tm, rt, first=tm))
    ffn = jnp.concatenate([w1 * a1 + w2 * a2, w1 * b1 + w2 * b2], axis=1)
    y = _layer_norm_rows(alpha * h_ref[...] + ffn, g_ref[...], b_ref[...])
    o_ref[...] = y
    ob_ref[...] = y.astype(BF16)


def _combine_ln(h, info, ys, slots, g, b, alpha):
    n, d = h.shape
    tm = _tile(n, 256)
    rt = d // 2 // LANES
    return pl.pallas_call(
        functools.partial(_combine_ln_kernel, tm=tm, rt=rt, alpha=alpha),
        out_shape=(jax.ShapeDtypeStruct((n, d), F32), jax.ShapeDtypeStruct((n, d), BF16)),
        grid_spec=pltpu.PrefetchScalarGridSpec(
            num_scalar_prefetch=1,
            grid=(n // tm,),
            in_specs=[pl.BlockSpec((tm, d), lambda i, s: (i, 0)),
                      pl.BlockSpec((tm, LANES), lambda i, s: (i, 0)),
                      pl.BlockSpec((1, d), lambda i, s: (0, 0)),
                      pl.BlockSpec((1, d), lambda i, s: (0, 0)),
                      pl.BlockSpec(memory_space=pl.ANY)],
            out_specs=(pl.BlockSpec((tm, d), lambda i, s: (i, 0)),
                       pl.BlockSpec((tm, d), lambda i, s: (i, 0))),
            scratch_shapes=[pltpu.VMEM((2, 2 * tm * rt, LANES), jnp.uint32), pltpu.SemaphoreType.DMA((2,))],
        ),
        compiler_params=_params(("arbitrary",)),
        name="moe_combine_ln",
    )(slots, h, info, g.reshape(1, d), b.reshape(1, d), ys)


def _route_weights(w_group, w_inner):
    d = w_group.shape[0]
    return jnp.concatenate(
        [w_group, w_inner.transpose(1, 0, 2).reshape(d, N_EXPERTS),
         jnp.zeros((d, LANES - N_GROUPS - N_EXPERTS), F32)], axis=1)


def _moe_ln(h, hp, info, cnt, layer, w_gate, w_up, w_down, ln_g, ln_b, alpha):
    n, d = h.shape
    tm = _tile(2 * n // N_EXPERTS, 512)
    n_rows = 2 * n + N_EXPERTS * tm
    n_tiles = n_rows // tm

    expert = info[:, 0:2].astype(jnp.int32)
    rank = info[:, 2:4].astype(jnp.int32)
    counts = cnt[0, ROUTE_LANE0:ROUTE_LANE0 + N_EXPERTS].astype(jnp.int32)
    padded = (counts + tm - 1) // tm * tm
    ends = jnp.cumsum(padded)
    starts = ends - padded
    slots = (starts[expert] + rank).reshape(-1)
    pad_r = jnp.arange(tm, dtype=jnp.int32)[None, :]
    pad_slots = jnp.where(pad_r < (padded - counts)[:, None], (starts + counts)[:, None] + pad_r, -1).reshape(-1)
    tile_ids = jnp.arange(n_tiles, dtype=jnp.int32)
    tile_expert = jnp.minimum(jnp.sum((tile_ids[:, None] >= (ends // tm)[None, :]).astype(jnp.int32), axis=1),
                              N_EXPERTS - 1)
    valid = (tile_ids < ends[-1] // tm).astype(jnp.int32)
    fresh = jnp.concatenate([jnp.ones((1,), jnp.int32),
                             (tile_expert[1:] != tile_expert[:-1]).astype(jnp.int32)])

    xs = _dispatch(hp, slots, pad_slots, (ends[-1:] // tm).astype(jnp.int32), n, tm)
    act = _moe_up(xs, w_gate, w_up, layer, tile_expert, fresh, valid, tm)
    ys = _moe_down(act, w_down, layer, tile_expert, fresh, valid, tm)
    return _combine_ln(h, info, ys, slots, ln_g, ln_b, alpha)


def _rope_tables(seq):
    half = DIFF_QK_DIM // 2
    lane = jnp.arange(LANES)
    inv_freq = ROPE_THETA ** (-(lane % half).astype(F32) / half)
    ang = jnp.arange(seq, dtype=F32)[:, None] * inv_freq[None, :]
    sign = jnp.where((lane % DIFF_QK_DIM) < half, -1.0, 1.0)
    return jnp.cos(ang), jnp.sin(ang) * sign


def kernel(x, sb_w_qkv, sb_w_o, shared_w_kv, diff_w_q, diff_w_o, diff_lambda, diff_subln_g, ln_mix_g, ln_mix_b,
           ln_ffn_g, ln_ffn_b, moe_w_group, moe_w_inner, moe_w_gate, moe_w_up, moe_w_down):
    batch, seq, d = x.shape
    n = batch * seq
    depth = ln_mix_g.shape[0]
    n_sb = sb_w_qkv.shape[0]
    alpha = (2.0 * depth) ** 0.25
    rope_tabs = _rope_tables(seq)

    h = x.reshape(n, d)
    hb = h.astype(BF16)
    kv = None
    for layer in range(depth):
        if layer < n_sb:
            qkv = _proj(hb, sb_w_qkv, layer, scale=HEAD_DIM ** -0.5 * LOG2_E, scale_cols=d)
            o = _sb_attention(qkv, batch, seq)
            w_o = sb_w_o[layer]
        else:
            i = layer - n_sb
            if kv is None:
                kv = _proj(hb, shared_w_kv, rope_cols=d, rope_tabs=rope_tabs, seq=seq)
            qd = _proj(hb, diff_w_q, i, scale=DIFF_QK_DIM ** -0.5 * LOG2_E, scale_cols=d,
                       rope_cols=d, rope_tabs=rope_tabs, seq=seq)
            lam_init = 0.8 - 0.6 * math.exp(-0.3 * layer)
            o = _diff_attention(qd, kv, diff_lambda[i], diff_subln_g[i], lam_init, batch, seq)
            w_o = diff_w_o[i]
        h, hp = _proj_ln(o, w_o.astype(BF16), h, ln_mix_g[layer], ln_mix_b[layer], alpha)
        info, cnt = _route(h, _route_weights(moe_w_group[layer], moe_w_inner[layer]))
        h, hb = _moe_ln(h, hp, info, cnt, layer, moe_w_gate, moe_w_up, moe_w_down,
                        ln_ffn_g[layer], ln_ffn_b[layer], alpha)
    return h.reshape(batch, seq, d)
```

```python
import functools
import math

import jax
import jax.numpy as jnp
from jax import lax
from jax.experimental import pallas as pl
from jax.experimental.pallas import tpu as pltpu

HEAD_DIM = 128
DIFF_QK_DIM = HEAD_DIM // 2
N_GROUPS = 4
EXPERTS_PER_GROUP = 4
N_EXPERTS = N_GROUPS * EXPERTS_PER_GROUP
ROPE_THETA = 10000.0
LN_EPS = 1e-5
SUBLN_EPS = 1e-5
LANES = 128
ROUTE_LANE0 = N_GROUPS
NEG_BIG = -1e30
SB_SKIP_BELOW = -150.0
LOG2_E = 1.0 / math.log(2.0)
DMA_ISSUE_UNROLL = 8
VMEM_LIMIT = 56 * 1024 * 1024

F32 = jnp.float32
BF16 = jnp.bfloat16


def _tile(n, pref):
    t = min(n, pref)
    assert n % t == 0, (n, pref)
    return t


def _params(sem):
    return pltpu.CompilerParams(dimension_semantics=sem, vmem_limit_bytes=VMEM_LIMIT)


def _dot(a, b):
    return jnp.dot(a, b, preferred_element_type=F32)


def _dot_nt(a, b):
    return lax.dot_general(a, b, (((1,), (1,)), ((), ())), preferred_element_type=F32)


def _rope_tile(acc, cos, sin):
    lane = lax.broadcasted_iota(jnp.int32, cos.shape, 1)
    first_half = (lane % DIFF_QK_DIM) < (DIFF_QK_DIM // 2)
    outs = []
    for g in range(acc.shape[1] // LANES):
        xg = acc[:, g * LANES:(g + 1) * LANES]
        up = pltpu.roll(xg, LANES - DIFF_QK_DIM // 2, axis=1)
        dn = pltpu.roll(xg, DIFF_QK_DIM // 2, axis=1)
        partner = jnp.where(first_half, up, dn)
        outs.append(xg * cos + partner * sin)
    return jnp.concatenate(outs, axis=1)


def _proj_kernel(x_ref, w_ref, *rest, scale, scale_tiles, rope_tiles):
    if rope_tiles:
        cos_ref, sin_ref, o_ref, w_bf = rest
    else:
        o_ref, w_bf = rest
    j = pl.program_id(0)

    @pl.when(pl.program_id(1) == 0)
    def _():
        w_bf[...] = w_ref[...].astype(BF16)

    acc = _dot(x_ref[...], w_bf[...])
    if scale_tiles:
        acc = acc * jnp.where(j < scale_tiles, jnp.float32(scale), jnp.float32(1.0))
    if rope_tiles:
        @pl.when(j < rope_tiles)
        def _():
            o_ref[...] = _rope_tile(acc, cos_ref[...], sin_ref[...]).astype(o_ref.dtype)

        @pl.when(j >= rope_tiles)
        def _():
            o_ref[...] = acc.astype(o_ref.dtype)
    else:
        o_ref[...] = acc.astype(o_ref.dtype)


def _proj(x, w, layer=None, *, scale=1.0, scale_cols=0, rope_cols=0, rope_tabs=None, seq=None):
    n, k = x.shape
    m = w.shape[-1]
    tm = _tile(n, 512)
    tn = math.gcd(1024, m, scale_cols, rope_cols)
    assert tn % LANES == 0
    if layer is None:
        w_spec = pl.BlockSpec((k, tn), lambda j, i: (0, j))
    else:
        w_spec = pl.BlockSpec((None, k, tn), lambda j, i: (layer, 0, j))
    in_specs = [pl.BlockSpec((tm, k), lambda j, i: (i, 0)), w_spec]
    args = [x, w]
    if rope_cols:
        assert seq % tm == 0
        spb = seq // tm
        in_specs += [pl.BlockSpec((tm, LANES), lambda j, i: (i % spb, 0))] * 2
        args += list(rope_tabs)
    kern = functools.partial(_proj_kernel, scale=scale, scale_tiles=scale_cols // tn,
                             rope_tiles=rope_cols // tn)
    return pl.pallas_call(
        kern,
        out_shape=jax.ShapeDtypeStruct((n, m), BF16),
        grid=(m // tn, n // tm),
        in_specs=in_specs,
        out_specs=pl.BlockSpec((tm, tn), lambda j, i: (i, j)),
        scratch_shapes=[pltpu.VMEM((k, tn), BF16)],
        compiler_params=_params(("arbitrary", "arbitrary")),
        name="proj",
    )(*args)


def _layer_norm_rows(v, g, b):
    mu = jnp.mean(v, axis=-1, keepdims=True)
    c = v - mu
    var = jnp.mean(c * c, axis=-1, keepdims=True)
    return c * lax.rsqrt(var + LN_EPS) * g + b


def _pack_bf16_pairs(y):
    c = y.shape[1] // 2
    bits = lax.bitcast_convert_type(y.astype(BF16).astype(F32), jnp.uint32)
    return bits[:, :c] | (bits[:, c:] >> 16)


def _unpack_bf16_pairs(w):
    return (lax.bitcast_convert_type(w & jnp.uint32(0xFFFF0000), F32),
            lax.bitcast_convert_type(w << 16, F32))


def _store_row_tiles(ref, packed):
    m, c = packed.shape
    r = c // LANES
    for s in range(r):
        ref[pl.ds(s, m, stride=r), :] = packed[:, s * LANES:(s + 1) * LANES]


def _load_row_tiles(ref, m, r, first=0):
    return jnp.concatenate([ref[pl.ds(first * r + s, m, stride=r), :] for s in range(r)], axis=1)


def _proj_ln_kernel(x_ref, w_ref, res_ref, g_ref, b_ref, o_ref, op_ref, *, alpha, parts):
    tm = x_ref.shape[0]
    rt = op_ref.shape[0] // tm
    ts = tm // parts
    accs = [_dot(x_ref[p * ts:(p + 1) * ts, :], w_ref[...]) for p in range(parts)]
    for p, acc in enumerate(accs):
        rows = slice(p * ts, (p + 1) * ts)
        y = _layer_norm_rows(alpha * res_ref[rows, :] + acc, g_ref[...], b_ref[...])
        o_ref[rows, :] = y
        _store_row_tiles(op_ref.at[p * ts * rt:(p + 1) * ts * rt, :], _pack_bf16_pairs(y))


def _proj_ln(x, w, res, g, b, alpha):
    n, k = x.shape
    d = w.shape[1]
    tm = _tile(n, 512)
    parts = 2 if tm % 512 == 0 else 1
    r = d // 2 // LANES
    return pl.pallas_call(
        functools.partial(_proj_ln_kernel, alpha=alpha, parts=parts),
        out_shape=(jax.ShapeDtypeStruct((n, d), F32), jax.ShapeDtypeStruct((n * r, LANES), jnp.uint32)),
        grid=(n // tm,),
        in_specs=[pl.BlockSpec((tm, k), lambda i: (i, 0)),
                  pl.BlockSpec((k, d), lambda i: (0, 0)),
                  pl.BlockSpec((tm, d), lambda i: (i, 0)),
                  pl.BlockSpec((1, d), lambda i: (0, 0)),
                  pl.BlockSpec((1, d), lambda i: (0, 0))],
        out_specs=(pl.BlockSpec((tm, d), lambda i: (i, 0)),
                   pl.BlockSpec((tm * r, LANES), lambda i: (i, 0))),
        compiler_params=_params(("arbitrary",)),
        name="proj_ln",
    )(x, w, res, g.reshape(1, d), b.reshape(1, d))


def _sb_attn_kernel(q_ref, k_ref, v_ref, o_ref, *, t, hb):
    qi = pl.program_id(2)
    row = lax.broadcasted_iota(jnp.int32, (2 * t, t), 0) % t
    col = lax.broadcasted_iota(jnp.int32, (2 * t, t), 1)
    later = (row > col).astype(BF16)
    past = (lax.broadcasted_iota(jnp.int32, (t, t), 1)
            < lax.broadcasted_iota(jnp.int32, (t, t), 0))

    def step(kc, accs, tails, masked):
        start = pl.multiple_of(kc * t, t)
        cols = [slice(h * HEAD_DIM, (h + 1) * HEAD_DIM) for h in range(hb)]
        zs = [_dot_nt(q_ref[:, c], k_ref[pl.ds(start, t), c]) for c in cols]
        log_betas, log_keeps, splits = [], [], []
        for z in zs:
            nl = -jnp.log2(1.0 + jnp.exp2(-jnp.abs(z)))
            log_betas.append(nl + jnp.minimum(z, 0.0))
            log_keep = nl - jnp.maximum(z, 0.0)
            if masked:
                log_keep = jnp.where(past, log_keep, 0.0)
            hi = log_keep.astype(BF16)
            lo = (log_keep - hi.astype(F32)).astype(BF16)
            log_keeps.append(log_keep)
            splits.append(jnp.concatenate([hi, lo], axis=1))
        tail_ins = [_dot(s, later) for s in splits]
        ws = []
        for h in range(hb):
            w = jnp.exp2(log_betas[h] + tail_ins[h] + tails[h])
            if masked:
                w = jnp.where(past, w, 0.0)
            ws.append(w.astype(BF16))
        outs = [_dot(ws[h], v_ref[pl.ds(start, t), cols[h]]) for h in range(hb)]
        accs = tuple(a + o for a, o in zip(accs, outs))
        tails = tuple(s + jnp.sum(lk, axis=1, keepdims=True) for s, lk in zip(tails, log_keeps))
        return accs, tails

    def any_weight_left(tails):
        worst = functools.reduce(jnp.maximum, tails)
        return (jnp.max(worst) >= SB_SKIP_BELOW).astype(jnp.int32)

    accs = tuple(jnp.zeros((t, HEAD_DIM), F32) for _ in range(hb))
    tails = tuple(jnp.zeros((t, 1), F32) for _ in range(hb))
    accs, tails = step(qi, accs, tails, True)

    def cond(c):
        return jnp.logical_and(c[0] >= 0, c[1] > 0)

    def body(c):
        kc, _, accs, tails = c
        accs, tails = step(kc, accs, tails, False)
        return kc - 1, any_weight_left(tails), accs, tails

    _, _, accs, _ = lax.while_loop(cond, body, (qi - 1, any_weight_left(tails), accs, tails))
    for h in range(hb):
        o_ref[:, h * HEAD_DIM:(h + 1) * HEAD_DIM] = accs[h].astype(o_ref.dtype)


def _sb_attention(qkv, batch, seq):
    n, three_d = qkv.shape
    d = three_d // 3
    heads = d // HEAD_DIM
    hb = math.gcd(heads, 16)
    t = _tile(seq, 128)
    nq = seq // t
    nh = heads // hb
    w = hb * HEAD_DIM
    return pl.pallas_call(
        functools.partial(_sb_attn_kernel, t=t, hb=hb),
        out_shape=jax.ShapeDtypeStruct((n, d), BF16),
        grid=(batch, nh, nq),
        in_specs=[pl.BlockSpec((t, w), lambda b, h, i: (b * nq + i, h)),
                  pl.BlockSpec((seq, w), lambda b, h, i: (b, nh + h)),
                  pl.BlockSpec((seq, w), lambda b, h, i: (b, 2 * nh + h))],
        out_specs=pl.BlockSpec((t, w), lambda b, h, i: (b * nq + i, h)),
        compiler_params=_params(("arbitrary", "arbitrary", "arbitrary")),
        name="sb_attn",
    )(qkv, qkv, qkv)


def _diff_attn_kernel(q_ref, k_ref, v_ref, lam_ref, g_ref, o_ref, *, t, hb, lam_init):
    qi = pl.program_id(2)
    lane = lax.broadcasted_iota(jnp.int32, (t, HEAD_DIM), 1)
    q_pos = lax.broadcasted_iota(jnp.int32, (t, t), 0)
    k_pos = lax.broadcasted_iota(jnp.int32, (t, t), 1)
    causal = k_pos <= q_pos
    cols = [slice(h * HEAD_DIM, (h + 1) * HEAD_DIM) for h in range(hb)]
    qs = []
    for c in cols:
        q = q_ref[:, c]
        zero = jnp.zeros_like(q)
        qs.append((jnp.where(lane < DIFF_QK_DIM, q, zero),
                   jnp.where(lane >= DIFF_QK_DIM, q, zero)))

    def step(kc, carry, masked):
        start = pl.multiple_of(kc * t, t)
        scores = [[_dot_nt(qx, k_ref[pl.ds(start, t), c]) for qx in qh] for qh, c in zip(qs, cols)]
        out = []
        for h in range(hb):
            v = v_ref[pl.ds(start, t), cols[h]]
            streams = []
            for s, (m, l, acc) in zip(scores[h], carry[h]):
                if masked:
                    s = jnp.where(causal, s, NEG_BIG)
                m_new = jnp.maximum(m, jnp.max(s, axis=1, keepdims=True))
                p = jnp.exp2(s - m_new)
                corr = jnp.exp2(m - m_new)
                l = corr * l + jnp.sum(p, axis=1, keepdims=True)
                acc = corr * acc + _dot(p.astype(BF16), v)
                streams.append((m_new, l, acc))
            out.append(tuple(streams))
        return tuple(out)

    init = (jnp.full((t, 1), NEG_BIG, F32), jnp.zeros((t, 1), F32), jnp.zeros((t, HEAD_DIM), F32))
    carry = step(qi, ((init, init),) * hb, True)
    carry = lax.fori_loop(0, qi, lambda c, s: step(c, s, False), carry)

    lp = lam_ref[...]
    lam = (jnp.exp(jnp.sum(lp[0:1] * lp[1:2], axis=1, keepdims=True))
           - jnp.exp(jnp.sum(lp[2:3] * lp[3:4], axis=1, keepdims=True)) + lam_init)
    for h in range(hb):
        (_, l1, a1), (_, l2, a2) = carry[h]
        o = a1 / l1 - lam * (a2 / l2)
        o = o * lax.rsqrt(jnp.mean(o * o, axis=1, keepdims=True) + SUBLN_EPS)
        o_ref[:, cols[h]] = (o * g_ref[...] * (1.0 - lam_init)).astype(o_ref.dtype)


def _diff_attention(qd, kv, lam_params, subln_g, lam_init, batch, seq):
    n, d = qd.shape
    heads = d // HEAD_DIM
    t = _tile(seq, 512)
    nq = seq // t
    hb = math.gcd(heads, 2)
    nh = heads // hb
    w = hb * HEAD_DIM
    return pl.pallas_call(
        functools.partial(_diff_attn_kernel, t=t, hb=hb, lam_init=lam_init),
        out_shape=jax.ShapeDtypeStruct((n, d), BF16),
        grid=(batch, nh, nq),
        in_specs=[pl.BlockSpec((t, w), lambda b, h, i: (b * nq + i, h)),
                  pl.BlockSpec((seq, w), lambda b, h, i: (b, h)),
                  pl.BlockSpec((seq, w), lambda b, h, i: (b, nh + h)),
                  pl.BlockSpec(lam_params.shape, lambda b, h, i: (0, 0)),
                  pl.BlockSpec((1, HEAD_DIM), lambda b, h, i: (0, 0))],
        out_specs=pl.BlockSpec((t, w), lambda b, h, i: (b * nq + i, h)),
        compiler_params=_params(("arbitrary", "arbitrary", "arbitrary")),
        name="diff_attn",
    )(qd, kv, kv, lam_params, subln_g.reshape(1, HEAD_DIM))


def _split_bf16(x):
    hi = x.astype(BF16)
    return hi, (x - hi.astype(F32)).astype(BF16)


def _route_rows(h, w_ref, info_ref, cnt_ref):
    tm = h.shape[0]
    i = pl.program_id(0)

    @pl.when(i == 0)
    def _():
        cnt_ref[...] = jnp.zeros_like(cnt_ref)

    h_hi, h_lo = _split_bf16(h)
    w_hi, w_lo = _split_bf16(w_ref[...])
    both_w = _dot(h_hi, jnp.concatenate([w_hi, w_lo], axis=1))
    logits = both_w[:, :LANES] + (both_w[:, LANES:] + _dot(h_lo, w_hi))

    lane = lax.broadcasted_iota(jnp.int32, (tm, LANES), 1).astype(F32)

    def first_max(mask, vals):
        top = jnp.max(jnp.where(mask, vals, NEG_BIG), axis=1, keepdims=True)
        idx = jnp.min(jnp.where(mask & (vals == top), lane, float(LANES)), axis=1, keepdims=True)
        return top, idx

    g_mask = lane < N_GROUPS
    g_top, g_sel = first_max(g_mask, logits)
    g_den = jnp.sum(jnp.where(g_mask, jnp.exp(logits - g_top), 0.0), axis=1, keepdims=True)
    g_w = 1.0 / g_den

    lo_lane = ROUTE_LANE0 + EXPERTS_PER_GROUP * g_sel
    in_group = (lane >= lo_lane) & (lane < lo_lane + EXPERTS_PER_GROUP)
    v1, i1 = first_max(in_group, logits)
    v2, i2 = first_max(in_group & (lane != i1), logits)
    ex = jnp.exp(v2 - v1)
    w1 = g_w / (1.0 + ex)
    w2 = w1 * ex

    m1 = (lane == i1).astype(F32)
    m2 = (lane == i2).astype(F32)
    both = m1 + m2
    r = lax.broadcasted_iota(jnp.int32, (tm, tm), 0)
    c = lax.broadcasted_iota(jnp.int32, (tm, tm), 1)
    earlier = (c < r).astype(BF16)
    before = _dot(earlier, both.astype(BF16)) + cnt_ref[0:1, :]
    rank1 = jnp.sum(m1 * before, axis=1, keepdims=True)
    rank2 = jnp.sum(m2 * before, axis=1, keepdims=True)
    cnt_ref[...] = cnt_ref[...] + jnp.sum(both, axis=0, keepdims=True)

    info = jnp.where(lane == 0, i1 - ROUTE_LANE0, 0.0)
    info = jnp.where(lane == 1, i2 - ROUTE_LANE0, info)
    info = jnp.where(lane == 2, rank1, info)
    info = jnp.where(lane == 3, rank2, info)
    info = jnp.where(lane == 4, w1, info)
    info = jnp.where(lane == 5, w2, info)
    info_ref[...] = info


def _route_kernel(h_ref, w_ref, info_ref, cnt_ref):
    _route_rows(h_ref[...], w_ref, info_ref, cnt_ref)


def _route(h, w_route):
    n, d = h.shape
    tm = _tile(n, 512)
    return pl.pallas_call(
        _route_kernel,
        out_shape=(jax.ShapeDtypeStruct((n, LANES), F32), jax.ShapeDtypeStruct((8, LANES), F32)),
        grid=(n // tm,),
        in_specs=[pl.BlockSpec((tm, d), lambda i: (i, 0)),
                  pl.BlockSpec((d, LANES), lambda i: (0, 0))],
        out_specs=(pl.BlockSpec((tm, LANES), lambda i: (i, 0)),
                   pl.BlockSpec((8, LANES), lambda i: (0, 0))),
        compiler_params=_params(("arbitrary",)),
        name="route",
    )(h, w_route)


def _token_rows(ref, token, rt, count=1):
    return ref.at[pl.ds(pl.multiple_of(token * rt, rt), rt * count), :]


def _dispatch_kernel(slot_ref, pad_ref, used_ref, h_ref, xs_hbm, zrow, sem, pad_sem, *, tb, tm, rt, n_tiles):
    i = pl.program_id(0)
    base = i * tb

    def row_copy(r, s):
        return pltpu.make_async_copy(_token_rows(h_ref, r, rt), _token_rows(xs_hbm, s, rt), sem)

    def issue(r, _):
        t = base + r
        row_copy(r, slot_ref[2 * t]).start()
        row_copy(r, slot_ref[2 * t + 1]).start()
        return 0

    lax.fori_loop(0, tb, issue, 0, unroll=DMA_ISSUE_UNROLL)

    def for_each_pad_block(act):
        for e in range(N_EXPERTS):
            first, length = pad_ref[e], pad_ref[N_EXPERTS + e]
            for bit in reversed(range(tm.bit_length() - 1)):
                size = 1 << bit
                slot = first + ((length >> (bit + 1)) << (bit + 1))

                @pl.when(((length >> bit) & 1) == 1)
                def _():
                    act(pltpu.make_async_copy(_token_rows(zrow, 0, rt, size), _token_rows(xs_hbm, slot, rt, size),
                                              pad_sem))

    @pl.when(i == 0)
    def _():
        zrow[...] = jnp.zeros_like(zrow)

        def tile_copy(t):
            return pltpu.make_async_copy(zrow, _token_rows(xs_hbm, t * tm, rt, tm), pad_sem)

        def issue_tile(t, _):
            tile_copy(t).start()
            return 0

        def drain_tile(t, _):
            tile_copy(t).wait()
            return 0

        for_each_pad_block(lambda copy: copy.start())
        lax.fori_loop(used_ref[0], n_tiles, issue_tile, 0)
        for_each_pad_block(lambda copy: copy.wait())
        lax.fori_loop(used_ref[0], n_tiles, drain_tile, 0)

    whole_tile = pltpu.make_async_copy(h_ref, _token_rows(xs_hbm, 0, rt, tb), sem)
    whole_tile.wait()
    whole_tile.wait()


def _dispatch(hp, slots, pads, tiles_used, n, tm):
    assert tm & (tm - 1) == 0
    rt = hp.shape[0] // n
    n_tiles = 2 * n // tm + N_EXPERTS
    tb = _tile(n, 1024)
    return pl.pallas_call(
        functools.partial(_dispatch_kernel, tb=tb, tm=tm, rt=rt, n_tiles=n_tiles),
        out_shape=jax.ShapeDtypeStruct((n_tiles * tm * rt, LANES), hp.dtype),
        grid_spec=pltpu.PrefetchScalarGridSpec(
            num_scalar_prefetch=3,
            grid=(n // tb,),
            in_specs=[pl.BlockSpec((tb * rt, LANES), lambda i, s, p, u: (i, 0))],
            out_specs=pl.BlockSpec(memory_space=pl.ANY),
            scratch_shapes=[pltpu.VMEM((tm * rt, LANES), hp.dtype), pltpu.SemaphoreType.DMA(()),
                            pltpu.SemaphoreType.DMA(())],
        ),
        compiler_params=_params(("arbitrary",)),
        name="moe_dispatch",
    )(slots, pads, tiles_used, hp)


def _moe_up_kernel(te_ref, fresh_ref, valid_ref, x_ref, wg_ref, wu_ref, o_ref, wg_bf, wu_bf):
    i = pl.program_id(0)
    tm = o_ref.shape[0]
    rt = x_ref.shape[0] // tm
    c = rt * LANES

    @pl.when(fresh_ref[i] == 1)
    def _():
        wg_bf[...] = wg_ref[...].astype(BF16)
        wu_bf[...] = wu_ref[...].astype(BF16)

    @pl.when(valid_ref[i] == 1)
    def _():
        xa, xb = _unpack_bf16_pairs(_load_row_tiles(x_ref, tm, rt))
        xa, xb = xa.astype(BF16), xb.astype(BF16)
        g = _dot(xa, wg_bf[:c, :]) + _dot(xb, wg_bf[c:, :])
        u = _dot(xa, wu_bf[:c, :]) + _dot(xb, wu_bf[c:, :])
        o_ref[...] = (g * (1.0 / (1.0 + jnp.exp(-g))) * u).astype(o_ref.dtype)

    @pl.when(valid_ref[i] == 0)
    def _():
        o_ref[...] = jnp.zeros_like(o_ref)


def _tile_or_first(i, valid_ref):
    return jnp.where(valid_ref[i] == 1, i, 0)


def _moe_up(xs, w_gate, w_up, layer, tile_expert, fresh, valid, tm):
    _, _, d, f = w_gate.shape
    rt = d // 2 // LANES
    p = xs.shape[0] // rt
    w_spec = pl.BlockSpec((None, None, d, f), lambda i, te, fr, va: (layer, te[i], 0, 0))
    return pl.pallas_call(
        _moe_up_kernel,
        out_shape=jax.ShapeDtypeStruct((p, f), BF16),
        grid_spec=pltpu.PrefetchScalarGridSpec(
            num_scalar_prefetch=3,
            grid=(p // tm,),
            in_specs=[pl.BlockSpec((tm * rt, LANES), lambda i, te, fr, va: (_tile_or_first(i, va), 0)),
                      w_spec, w_spec],
            out_specs=pl.BlockSpec((tm, f), lambda i, te, fr, va: (i, 0)),
            scratch_shapes=[pltpu.VMEM((d, f), BF16), pltpu.VMEM((d, f), BF16)],
        ),
        compiler_params=_params(("arbitrary",)),
        name="moe_up",
    )(tile_expert, fresh, valid, xs, w_gate, w_up)


def _moe_down_kernel(te_ref, fresh_ref, valid_ref, a_ref, wd_ref, o_ref, wd_bf):
    i = pl.program_id(0)

    @pl.when(fresh_ref[i] == 1)
    def _():
        wd_bf[...] = wd_ref[...].astype(BF16)

    @pl.when(valid_ref[i] == 1)
    def _():
        _store_row_tiles(o_ref, _pack_bf16_pairs(_dot(a_ref[...], wd_bf[...])))

    @pl.when(valid_ref[i] == 0)
    def _():
        o_ref[...] = jnp.zeros_like(o_ref)


def _moe_down(act, w_down, layer, tile_expert, fresh, valid, tm):
    p, f = act.shape
    d = w_down.shape[3]
    rt = d // 2 // LANES
    return pl.pallas_call(
        _moe_down_kernel,
        out_shape=jax.ShapeDtypeStruct((p * rt, LANES), jnp.uint32),
        grid_spec=pltpu.PrefetchScalarGridSpec(
            num_scalar_prefetch=3,
            grid=(p // tm,),
            in_specs=[pl.BlockSpec((tm, f), lambda i, te, fr, va: (_tile_or_first(i, va), 0)),
                      pl.BlockSpec((None, None, f, d), lambda i, te, fr, va: (layer, te[i], 0, 0))],
            out_specs=pl.BlockSpec((tm * rt, LANES), lambda i, te, fr, va: (i, 0)),
            scratch_shapes=[pltpu.VMEM((f, d), BF16)],
        ),
        compiler_params=_params(("arbitrary",)),
        name="moe_down",
    )(tile_expert, fresh, valid, act, w_down)


def _combine_ln_kernel(slot_ref, h_ref, info_ref, g_ref, b_ref, ys_hbm, o_ref, ob_ref, buf, sems, *, tm, rt, alpha):
    i = pl.program_id(0)

    def row_copy(s, par, r):
        return pltpu.make_async_copy(_token_rows(ys_hbm, s, rt), _token_rows(buf.at[par], r, rt), sems.at[par])

    def gather_tile(tile, par):
        def issue(r, _):
            t = tile * tm + r
            row_copy(slot_ref[2 * t], par, r).start()
            row_copy(slot_ref[2 * t + 1], par, tm + r).start()
            return 0
        lax.fori_loop(0, tm, issue, 0, unroll=DMA_ISSUE_UNROLL)

    @pl.when(i == 0)
    def _():
        gather_tile(0, 0)

    @pl.when(i + 1 < pl.num_programs(0))
    def _():
        gather_tile(i + 1, (i + 1) % 2)

    par = i % 2
    pltpu.make_async_copy(_token_rows(ys_hbm, 0, rt, 2 * tm), buf.at[par], sems.at[par]).wait()

    info = info_ref[...]
    w1, w2 = info[:, 4:5], info[:, 5:6]
    a1, b1 = _unpack_bf16_pairs(_load_row_tiles(buf.at[par], tm, rt))
    a2, b2 = _unpack_bf16_pairs(_load_row_tiles(buf.at[par], tm, rt, first=tm))
    ffn = jnp.concatenate([w1 * a1 + w2 * a2, w1 * b1 + w2 * b2], axis=1)
    y = _layer_norm_rows(alpha * h_ref[...] + ffn, g_ref[...], b_ref[...])
    o_ref[...] = y
    ob_ref[...] = y.astype(BF16)


def _combine_ln(h, info, ys, slots, g, b, alpha):
    n, d = h.shape
    tm = _tile(n, 256)
    rt = d // 2 // LANES
    return pl.pallas_call(
        functools.partial(_combine_ln_kernel, tm=tm, rt=rt, alpha=alpha),
        out_shape=(jax.ShapeDtypeStruct((n, d), F32), jax.ShapeDtypeStruct((n, d), BF16)),
        grid_spec=pltpu.PrefetchScalarGridSpec(
            num_scalar_prefetch=1,
            grid=(n // tm,),
            in_specs=[pl.BlockSpec((tm, d), lambda i, s: (i, 0)),
                      pl.BlockSpec((tm, LANES), lambda i, s: (i, 0)),
                      pl.BlockSpec((1, d), lambda i, s: (0, 0)),
                      pl.BlockSpec((1, d), lambda i, s: (0, 0)),
                      pl.BlockSpec(memory_space=pl.ANY)],
            out_specs=(pl.BlockSpec((tm, d), lambda i, s: (i, 0)),
                       pl.BlockSpec((tm, d), lambda i, s: (i, 0))),
            scratch_shapes=[pltpu.VMEM((2, 2 * tm * rt, LANES), jnp.uint32), pltpu.SemaphoreType.DMA((2,))],
        ),
        compiler_params=_params(("arbitrary",)),
        name="moe_combine_ln",
    )(slots, h, info, g.reshape(1, d), b.reshape(1, d), ys)


def _route_weights(w_group, w_inner):
    d = w_group.shape[0]
    return jnp.concatenate(
        [w_group, w_inner.transpose(1, 0, 2).reshape(d, N_EXPERTS),
         jnp.zeros((d, LANES - N_GROUPS - N_EXPERTS), F32)], axis=1)


def _moe_ln(h, hp, info, cnt, layer, w_gate, w_up, w_down, ln_g, ln_b, alpha):
    n, d = h.shape
    tm = _tile(2 * n // N_EXPERTS, 512)
    n_rows = 2 * n + N_EXPERTS * tm
    n_tiles = n_rows // tm

    expert = info[:, 0:2].astype(jnp.int32)
    rank = info[:, 2:4].astype(jnp.int32)
    counts = cnt[0, ROUTE_LANE0:ROUTE_LANE0 + N_EXPERTS].astype(jnp.int32)
    padded = (counts + tm - 1) // tm * tm
    ends = jnp.cumsum(padded)
    starts = ends - padded
    slots = (starts[expert] + rank).reshape(-1)
    pads = jnp.concatenate([starts + counts, padded - counts])
    tile_ids = jnp.arange(n_tiles, dtype=jnp.int32)
    tile_expert = jnp.minimum(jnp.sum((tile_ids[:, None] >= (ends // tm)[None, :]).astype(jnp.int32), axis=1),
                              N_EXPERTS - 1)
    valid = (tile_ids < ends[-1] // tm).astype(jnp.int32)
    fresh = jnp.concatenate([jnp.ones((1,), jnp.int32),
                             (tile_expert[1:] != tile_expert[:-1]).astype(jnp.int32)])

    xs = _dispatch(hp, slots, pads, (ends[-1:] // tm).astype(jnp.int32), n, tm)
    act = _moe_up(xs, w_gate, w_up, layer, tile_expert, fresh, valid, tm)
    ys = _moe_down(act, w_down, layer, tile_expert, fresh, valid, tm)
    return _combine_ln(h, info, ys, slots, ln_g, ln_b, alpha)


def _rope_tables(seq):
    half = DIFF_QK_DIM // 2
    lane = jnp.arange(LANES)
    inv_freq = ROPE_THETA ** (-(lane % half).astype(F32) / half)
    ang = jnp.arange(seq, dtype=F32)[:, None] * inv_freq[None, :]
    sign = jnp.where((lane % DIFF_QK_DIM) < half, -1.0, 1.0)
    return jnp.cos(ang), jnp.sin(ang) * sign


def kernel(x, sb_w_qkv, sb_w_o, shared_w_kv, diff_w_q, diff_w_o, diff_lambda, diff_subln_g, ln_mix_g, ln_mix_b,
           ln_ffn_g, ln_ffn_b, moe_w_group, moe_w_inner, moe_w_gate, moe_w_up, moe_w_down):
    batch, seq, d = x.shape
    n = batch * seq
    depth = ln_mix_g.shape[0]
    n_sb = sb_w_qkv.shape[0]
    alpha = (2.0 * depth) ** 0.25
    rope_tabs = _rope_tables(seq)

    h = x.reshape(n, d)
    hb = h.astype(BF16)
    kv = None
    for layer in range(depth):
        if layer < n_sb:
            qkv = _proj(hb, sb_w_qkv, layer, scale=HEAD_DIM ** -0.5 * LOG2_E, scale_cols=d)
            o = _sb_attention(qkv, batch, seq)
            w_o = sb_w_o[layer]
        else:
            i = layer - n_sb
            if kv is None:
                kv = _proj(hb, shared_w_kv, rope_cols=d, rope_tabs=rope_tabs, seq=seq)
            qd = _proj(hb, diff_w_q, i, scale=DIFF_QK_DIM ** -0.5 * LOG2_E, scale_cols=d,
                       rope_cols=d, rope_tabs=rope_tabs, seq=seq)
            lam_init = 0.8 - 0.6 * math.exp(-0.3 * layer)
            o = _diff_attention(qd, kv, diff_lambda[i], diff_subln_g[i], lam_init, batch, seq)
            w_o = diff_w_o[i]
        h, hp = _proj_ln(o, w_o.astype(BF16), h, ln_mix_g[layer], ln_mix_b[layer], alpha)
        info, cnt = _route(h, _route_weights(moe_w_group[layer], moe_w_inner[layer]))
        h, hb = _moe_ln(h, hp, info, cnt, layer, moe_w_gate, moe_w_up, moe_w_down,
                        ln_ffn_g[layer], ln_ffn_b[layer], alpha)
    return h.reshape(batch, seq, d)
```

```python
import functools
import math

import jax
import jax.numpy as jnp
from jax import lax
from jax.experimental import pallas as pl
from jax.experimental.pallas import tpu as pltpu

HEAD_DIM = 128
DIFF_QK_DIM = HEAD_DIM // 2
N_GROUPS = 4
EXPERTS_PER_GROUP = 4
N_EXPERTS = N_GROUPS * EXPERTS_PER_GROUP
ROPE_THETA = 10000.0
LN_EPS = 1e-5
SUBLN_EPS = 1e-5
LANES = 128
ROUTE_LANE0 = N_GROUPS
NEG_BIG = -1e30
SB_SKIP_BELOW = -150.0
LOG2_E = 1.0 / math.log(2.0)
DMA_ISSUE_UNROLL = 8
VMEM_LIMIT = 56 * 1024 * 1024

F32 = jnp.float32
BF16 = jnp.bfloat16


def _tile(n, pref):
    t = min(n, pref)
    assert n % t == 0, (n, pref)
    return t


def _params(sem):
    return pltpu.CompilerParams(dimension_semantics=sem, vmem_limit_bytes=VMEM_LIMIT)


def _dot(a, b):
    return jnp.dot(a, b, preferred_element_type=F32)


def _dot_nt(a, b):
    return lax.dot_general(a, b, (((1,), (1,)), ((), ())), preferred_element_type=F32)


def _rope_tile(acc, cos, sin):
    lane = lax.broadcasted_iota(jnp.int32, cos.shape, 1)
    first_half = (lane % DIFF_QK_DIM) < (DIFF_QK_DIM // 2)
    outs = []
    for g in range(acc.shape[1] // LANES):
        xg = acc[:, g * LANES:(g + 1) * LANES]
        up = pltpu.roll(xg, LANES - DIFF_QK_DIM // 2, axis=1)
        dn = pltpu.roll(xg, DIFF_QK_DIM // 2, axis=1)
        partner = jnp.where(first_half, up, dn)
        outs.append(xg * cos + partner * sin)
    return jnp.concatenate(outs, axis=1)


def _proj_kernel(x_ref, w_ref, *rest, scale, scale_tiles, rope_tiles):
    if rope_tiles:
        cos_ref, sin_ref, o_ref, w_bf = rest
    else:
        o_ref, w_bf = rest
    j = pl.program_id(0)

    @pl.when(pl.program_id(1) == 0)
    def _():
        w_bf[...] = w_ref[...].astype(BF16)

    acc = _dot(x_ref[...], w_bf[...])
    if scale_tiles:
        acc = acc * jnp.where(j < scale_tiles, jnp.float32(scale), jnp.float32(1.0))
    if rope_tiles:
        @pl.when(j < rope_tiles)
        def _():
            o_ref[...] = _rope_tile(acc, cos_ref[...], sin_ref[...]).astype(o_ref.dtype)

        @pl.when(j >= rope_tiles)
        def _():
            o_ref[...] = acc.astype(o_ref.dtype)
    else:
        o_ref[...] = acc.astype(o_ref.dtype)


def _proj(x, w, layer=None, *, scale=1.0, scale_cols=0, rope_cols=0, rope_tabs=None, seq=None):
    n, k = x.shape
    m = w.shape[-1]
    tm = math.gcd(_tile(n, 1024), seq) if rope_cols else _tile(n, 1024)
    tn = math.gcd(1024, m, scale_cols, rope_cols)
    assert tn % LANES == 0
    if layer is None:
        w_spec = pl.BlockSpec((k, tn), lambda j, i: (0, j))
    else:
        w_spec = pl.BlockSpec((None, k, tn), lambda j, i: (layer, 0, j))
    in_specs = [pl.BlockSpec((tm, k), lambda j, i: (i, 0)), w_spec]
    args = [x, w]
    if rope_cols:
        assert seq % tm == 0
        spb = seq // tm
        in_specs += [pl.BlockSpec((tm, LANES), lambda j, i: (i % spb, 0))] * 2
        args += list(rope_tabs)
    kern = functools.partial(_proj_kernel, scale=scale, scale_tiles=scale_cols // tn,
                             rope_tiles=rope_cols // tn)
    return pl.pallas_call(
        kern,
        out_shape=jax.ShapeDtypeStruct((n, m), BF16),
        grid=(m // tn, n // tm),
        in_specs=in_specs,
        out_specs=pl.BlockSpec((tm, tn), lambda j, i: (i, j)),
        scratch_shapes=[pltpu.VMEM((k, tn), BF16)],
        compiler_params=_params(("arbitrary", "arbitrary")),
        name="proj",
    )(*args)


def _layer_norm_rows(v, g, b):
    mu = jnp.mean(v, axis=-1, keepdims=True)
    c = v - mu
    var = jnp.mean(c * c, axis=-1, keepdims=True)
    return c * lax.rsqrt(var + LN_EPS) * g + b


def _pack_bf16_pairs(y):
    c = y.shape[1] // 2
    bits = lax.bitcast_convert_type(y.astype(BF16).astype(F32), jnp.uint32)
    return bits[:, :c] | (bits[:, c:] >> 16)


def _unpack_bf16_pairs(w):
    return (lax.bitcast_convert_type(w & jnp.uint32(0xFFFF0000), F32),
            lax.bitcast_convert_type(w << 16, F32))


def _store_row_tiles(ref, packed):
    m, c = packed.shape
    r = c // LANES
    for s in range(r):
        ref[pl.ds(s, m, stride=r), :] = packed[:, s * LANES:(s + 1) * LANES]


def _load_row_tiles(ref, m, r, first=0):
    return jnp.concatenate([ref[pl.ds(first * r + s, m, stride=r), :] for s in range(r)], axis=1)


def _proj_ln_kernel(x_ref, w_ref, res_ref, g_ref, b_ref, o_ref, op_ref, *, alpha, parts):
    tm = x_ref.shape[0]
    rt = op_ref.shape[0] // tm
    ts = tm // parts
    accs = [_dot(x_ref[p * ts:(p + 1) * ts, :], w_ref[...]) for p in range(parts)]
    for p, acc in enumerate(accs):
        rows = slice(p * ts, (p + 1) * ts)
        y = _layer_norm_rows(alpha * res_ref[rows, :] + acc, g_ref[...], b_ref[...])
        o_ref[rows, :] = y
        _store_row_tiles(op_ref.at[p * ts * rt:(p + 1) * ts * rt, :], _pack_bf16_pairs(y))


def _proj_ln(x, w, res, g, b, alpha):
    n, k = x.shape
    d = w.shape[1]
    tm = _tile(n, 512)
    parts = 2 if tm % 512 == 0 else 1
    r = d // 2 // LANES
    return pl.pallas_call(
        functools.partial(_proj_ln_kernel, alpha=alpha, parts=parts),
        out_shape=(jax.ShapeDtypeStruct((n, d), F32), jax.ShapeDtypeStruct((n * r, LANES), jnp.uint32)),
        grid=(n // tm,),
        in_specs=[pl.BlockSpec((tm, k), lambda i: (i, 0)),
                  pl.BlockSpec((k, d), lambda i: (0, 0)),
                  pl.BlockSpec((tm, d), lambda i: (i, 0)),
                  pl.BlockSpec((1, d), lambda i: (0, 0)),
                  pl.BlockSpec((1, d), lambda i: (0, 0))],
        out_specs=(pl.BlockSpec((tm, d), lambda i: (i, 0)),
                   pl.BlockSpec((tm * r, LANES), lambda i: (i, 0))),
        compiler_params=_params(("arbitrary",)),
        name="proj_ln",
    )(x, w, res, g.reshape(1, d), b.reshape(1, d))


def _sb_attn_kernel(q_ref, k_ref, v_ref, o_ref, *, t, hb):
    qi = pl.program_id(2)
    row = lax.broadcasted_iota(jnp.int32, (2 * t, t), 0) % t
    col = lax.broadcasted_iota(jnp.int32, (2 * t, t), 1)
    later = (row > col).astype(BF16)
    past = (lax.broadcasted_iota(jnp.int32, (t, t), 1)
            < lax.broadcasted_iota(jnp.int32, (t, t), 0))

    def step(kc, accs, tails, masked):
        start = pl.multiple_of(kc * t, t)
        cols = [slice(h * HEAD_DIM, (h + 1) * HEAD_DIM) for h in range(hb)]
        zs = [_dot_nt(q_ref[:, c], k_ref[pl.ds(start, t), c]) for c in cols]
        log_betas, log_keeps, splits = [], [], []
        for z in zs:
            nl = -jnp.log2(1.0 + jnp.exp2(-jnp.abs(z)))
            log_betas.append(nl + jnp.minimum(z, 0.0))
            log_keep = nl - jnp.maximum(z, 0.0)
            if masked:
                log_keep = jnp.where(past, log_keep, 0.0)
            hi = log_keep.astype(BF16)
            lo = (log_keep - hi.astype(F32)).astype(BF16)
            log_keeps.append(log_keep)
            splits.append(jnp.concatenate([hi, lo], axis=1))
        tail_ins = [_dot(s, later) for s in splits]
        ws = []
        for h in range(hb):
            w = jnp.exp2(log_betas[h] + tail_ins[h] + tails[h])
            if masked:
                w = jnp.where(past, w, 0.0)
            ws.append(w.astype(BF16))
        outs = [_dot(ws[h], v_ref[pl.ds(start, t), cols[h]]) for h in range(hb)]
        accs = tuple(a + o for a, o in zip(accs, outs))
        tails = tuple(s + jnp.sum(lk, axis=1, keepdims=True) for s, lk in zip(tails, log_keeps))
        return accs, tails

    def any_weight_left(tails):
        worst = functools.reduce(jnp.maximum, tails)
        return (jnp.max(worst) >= SB_SKIP_BELOW).astype(jnp.int32)

    accs = tuple(jnp.zeros((t, HEAD_DIM), F32) for _ in range(hb))
    tails = tuple(jnp.zeros((t, 1), F32) for _ in range(hb))
    accs, tails = step(qi, accs, tails, True)

    def cond(c):
        return jnp.logical_and(c[0] >= 0, c[1] > 0)

    def body(c):
        kc, _, accs, tails = c
        accs, tails = step(kc, accs, tails, False)
        return kc - 1, any_weight_left(tails), accs, tails

    _, _, accs, _ = lax.while_loop(cond, body, (qi - 1, any_weight_left(tails), accs, tails))
    for h in range(hb):
        o_ref[:, h * HEAD_DIM:(h + 1) * HEAD_DIM] = accs[h].astype(o_ref.dtype)


def _sb_attention(qkv, batch, seq):
    n, three_d = qkv.shape
    d = three_d // 3
    heads = d // HEAD_DIM
    hb = math.gcd(heads, 16)
    t = _tile(seq, 128)
    nq = seq // t
    nh = heads // hb
    w = hb * HEAD_DIM
    return pl.pallas_call(
        functools.partial(_sb_attn_kernel, t=t, hb=hb),
        out_shape=jax.ShapeDtypeStruct((n, d), BF16),
        grid=(batch, nh, nq),
        in_specs=[pl.BlockSpec((t, w), lambda b, h, i: (b * nq + i, h)),
                  pl.BlockSpec((seq, w), lambda b, h, i: (b, nh + h)),
                  pl.BlockSpec((seq, w), lambda b, h, i: (b, 2 * nh + h))],
        out_specs=pl.BlockSpec((t, w), lambda b, h, i: (b * nq + i, h)),
        compiler_params=_params(("arbitrary", "arbitrary", "arbitrary")),
        name="sb_attn",
    )(qkv, qkv, qkv)


def _diff_attn_kernel(q_ref, k_ref, v_ref, lam_ref, g_ref, o_ref, *, t, hb, lam_init):
    qi = pl.program_id(2)
    lane = lax.broadcasted_iota(jnp.int32, (t, HEAD_DIM), 1)
    q_pos = lax.broadcasted_iota(jnp.int32, (t, t), 0)
    k_pos = lax.broadcasted_iota(jnp.int32, (t, t), 1)
    causal = k_pos <= q_pos
    cols = [slice(h * HEAD_DIM, (h + 1) * HEAD_DIM) for h in range(hb)]
    qs = []
    for c in cols:
        q = q_ref[:, c]
        zero = jnp.zeros_like(q)
        qs.append((jnp.where(lane < DIFF_QK_DIM, q, zero),
                   jnp.where(lane >= DIFF_QK_DIM, q, zero)))

    def step(kc, carry, masked):
        start = pl.multiple_of(kc * t, t)
        scores = [[_dot_nt(qx, k_ref[pl.ds(start, t), c]) for qx in qh] for qh, c in zip(qs, cols)]
        out = []
        for h in range(hb):
            v = v_ref[pl.ds(start, t), cols[h]]
            streams = []
            for s, (m, l, acc) in zip(scores[h], carry[h]):
                if masked:
                    s = jnp.where(causal, s, NEG_BIG)
                m_new = jnp.maximum(m, jnp.max(s, axis=1, keepdims=True))
                p = jnp.exp2(s - m_new)
                corr = jnp.exp2(m - m_new)
                l = corr * l + jnp.sum(p, axis=1, keepdims=True)
                acc = corr * acc + _dot(p.astype(BF16), v)
                streams.append((m_new, l, acc))
            out.append(tuple(streams))
        return tuple(out)

    init = (jnp.full((t, 1), NEG_BIG, F32), jnp.zeros((t, 1), F32), jnp.zeros((t, HEAD_DIM), F32))
    carry = step(qi, ((init, init),) * hb, True)
    carry = lax.fori_loop(0, qi, lambda c, s: step(c, s, False), carry)

    lp = lam_ref[...]
    lam = (jnp.exp(jnp.sum(lp[0:1] * lp[1:2], axis=1, keepdims=True))
           - jnp.exp(jnp.sum(lp[2:3] * lp[3:4], axis=1, keepdims=True)) + lam_init)
    for h in range(hb):
        (_, l1, a1), (_, l2, a2) = carry[h]
        o = a1 / l1 - lam * (a2 / l2)
        o = o * lax.rsqrt(jnp.mean(o * o, axis=1, keepdims=True) + SUBLN_EPS)
        o_ref[:, cols[h]] = (o * g_ref[...] * (1.0 - lam_init)).astype(o_ref.dtype)


def _diff_attention(qd, kv, lam_params, subln_g, lam_init, batch, seq):
    n, d = qd.shape
    heads = d // HEAD_DIM
    t = _tile(seq, 512)
    nq = seq // t
    hb = math.gcd(heads, 2)
    nh = heads // hb
    w = hb * HEAD_DIM
    return pl.pallas_call(
        functools.partial(_diff_attn_kernel, t=t, hb=hb, lam_init=lam_init),
        out_shape=jax.ShapeDtypeStruct((n, d), BF16),
        grid=(batch, nh, nq),
        in_specs=[pl.BlockSpec((t, w), lambda b, h, i: (b * nq + i, h)),
                  pl.BlockSpec((seq, w), lambda b, h, i: (b, h)),
                  pl.BlockSpec((seq, w), lambda b, h, i: (b, nh + h)),
                  pl.BlockSpec(lam_params.shape, lambda b, h, i: (0, 0)),
                  pl.BlockSpec((1, HEAD_DIM), lambda b, h, i: (0, 0))],
        out_specs=pl.BlockSpec((t, w), lambda b, h, i: (b * nq + i, h)),
        compiler_params=_params(("arbitrary", "arbitrary", "arbitrary")),
        name="diff_attn",
    )(qd, kv, kv, lam_params, subln_g.reshape(1, HEAD_DIM))


def _split_bf16(x):
    hi = x.astype(BF16)
    return hi, (x - hi.astype(F32)).astype(BF16)


def _route_rows(h, w_ref, info_ref, cnt_ref):
    tm = h.shape[0]
    i = pl.program_id(0)

    @pl.when(i == 0)
    def _():
        cnt_ref[...] = jnp.zeros_like(cnt_ref)

    h_hi, h_lo = _split_bf16(h)
    w_hi, w_lo = _split_bf16(w_ref[...])
    both_w = _dot(h_hi, jnp.concatenate([w_hi, w_lo], axis=1))
    logits = both_w[:, :LANES] + (both_w[:, LANES:] + _dot(h_lo, w_hi))

    lane = lax.broadcasted_iota(jnp.int32, (tm, LANES), 1).astype(F32)

    def first_max(mask, vals):
        top = jnp.max(jnp.where(mask, vals, NEG_BIG), axis=1, keepdims=True)
        idx = jnp.min(jnp.where(mask & (vals == top), lane, float(LANES)), axis=1, keepdims=True)
        return top, idx

    g_mask = lane < N_GROUPS
    g_top, g_sel = first_max(g_mask, logits)
    g_den = jnp.sum(jnp.where(g_mask, jnp.exp(logits - g_top), 0.0), axis=1, keepdims=True)
    g_w = 1.0 / g_den

    lo_lane = ROUTE_LANE0 + EXPERTS_PER_GROUP * g_sel
    in_group = (lane >= lo_lane) & (lane < lo_lane + EXPERTS_PER_GROUP)
    v1, i1 = first_max(in_group, logits)
    v2, i2 = first_max(in_group & (lane != i1), logits)
    ex = jnp.exp(v2 - v1)
    w1 = g_w / (1.0 + ex)
    w2 = w1 * ex

    m1 = (lane == i1).astype(F32)
    m2 = (lane == i2).astype(F32)
    both = m1 + m2
    r = lax.broadcasted_iota(jnp.int32, (tm, tm), 0)
    c = lax.broadcasted_iota(jnp.int32, (tm, tm), 1)
    earlier = (c < r).astype(BF16)
    before = _dot(earlier, both.astype(BF16)) + cnt_ref[0:1, :]
    rank1 = jnp.sum(m1 * before, axis=1, keepdims=True)
    rank2 = jnp.sum(m2 * before, axis=1, keepdims=True)
    cnt_ref[...] = cnt_ref[...] + jnp.sum(both, axis=0, keepdims=True)

    info = jnp.where(lane == 0, i1 - ROUTE_LANE0, 0.0)
    info = jnp.where(lane == 1, i2 - ROUTE_LANE0, info)
    info = jnp.where(lane == 2, rank1, info)
    info = jnp.where(lane == 3, rank2, info)
    info = jnp.where(lane == 4, w1, info)
    info = jnp.where(lane == 5, w2, info)
    info_ref[...] = info


def _route_kernel(h_ref, w_ref, info_ref, cnt_ref):
    _route_rows(h_ref[...], w_ref, info_ref, cnt_ref)


def _route(h, w_route):
    n, d = h.shape
    tm = _tile(n, 512)
    return pl.pallas_call(
        _route_kernel,
        out_shape=(jax.ShapeDtypeStruct((n, LANES), F32), jax.ShapeDtypeStruct((8, LANES), F32)),
        grid=(n // tm,),
        in_specs=[pl.BlockSpec((tm, d), lambda i: (i, 0)),
                  pl.BlockSpec((d, LANES), lambda i: (0, 0))],
        out_specs=(pl.BlockSpec((tm, LANES), lambda i: (i, 0)),
                   pl.BlockSpec((8, LANES), lambda i: (0, 0))),
        compiler_params=_params(("arbitrary",)),
        name="route",
    )(h, w_route)


def _token_rows(ref, token, rt, count=1):
    return ref.at[pl.ds(pl.multiple_of(token * rt, rt), rt * count), :]


def _dispatch_kernel(slot_ref, pad_ref, used_ref, h_ref, xs_hbm, zrow, sem, pad_sem, *, tb, tm, rt, n_tiles):
    i = pl.program_id(0)
    base = i * tb

    def row_copy(r, s):
        return pltpu.make_async_copy(_token_rows(h_ref, r, rt), _token_rows(xs_hbm, s, rt), sem)

    def issue(r, _):
        t = base + r
        row_copy(r, slot_ref[2 * t]).start()
        row_copy(r, slot_ref[2 * t + 1]).start()
        return 0

    lax.fori_loop(0, tb, issue, 0, unroll=DMA_ISSUE_UNROLL)

    def for_each_pad_block(act):
        for e in range(N_EXPERTS):
            first, length = pad_ref[e], pad_ref[N_EXPERTS + e]
            for bit in reversed(range(tm.bit_length() - 1)):
                size = 1 << bit
                slot = first + ((length >> (bit + 1)) << (bit + 1))

                @pl.when(((length >> bit) & 1) == 1)
                def _():
                    act(pltpu.make_async_copy(_token_rows(zrow, 0, rt, size), _token_rows(xs_hbm, slot, rt, size),
                                              pad_sem))

    @pl.when(i == 0)
    def _():
        zrow[...] = jnp.zeros_like(zrow)

        def tile_copy(t):
            return pltpu.make_async_copy(zrow, _token_rows(xs_hbm, t * tm, rt, tm), pad_sem)

        def issue_tile(t, _):
            tile_copy(t).start()
            return 0

        def drain_tile(t, _):
            tile_copy(t).wait()
            return 0

        for_each_pad_block(lambda copy: copy.start())
        lax.fori_loop(used_ref[0], n_tiles, issue_tile, 0)
        for_each_pad_block(lambda copy: copy.wait())
        lax.fori_loop(used_ref[0], n_tiles, drain_tile, 0)

    whole_tile = pltpu.make_async_copy(h_ref, _token_rows(xs_hbm, 0, rt, tb), sem)
    whole_tile.wait()
    whole_tile.wait()


def _dispatch(hp, slots, pads, tiles_used, n, tm):
    assert tm & (tm - 1) == 0
    rt = hp.shape[0] // n
    n_tiles = 2 * n // tm + N_EXPERTS
    tb = _tile(n, 1024)
    return pl.pallas_call(
        functools.partial(_dispatch_kernel, tb=tb, tm=tm, rt=rt, n_tiles=n_tiles),
        out_shape=jax.ShapeDtypeStruct((n_tiles * tm * rt, LANES), hp.dtype),
        grid_spec=pltpu.PrefetchScalarGridSpec(
            num_scalar_prefetch=3,
            grid=(n // tb,),
            in_specs=[pl.BlockSpec((tb * rt, LANES), lambda i, s, p, u: (i, 0))],
            out_specs=pl.BlockSpec(memory_space=pl.ANY),
            scratch_shapes=[pltpu.VMEM((tm * rt, LANES), hp.dtype), pltpu.SemaphoreType.DMA(()),
                            pltpu.SemaphoreType.DMA(())],
        ),
        compiler_params=_params(("arbitrary",)),
        name="moe_dispatch",
    )(slots, pads, tiles_used, hp)


def _moe_up_kernel(te_ref, fresh_ref, valid_ref, x_ref, wg_ref, wu_ref, o_ref, wg_bf, wu_bf):
    i = pl.program_id(0)
    tm = o_ref.shape[0]
    rt = x_ref.shape[0] // tm
    c = rt * LANES

    @pl.when(fresh_ref[i] == 1)
    def _():
        wg_bf[...] = wg_ref[...].astype(BF16)
        wu_bf[...] = wu_ref[...].astype(BF16)

    @pl.when(valid_ref[i] == 1)
    def _():
        xa, xb = _unpack_bf16_pairs(_load_row_tiles(x_ref, tm, rt))
        xa, xb = xa.astype(BF16), xb.astype(BF16)
        g = _dot(xa, wg_bf[:c, :]) + _dot(xb, wg_bf[c:, :])
        u = _dot(xa, wu_bf[:c, :]) + _dot(xb, wu_bf[c:, :])
        o_ref[...] = (g * (1.0 / (1.0 + jnp.exp(-g))) * u).astype(o_ref.dtype)

    @pl.when(valid_ref[i] == 0)
    def _():
        o_ref[...] = jnp.zeros_like(o_ref)


def _tile_or_first(i, valid_ref):
    return jnp.where(valid_ref[i] == 1, i, 0)


def _moe_up(xs, w_gate, w_up, layer, tile_expert, fresh, valid, tm):
    _, _, d, f = w_gate.shape
    rt = d // 2 // LANES
    p = xs.shape[0] // rt
    w_spec = pl.BlockSpec((None, None, d, f), lambda i, te, fr, va: (layer, te[i], 0, 0))
    return pl.pallas_call(
        _moe_up_kernel,
        out_shape=jax.ShapeDtypeStruct((p, f), BF16),
        grid_spec=pltpu.PrefetchScalarGridSpec(
            num_scalar_prefetch=3,
            grid=(p // tm,),
            in_specs=[pl.BlockSpec((tm * rt, LANES), lambda i, te, fr, va: (_tile_or_first(i, va), 0)),
                      w_spec, w_spec],
            out_specs=pl.BlockSpec((tm, f), lambda i, te, fr, va: (i, 0)),
            scratch_shapes=[pltpu.VMEM((d, f), BF16), pltpu.VMEM((d, f), BF16)],
        ),
        compiler_params=_params(("arbitrary",)),
        name="moe_up",
    )(tile_expert, fresh, valid, xs, w_gate, w_up)


def _moe_down_kernel(te_ref, fresh_ref, valid_ref, a_ref, wd_ref, o_ref, wd_bf):
    i = pl.program_id(0)

    @pl.when(fresh_ref[i] == 1)
    def _():
        wd_bf[...] = wd_ref[...].astype(BF16)

    @pl.when(valid_ref[i] == 1)
    def _():
        _store_row_tiles(o_ref, _pack_bf16_pairs(_dot(a_ref[...], wd_bf[...])))

    @pl.when(valid_ref[i] == 0)
    def _():
        o_ref[...] = jnp.zeros_like(o_ref)


def _moe_down(act, w_down, layer, tile_expert, fresh, valid, tm):
    p, f = act.shape
    d = w_down.shape[3]
    rt = d // 2 // LANES
    return pl.pallas_call(
        _moe_down_kernel,
        out_shape=jax.ShapeDtypeStruct((p * rt, LANES), jnp.uint32),
        grid_spec=pltpu.PrefetchScalarGridSpec(
            num_scalar_prefetch=3,
            grid=(p // tm,),
            in_specs=[pl.BlockSpec((tm, f), lambda i, te, fr, va: (_tile_or_first(i, va), 0)),
                      pl.BlockSpec((None, None, f, d), lambda i, te, fr, va: (layer, te[i], 0, 0))],
            out_specs=pl.BlockSpec((tm * rt, LANES), lambda i, te, fr, va: (i, 0)),
            scratch_shapes=[pltpu.VMEM((f, d), BF16)],
        ),
        compiler_params=_params(("arbitrary",)),
        name="moe_down",
    )(tile_expert, fresh, valid, act, w_down)


def _combine_ln_kernel(slot_ref, h_ref, info_ref, g_ref, b_ref, ys_hbm, o_ref, ob_ref, buf, sems, *, tm, rt, alpha):
    i = pl.program_id(0)

    def row_copy(s, par, r):
        return pltpu.make_async_copy(_token_rows(ys_hbm, s, rt), _token_rows(buf.at[par], r, rt), sems.at[par])

    def gather_tile(tile, par):
        def issue(r, _):
            t = tile * tm + r
            row_copy(slot_ref[2 * t], par, r).start()
            row_copy(slot_ref[2 * t + 1], par, tm + r).start()
            return 0
        lax.fori_loop(0, tm, issue, 0, unroll=DMA_ISSUE_UNROLL)

    @pl.when(i == 0)
    def _():
        gather_tile(0, 0)

    @pl.when(i + 1 < pl.num_programs(0))
    def _():
        gather_tile(i + 1, (i + 1) % 2)

    par = i % 2
    pltpu.make_async_copy(_token_rows(ys_hbm, 0, rt, 2 * tm), buf.at[par], sems.at[par]).wait()

    info = info_ref[...]
    w1, w2 = info[:, 4:5], info[:, 5:6]
    a1, b1 = _unpack_bf16_pairs(_load_row_tiles(buf.at[par], tm, rt))
    a2, b2 = _unpack_bf16_pairs(_load_row_tiles(buf.at[par], tm, rt, first=tm))
    ffn = jnp.concatenate([w1 * a1 + w2 * a2, w1 * b1 + w2 * b2], axis=1)
    y = _layer_norm_rows(alpha * h_ref[...] + ffn, g_ref[...], b_ref[...])
    o_ref[...] = y
    ob_ref[...] = y.astype(BF16)


def _combine_ln(h, info, ys, slots, g, b, alpha):
    n, d = h.shape
    tm = _tile(n, 512)
    rt = d // 2 // LANES
    return pl.pallas_call(
        functools.partial(_combine_ln_kernel, tm=tm, rt=rt, alpha=alpha),
        out_shape=(jax.ShapeDtypeStruct((n, d), F32), jax.ShapeDtypeStruct((n, d), BF16)),
        grid_spec=pltpu.PrefetchScalarGridSpec(
            num_scalar_prefetch=1,
            grid=(n // tm,),
            in_specs=[pl.BlockSpec((tm, d), lambda i, s: (i, 0)),
                      pl.BlockSpec((tm, LANES), lambda i, s: (i, 0)),
                      pl.BlockSpec((1, d), lambda i, s: (0, 0)),
                      pl.BlockSpec((1, d), lambda i, s: (0, 0)),
                      pl.BlockSpec(memory_space=pl.ANY)],
            out_specs=(pl.BlockSpec((tm, d), lambda i, s: (i, 0)),
                       pl.BlockSpec((tm, d), lambda i, s: (i, 0))),
            scratch_shapes=[pltpu.VMEM((2, 2 * tm * rt, LANES), jnp.uint32), pltpu.SemaphoreType.DMA((2,))],
        ),
        compiler_params=_params(("arbitrary",)),
        name="moe_combine_ln",
    )(slots, h, info, g.reshape(1, d), b.reshape(1, d), ys)


def _route_weights(w_group, w_inner):
    d = w_group.shape[0]
    return jnp.concatenate(
        [w_group, w_inner.transpose(1, 0, 2).reshape(d, N_EXPERTS),
         jnp.zeros((d, LANES - N_GROUPS - N_EXPERTS), F32)], axis=1)


def _moe_ln(h, hp, info, cnt, layer, w_gate, w_up, w_down, ln_g, ln_b, alpha):
    n, d = h.shape
    tm = _tile(2 * n // N_EXPERTS, 512)
    n_rows = 2 * n + N_EXPERTS * tm
    n_tiles = n_rows // tm

    expert = info[:, 0:2].astype(jnp.int32)
    rank = info[:, 2:4].astype(jnp.int32)
    counts = cnt[0, ROUTE_LANE0:ROUTE_LANE0 + N_EXPERTS].astype(jnp.int32)
    padded = (counts + tm - 1) // tm * tm
    ends = jnp.cumsum(padded)
    starts = ends - padded
    slots = (starts[expert] + rank).reshape(-1)
    pads = jnp.concatenate([starts + counts, padded - counts])
    tile_ids = jnp.arange(n_tiles, dtype=jnp.int32)
    tile_expert = jnp.minimum(jnp.sum((tile_ids[:, None] >= (ends // tm)[None, :]).astype(jnp.int32), axis=1),
                              N_EXPERTS - 1)
    valid = (tile_ids < ends[-1] // tm).astype(jnp.int32)
    fresh = jnp.concatenate([jnp.ones((1,), jnp.int32),
                             (tile_expert[1:] != tile_expert[:-1]).astype(jnp.int32)])

    xs = _dispatch(hp, slots, pads, (ends[-1:] // tm).astype(jnp.int32), n, tm)
    act = _moe_up(xs, w_gate, w_up, layer, tile_expert, fresh, valid, tm)
    ys = _moe_down(act, w_down, layer, tile_expert, fresh, valid, tm)
    return _combine_ln(h, info, ys, slots, ln_g, ln_b, alpha)


def _rope_tables(seq):
    half = DIFF_QK_DIM // 2
    lane = jnp.arange(LANES)
    inv_freq = ROPE_THETA ** (-(lane % half).astype(F32) / half)
    ang = jnp.arange(seq, dtype=F32)[:, None] * inv_freq[None, :]
    sign = jnp.where((lane % DIFF_QK_DIM) < half, -1.0, 1.0)
    return jnp.cos(ang), jnp.sin(ang) * sign


def kernel(x, sb_w_qkv, sb_w_o, shared_w_kv, diff_w_q, diff_w_o, diff_lambda, diff_subln_g, ln_mix_g, ln_mix_b,
           ln_ffn_g, ln_ffn_b, moe_w_group, moe_w_inner, moe_w_gate, moe_w_up, moe_w_down):
    batch, seq, d = x.shape
    n = batch * seq
    depth = ln_mix_g.shape[0]
    n_sb = sb_w_qkv.shape[0]
    alpha = (2.0 * depth) ** 0.25
    rope_tabs = _rope_tables(seq)

    h = x.reshape(n, d)
    hb = h.astype(BF16)
    kv = None
    for layer in range(depth):
        if layer < n_sb:
            qkv = _proj(hb, sb_w_qkv, layer, scale=HEAD_DIM ** -0.5 * LOG2_E, scale_cols=d)
            o = _sb_attention(qkv, batch, seq)
            w_o = sb_w_o[layer]
        else:
            i = layer - n_sb
            if kv is None:
                kv = _proj(hb, shared_w_kv, rope_cols=d, rope_tabs=rope_tabs, seq=seq)
            qd = _proj(hb, diff_w_q, i, scale=DIFF_QK_DIM ** -0.5 * LOG2_E, scale_cols=d,
                       rope_cols=d, rope_tabs=rope_tabs, seq=seq)
            lam_init = 0.8 - 0.6 * math.exp(-0.3 * layer)
            o = _diff_attention(qd, kv, diff_lambda[i], diff_subln_g[i], lam_init, batch, seq)
            w_o = diff_w_o[i]
        h, hp = _proj_ln(o, w_o.astype(BF16), h, ln_mix_g[layer], ln_mix_b[layer], alpha)
        info, cnt = _route(h, _route_weights(moe_w_group[layer], moe_w_inner[layer]))
        h, hb = _moe_ln(h, hp, info, cnt, layer, moe_w_gate, moe_w_up, moe_w_down,
                        ln_ffn_g[layer], ln_ffn_b[layer], alpha)
    return h.reshape(batch, seq, d)
```

```python
import functools
import math

import jax
import jax.numpy as jnp
from jax import lax
from jax.experimental import pallas as pl
from jax.experimental.pallas import tpu as pltpu

HEAD_DIM = 128
DIFF_QK_DIM = HEAD_DIM // 2
N_GROUPS = 4
EXPERTS_PER_GROUP = 4
N_EXPERTS = N_GROUPS * EXPERTS_PER_GROUP
ROPE_THETA = 10000.0
LN_EPS = 1e-5
SUBLN_EPS = 1e-5
LANES = 128
ROUTE_LANE0 = N_GROUPS
NEG_BIG = -1e30
SB_SKIP_BELOW = -150.0
LOG2_E = 1.0 / math.log(2.0)
DMA_ISSUE_UNROLL = 8
VMEM_LIMIT = 56 * 1024 * 1024

F32 = jnp.float32
BF16 = jnp.bfloat16


def _tile(n, pref):
    t = min(n, pref)
    assert n % t == 0, (n, pref)
    return t


def _params(sem):
    return pltpu.CompilerParams(dimension_semantics=sem, vmem_limit_bytes=VMEM_LIMIT)


def _dot(a, b):
    return jnp.dot(a, b, preferred_element_type=F32)


def _dot_nt(a, b):
    return lax.dot_general(a, b, (((1,), (1,)), ((), ())), preferred_element_type=F32)


def _rope_tile(acc, cos, sin):
    lane = lax.broadcasted_iota(jnp.int32, cos.shape, 1)
    first_half = (lane % DIFF_QK_DIM) < (DIFF_QK_DIM // 2)
    outs = []
    for g in range(acc.shape[1] // LANES):
        xg = acc[:, g * LANES:(g + 1) * LANES]
        up = pltpu.roll(xg, LANES - DIFF_QK_DIM // 2, axis=1)
        dn = pltpu.roll(xg, DIFF_QK_DIM // 2, axis=1)
        partner = jnp.where(first_half, up, dn)
        outs.append(xg * cos + partner * sin)
    return jnp.concatenate(outs, axis=1)


def _proj_kernel(x_ref, w_ref, *rest, scale, scale_tiles, rope_tiles):
    if rope_tiles:
        cos_ref, sin_ref, o_ref, w_bf = rest
    else:
        o_ref, w_bf = rest
    j = pl.program_id(0)

    @pl.when(pl.program_id(1) == 0)
    def _():
        w_bf[...] = w_ref[...].astype(BF16)

    acc = _dot(x_ref[...], w_bf[...])
    if scale_tiles:
        acc = acc * jnp.where(j < scale_tiles, jnp.float32(scale), jnp.float32(1.0))
    if rope_tiles:
        @pl.when(j < rope_tiles)
        def _():
            o_ref[...] = _rope_tile(acc, cos_ref[...], sin_ref[...]).astype(o_ref.dtype)

        @pl.when(j >= rope_tiles)
        def _():
            o_ref[...] = acc.astype(o_ref.dtype)
    else:
        o_ref[...] = acc.astype(o_ref.dtype)


def _proj(x, w, layer=None, *, scale=1.0, scale_cols=0, rope_cols=0, rope_tabs=None, seq=None):
    n, k = x.shape
    m = w.shape[-1]
    tm = math.gcd(_tile(n, 1024), seq) if rope_cols else _tile(n, 1024)
    tn = math.gcd(1024, m, scale_cols, rope_cols)
    assert tn % LANES == 0
    if layer is None:
        w_spec = pl.BlockSpec((k, tn), lambda j, i: (0, j))
    else:
        w_spec = pl.BlockSpec((None, k, tn), lambda j, i: (layer, 0, j))
    in_specs = [pl.BlockSpec((tm, k), lambda j, i: (i, 0)), w_spec]
    args = [x, w]
    if rope_cols:
        assert seq % tm == 0
        spb = seq // tm
        in_specs += [pl.BlockSpec((tm, LANES), lambda j, i: (i % spb, 0))] * 2
        args += list(rope_tabs)
    kern = functools.partial(_proj_kernel, scale=scale, scale_tiles=scale_cols // tn,
                             rope_tiles=rope_cols // tn)
    return pl.pallas_call(
        kern,
        out_shape=jax.ShapeDtypeStruct((n, m), BF16),
        grid=(m // tn, n // tm),
        in_specs=in_specs,
        out_specs=pl.BlockSpec((tm, tn), lambda j, i: (i, j)),
        scratch_shapes=[pltpu.VMEM((k, tn), BF16)],
        compiler_params=_params(("arbitrary", "arbitrary")),
        name="proj",
    )(*args)


def _layer_norm_rows(v, g, b):
    mu = jnp.mean(v, axis=-1, keepdims=True)
    c = v - mu
    var = jnp.mean(c * c, axis=-1, keepdims=True)
    return c * lax.rsqrt(var + LN_EPS) * g + b


def _pack_bf16_pairs(y):
    c = y.shape[1] // 2
    bits = lax.bitcast_convert_type(y.astype(BF16).astype(F32), jnp.uint32)
    return bits[:, :c] | (bits[:, c:] >> 16)


def _unpack_bf16_pairs(w):
    return (lax.bitcast_convert_type(w & jnp.uint32(0xFFFF0000), F32),
            lax.bitcast_convert_type(w << 16, F32))


def _store_row_tiles(ref, packed):
    m, c = packed.shape
    r = c // LANES
    for s in range(r):
        ref[pl.ds(s, m, stride=r), :] = packed[:, s * LANES:(s + 1) * LANES]


def _load_row_tiles(ref, m, r, first=0):
    return jnp.concatenate([ref[pl.ds(first * r + s, m, stride=r), :] for s in range(r)], axis=1)


def _proj_ln_kernel(x_ref, w_ref, res_ref, g_ref, b_ref, o_ref, op_ref, *, alpha, parts):
    tm = x_ref.shape[0]
    rt = op_ref.shape[0] // tm
    ts = tm // parts
    accs = [_dot(x_ref[p * ts:(p + 1) * ts, :], w_ref[...]) for p in range(parts)]
    for p, acc in enumerate(accs):
        rows = slice(p * ts, (p + 1) * ts)
        y = _layer_norm_rows(alpha * res_ref[rows, :] + acc, g_ref[...], b_ref[...])
        o_ref[rows, :] = y
        _store_row_tiles(op_ref.at[p * ts * rt:(p + 1) * ts * rt, :], _pack_bf16_pairs(y))


def _proj_ln(x, w, res, g, b, alpha):
    n, k = x.shape
    d = w.shape[1]
    tm = _tile(n, 512)
    parts = 2 if tm % 512 == 0 else 1
    r = d // 2 // LANES
    return pl.pallas_call(
        functools.partial(_proj_ln_kernel, alpha=alpha, parts=parts),
        out_shape=(jax.ShapeDtypeStruct((n, d), F32), jax.ShapeDtypeStruct((n * r, LANES), jnp.uint32)),
        grid=(n // tm,),
        in_specs=[pl.BlockSpec((tm, k), lambda i: (i, 0)),
                  pl.BlockSpec((k, d), lambda i: (0, 0)),
                  pl.BlockSpec((tm, d), lambda i: (i, 0)),
                  pl.BlockSpec((1, d), lambda i: (0, 0)),
                  pl.BlockSpec((1, d), lambda i: (0, 0))],
        out_specs=(pl.BlockSpec((tm, d), lambda i: (i, 0)),
                   pl.BlockSpec((tm * r, LANES), lambda i: (i, 0))),
        compiler_params=_params(("arbitrary",)),
        name="proj_ln",
    )(x, w, res, g.reshape(1, d), b.reshape(1, d))


def _sb_attn_kernel(q_ref, k_ref, v_ref, o_ref, *, t, hb):
    qi = pl.program_id(2)
    row = lax.broadcasted_iota(jnp.int32, (2 * t, t), 0) % t
    col = lax.broadcasted_iota(jnp.int32, (2 * t, t), 1)
    later = (row > col).astype(BF16)
    past = (lax.broadcasted_iota(jnp.int32, (t, t), 1)
            < lax.broadcasted_iota(jnp.int32, (t, t), 0))

    def step(kc, accs, tails, masked):
        start = pl.multiple_of(kc * t, t)
        cols = [slice(h * HEAD_DIM, (h + 1) * HEAD_DIM) for h in range(hb)]
        zs = [_dot_nt(q_ref[:, c], k_ref[pl.ds(start, t), c]) for c in cols]
        log_betas, log_keeps, splits = [], [], []
        for z in zs:
            nl = -jnp.log2(1.0 + jnp.exp2(-jnp.abs(z)))
            log_betas.append(nl + jnp.minimum(z, 0.0))
            log_keep = nl - jnp.maximum(z, 0.0)
            if masked:
                log_keep = jnp.where(past, log_keep, 0.0)
            hi = log_keep.astype(BF16)
            lo = (log_keep - hi.astype(F32)).astype(BF16)
            log_keeps.append(log_keep)
            splits.append(jnp.concatenate([hi, lo], axis=1))
        tail_ins = [_dot(s, later) for s in splits]
        new_tails = tuple(s + jnp.sum(lk, axis=1, keepdims=True) for s, lk in zip(tails, log_keeps))
        more = any_weight_left(new_tails)
        ws = []
        for h in range(hb):
            w = jnp.exp2(log_betas[h] + tail_ins[h] + tails[h])
            if masked:
                w = jnp.where(past, w, 0.0)
            ws.append(w.astype(BF16))
        outs = [_dot(ws[h], v_ref[pl.ds(start, t), cols[h]]) for h in range(hb)]
        accs = tuple(a + o for a, o in zip(accs, outs))
        return accs, new_tails, more

    def any_weight_left(tails):
        worst = functools.reduce(jnp.maximum, tails)
        return (jnp.max(worst) >= SB_SKIP_BELOW).astype(jnp.int32)

    accs = tuple(jnp.zeros((t, HEAD_DIM), F32) for _ in range(hb))
    tails = tuple(jnp.zeros((t, 1), F32) for _ in range(hb))
    accs, tails, more = step(qi, accs, tails, True)

    def cond(c):
        return jnp.logical_and(c[0] >= 0, c[1] > 0)

    def body(c):
        kc, _, accs, tails = c
        accs, tails, more = step(kc, accs, tails, False)
        return kc - 1, more, accs, tails

    _, _, accs, _ = lax.while_loop(cond, body, (qi - 1, more, accs, tails))
    for h in range(hb):
        o_ref[:, h * HEAD_DIM:(h + 1) * HEAD_DIM] = accs[h].astype(o_ref.dtype)


def _sb_attention(qkv, batch, seq):
    n, three_d = qkv.shape
    d = three_d // 3
    heads = d // HEAD_DIM
    hb = math.gcd(heads, 16)
    t = _tile(seq, 128)
    nq = seq // t
    nh = heads // hb
    w = hb * HEAD_DIM
    return pl.pallas_call(
        functools.partial(_sb_attn_kernel, t=t, hb=hb),
        out_shape=jax.ShapeDtypeStruct((n, d), BF16),
        grid=(batch, nh, nq),
        in_specs=[pl.BlockSpec((t, w), lambda b, h, i: (b * nq + i, h)),
                  pl.BlockSpec((seq, w), lambda b, h, i: (b, nh + h)),
                  pl.BlockSpec((seq, w), lambda b, h, i: (b, 2 * nh + h))],
        out_specs=pl.BlockSpec((t, w), lambda b, h, i: (b * nq + i, h)),
        compiler_params=_params(("arbitrary", "arbitrary", "arbitrary")),
        name="sb_attn",
    )(qkv, qkv, qkv)


def _diff_attn_kernel(q_ref, k_ref, v_ref, lam_ref, g_ref, o_ref, *, t, hb, lam_init):
    qi = pl.program_id(2)
    lane = lax.broadcasted_iota(jnp.int32, (t, HEAD_DIM), 1)
    q_pos = lax.broadcasted_iota(jnp.int32, (t, t), 0)
    k_pos = lax.broadcasted_iota(jnp.int32, (t, t), 1)
    causal = k_pos <= q_pos
    cols = [slice(h * HEAD_DIM, (h + 1) * HEAD_DIM) for h in range(hb)]
    qs = []
    for c in cols:
        q = q_ref[:, c]
        zero = jnp.zeros_like(q)
        qs.append((jnp.where(lane < DIFF_QK_DIM, q, zero),
                   jnp.where(lane >= DIFF_QK_DIM, q, zero)))

    def step(kc, carry, masked):
        start = pl.multiple_of(kc * t, t)
        scores = [[_dot_nt(qx, k_ref[pl.ds(start, t), c]) for qx in qh] for qh, c in zip(qs, cols)]
        out = []
        for h in range(hb):
            v = v_ref[pl.ds(start, t), cols[h]]
            streams = []
            for s, (m, l, acc) in zip(scores[h], carry[h]):
                if masked:
                    s = jnp.where(causal, s, NEG_BIG)
                m_new = jnp.maximum(m, jnp.max(s, axis=1, keepdims=True))
                p = jnp.exp2(s - m_new)
                corr = jnp.exp2(m - m_new)
                l = corr * l + jnp.sum(p, axis=1, keepdims=True)
                acc = corr * acc + _dot(p.astype(BF16), v)
                streams.append((m_new, l, acc))
            out.append(tuple(streams))
        return tuple(out)

    init = (jnp.full((t, 1), NEG_BIG, F32), jnp.zeros((t, 1), F32), jnp.zeros((t, HEAD_DIM), F32))
    carry = step(qi, ((init, init),) * hb, True)
    carry = lax.fori_loop(0, qi, lambda c, s: step(c, s, False), carry)

    lp = lam_ref[...]
    lam = (jnp.exp(jnp.sum(lp[0:1] * lp[1:2], axis=1, keepdims=True))
           - jnp.exp(jnp.sum(lp[2:3] * lp[3:4], axis=1, keepdims=True)) + lam_init)
    for h in range(hb):
        (_, l1, a1), (_, l2, a2) = carry[h]
        o = a1 / l1 - lam * (a2 / l2)
        o = o * lax.rsqrt(jnp.mean(o * o, axis=1, keepdims=True) + SUBLN_EPS)
        o_ref[:, cols[h]] = (o * g_ref[...] * (1.0 - lam_init)).astype(o_ref.dtype)


def _diff_attention(qd, kv, lam_params, subln_g, lam_init, batch, seq):
    n, d = qd.shape
    heads = d // HEAD_DIM
    t = _tile(seq, 512)
    nq = seq // t
    hb = math.gcd(heads, 2)
    nh = heads // hb
    w = hb * HEAD_DIM
    return pl.pallas_call(
        functools.partial(_diff_attn_kernel, t=t, hb=hb, lam_init=lam_init),
        out_shape=jax.ShapeDtypeStruct((n, d), BF16),
        grid=(batch, nh, nq),
        in_specs=[pl.BlockSpec((t, w), lambda b, h, i: (b * nq + i, h)),
                  pl.BlockSpec((seq, w), lambda b, h, i: (b, h)),
                  pl.BlockSpec((seq, w), lambda b, h, i: (b, nh + h)),
                  pl.BlockSpec(lam_params.shape, lambda b, h, i: (0, 0)),
                  pl.BlockSpec((1, HEAD_DIM), lambda b, h, i: (0, 0))],
        out_specs=pl.BlockSpec((t, w), lambda b, h, i: (b * nq + i, h)),
        compiler_params=_params(("arbitrary", "arbitrary", "arbitrary")),
        name="diff_attn",
    )(qd, kv, kv, lam_params, subln_g.reshape(1, HEAD_DIM))


def _split_bf16(x):
    hi = x.astype(BF16)
    return hi, (x - hi.astype(F32)).astype(BF16)


def _route_rows(h, w_ref, info_ref, cnt_ref):
    tm = h.shape[0]
    i = pl.program_id(0)

    @pl.when(i == 0)
    def _():
        cnt_ref[...] = jnp.zeros_like(cnt_ref)

    h_hi, h_lo = _split_bf16(h)
    w_hi, w_lo = _split_bf16(w_ref[...])
    both_w = _dot(h_hi, jnp.concatenate([w_hi, w_lo], axis=1))
    logits = both_w[:, :LANES] + (both_w[:, LANES:] + _dot(h_lo, w_hi))

    lane = lax.broadcasted_iota(jnp.int32, (tm, LANES), 1).astype(F32)

    def first_max(mask, vals):
        top = jnp.max(jnp.where(mask, vals, NEG_BIG), axis=1, keepdims=True)
        idx = jnp.min(jnp.where(mask & (vals == top), lane, float(LANES)), axis=1, keepdims=True)
        return top, idx

    g_mask = lane < N_GROUPS
    g_top, g_sel = first_max(g_mask, logits)
    g_den = jnp.sum(jnp.where(g_mask, jnp.exp(logits - g_top), 0.0), axis=1, keepdims=True)
    g_w = 1.0 / g_den

    lo_lane = ROUTE_LANE0 + EXPERTS_PER_GROUP * g_sel
    in_group = (lane >= lo_lane) & (lane < lo_lane + EXPERTS_PER_GROUP)
    v1, i1 = first_max(in_group, logits)
    v2, i2 = first_max(in_group & (lane != i1), logits)
    ex = jnp.exp(v2 - v1)
    w1 = g_w / (1.0 + ex)
    w2 = w1 * ex

    m1 = (lane == i1).astype(F32)
    m2 = (lane == i2).astype(F32)
    both = m1 + m2
    r = lax.broadcasted_iota(jnp.int32, (tm, tm), 0)
    c = lax.broadcasted_iota(jnp.int32, (tm, tm), 1)
    earlier = (c < r).astype(BF16)
    before = _dot(earlier, both.astype(BF16)) + cnt_ref[0:1, :]
    rank1 = jnp.sum(m1 * before, axis=1, keepdims=True)
    rank2 = jnp.sum(m2 * before, axis=1, keepdims=True)
    cnt_ref[...] = cnt_ref[...] + jnp.sum(both, axis=0, keepdims=True)

    info = jnp.where(lane == 0, i1 - ROUTE_LANE0, 0.0)
    info = jnp.where(lane == 1, i2 - ROUTE_LANE0, info)
    info = jnp.where(lane == 2, rank1, info)
    info = jnp.where(lane == 3, rank2, info)
    info = jnp.where(lane == 4, w1, info)
    info = jnp.where(lane == 5, w2, info)
    info_ref[...] = info


def _route_kernel(h_ref, w_ref, info_ref, cnt_ref):
    _route_rows(h_ref[...], w_ref, info_ref, cnt_ref)


def _route(h, w_route):
    n, d = h.shape
    tm = _tile(n, 512)
    return pl.pallas_call(
        _route_kernel,
        out_shape=(jax.ShapeDtypeStruct((n, LANES), F32), jax.ShapeDtypeStruct((8, LANES), F32)),
        grid=(n // tm,),
        in_specs=[pl.BlockSpec((tm, d), lambda i: (i, 0)),
                  pl.BlockSpec((d, LANES), lambda i: (0, 0))],
        out_specs=(pl.BlockSpec((tm, LANES), lambda i: (i, 0)),
                   pl.BlockSpec((8, LANES), lambda i: (0, 0))),
        compiler_params=_params(("arbitrary",)),
        name="route",
    )(h, w_route)


def _token_rows(ref, token, rt, count=1):
    return ref.at[pl.ds(pl.multiple_of(token * rt, rt), rt * count), :]


def _dispatch_kernel(slot_ref, pad_ref, used_ref, h_ref, xs_hbm, zrow, sem, pad_sem, *, tb, tm, rt, n_tiles):
    i = pl.program_id(0)
    base = i * tb

    def row_copy(r, s):
        return pltpu.make_async_copy(_token_rows(h_ref, r, rt), _token_rows(xs_hbm, s, rt), sem)

    def issue(r, _):
        t = base + r
        row_copy(r, slot_ref[2 * t]).start()
        row_copy(r, slot_ref[2 * t + 1]).start()
        return 0

    lax.fori_loop(0, tb, issue, 0, unroll=DMA_ISSUE_UNROLL)

    def for_each_pad_block(act):
        for e in range(N_EXPERTS):
            first, length = pad_ref[e], pad_ref[N_EXPERTS + e]
            for bit in reversed(range(tm.bit_length() - 1)):
                size = 1 << bit
                slot = first + ((length >> (bit + 1)) << (bit + 1))

                @pl.when(((length >> bit) & 1) == 1)
                def _():
                    act(pltpu.make_async_copy(_token_rows(zrow, 0, rt, size), _token_rows(xs_hbm, slot, rt, size),
                                              pad_sem))

    @pl.when(i == 0)
    def _():
        zrow[...] = jnp.zeros_like(zrow)

        def tile_copy(t):
            return pltpu.make_async_copy(zrow, _token_rows(xs_hbm, t * tm, rt, tm), pad_sem)

        def issue_tile(t, _):
            tile_copy(t).start()
            return 0

        def drain_tile(t, _):
            tile_copy(t).wait()
            return 0

        for_each_pad_block(lambda copy: copy.start())
        lax.fori_loop(used_ref[0], n_tiles, issue_tile, 0)
        for_each_pad_block(lambda copy: copy.wait())
        lax.fori_loop(used_ref[0], n_tiles, drain_tile, 0)

    whole_tile = pltpu.make_async_copy(h_ref, _token_rows(xs_hbm, 0, rt, tb), sem)
    whole_tile.wait()
    whole_tile.wait()


def _dispatch(hp, slots, pads, tiles_used, n, tm):
    assert tm & (tm - 1) == 0
    rt = hp.shape[0] // n
    n_tiles = 2 * n // tm + N_EXPERTS
    tb = _tile(n, 1024)
    return pl.pallas_call(
        functools.partial(_dispatch_kernel, tb=tb, tm=tm, rt=rt, n_tiles=n_tiles),
        out_shape=jax.ShapeDtypeStruct((n_tiles * tm * rt, LANES), hp.dtype),
        grid_spec=pltpu.PrefetchScalarGridSpec(
            num_scalar_prefetch=3,
            grid=(n // tb,),
            in_specs=[pl.BlockSpec((tb * rt, LANES), lambda i, s, p, u: (i, 0))],
            out_specs=pl.BlockSpec(memory_space=pl.ANY),
            scratch_shapes=[pltpu.VMEM((tm * rt, LANES), hp.dtype), pltpu.SemaphoreType.DMA(()),
                            pltpu.SemaphoreType.DMA(())],
        ),
        compiler_params=_params(("arbitrary",)),
        name="moe_dispatch",
    )(slots, pads, tiles_used, hp)


def _moe_up_kernel(te_ref, fresh_ref, valid_ref, x_ref, wg_ref, wu_ref, o_ref, wg_bf, wu_bf):
    i = pl.program_id(0)
    tm = o_ref.shape[0]
    rt = x_ref.shape[0] // tm
    c = rt * LANES

    @pl.when(fresh_ref[i] == 1)
    def _():
        wg_bf[...] = wg_ref[...].astype(BF16)
        wu_bf[...] = wu_ref[...].astype(BF16)

    @pl.when(valid_ref[i] == 1)
    def _():
        xa, xb = _unpack_bf16_pairs(_load_row_tiles(x_ref, tm, rt))
        xa, xb = xa.astype(BF16), xb.astype(BF16)
        g = _dot(xa, wg_bf[:c, :]) + _dot(xb, wg_bf[c:, :])
        u = _dot(xa, wu_bf[:c, :]) + _dot(xb, wu_bf[c:, :])
        o_ref[...] = (g * (1.0 / (1.0 + jnp.exp(-g))) * u).astype(o_ref.dtype)

    @pl.when(valid_ref[i] == 0)
    def _():
        o_ref[...] = jnp.zeros_like(o_ref)


def _tile_or_first(i, valid_ref):
    return jnp.where(valid_ref[i] == 1, i, 0)


def _moe_up(xs, w_gate, w_up, layer, tile_expert, fresh, valid, tm):
    _, _, d, f = w_gate.shape
    rt = d // 2 // LANES
    p = xs.shape[0] // rt
    w_spec = pl.BlockSpec((None, None, d, f), lambda i, te, fr, va: (layer, te[i], 0, 0))
    return pl.pallas_call(
        _moe_up_kernel,
        out_shape=jax.ShapeDtypeStruct((p, f), BF16),
        grid_spec=pltpu.PrefetchScalarGridSpec(
            num_scalar_prefetch=3,
            grid=(p // tm,),
            in_specs=[pl.BlockSpec((tm * rt, LANES), lambda i, te, fr, va: (_tile_or_first(i, va), 0)),
                      w_spec, w_spec],
            out_specs=pl.BlockSpec((tm, f), lambda i, te, fr, va: (i, 0)),
            scratch_shapes=[pltpu.VMEM((d, f), BF16), pltpu.VMEM((d, f), BF16)],
        ),
        compiler_params=_params(("arbitrary",)),
        name="moe_up",
    )(tile_expert, fresh, valid, xs, w_gate, w_up)


def _moe_down_kernel(te_ref, fresh_ref, valid_ref, a_ref, wd_ref, o_ref, wd_bf):
    i = pl.program_id(0)

    @pl.when(fresh_ref[i] == 1)
    def _():
        wd_bf[...] = wd_ref[...].astype(BF16)

    @pl.when(valid_ref[i] == 1)
    def _():
        _store_row_tiles(o_ref, _pack_bf16_pairs(_dot(a_ref[...], wd_bf[...])))

    @pl.when(valid_ref[i] == 0)
    def _():
        o_ref[...] = jnp.zeros_like(o_ref)


def _moe_down(act, w_down, layer, tile_expert, fresh, valid, tm):
    p, f = act.shape
    d = w_down.shape[3]
    rt = d // 2 // LANES
    return pl.pallas_call(
        _moe_down_kernel,
        out_shape=jax.ShapeDtypeStruct((p * rt, LANES), jnp.uint32),
        grid_spec=pltpu.PrefetchScalarGridSpec(
            num_scalar_prefetch=3,
            grid=(p // tm,),
            in_specs=[pl.BlockSpec((tm, f), lambda i, te, fr, va: (_tile_or_first(i, va), 0)),
                      pl.BlockSpec((None, None, f, d), lambda i, te, fr, va: (layer, te[i], 0, 0))],
            out_specs=pl.BlockSpec((tm * rt, LANES), lambda i, te, fr, va: (i, 0)),
            scratch_shapes=[pltpu.VMEM((f, d), BF16)],
        ),
        compiler_params=_params(("arbitrary",)),
        name="moe_down",
    )(tile_expert, fresh, valid, act, w_down)


def _combine_ln_kernel(slot_ref, h_ref, info_ref, g_ref, b_ref, ys_hbm, o_ref, ob_ref, buf, sems, *, tm, rt, alpha):
    i = pl.program_id(0)

    def row_copy(s, par, r):
        return pltpu.make_async_copy(_token_rows(ys_hbm, s, rt), _token_rows(buf.at[par], r, rt), sems.at[par])

    def gather_tile(tile, par):
        def issue(r, _):
            t = tile * tm + r
            row_copy(slot_ref[2 * t], par, r).start()
            row_copy(slot_ref[2 * t + 1], par, tm + r).start()
            return 0
        lax.fori_loop(0, tm, issue, 0, unroll=DMA_ISSUE_UNROLL)

    @pl.when(i == 0)
    def _():
        gather_tile(0, 0)

    @pl.when(i + 1 < pl.num_programs(0))
    def _():
        gather_tile(i + 1, (i + 1) % 2)

    par = i % 2
    pltpu.make_async_copy(_token_rows(ys_hbm, 0, rt, 2 * tm), buf.at[par], sems.at[par]).wait()

    info = info_ref[...]
    w1, w2 = info[:, 4:5], info[:, 5:6]
    a1, b1 = _unpack_bf16_pairs(_load_row_tiles(buf.at[par], tm, rt))
    a2, b2 = _unpack_bf16_pairs(_load_row_tiles(buf.at[par], tm, rt, first=tm))
    ffn = jnp.concatenate([w1 * a1 + w2 * a2, w1 * b1 + w2 * b2], axis=1)
    y = _layer_norm_rows(alpha * h_ref[...] + ffn, g_ref[...], b_ref[...])
    o_ref[...] = y
    ob_ref[...] = y.astype(BF16)


def _combine_ln(h, info, ys, slots, g, b, alpha):
    n, d = h.shape
    tm = _tile(n, 256)
    rt = d // 2 // LANES
    return pl.pallas_call(
        functools.partial(_combine_ln_kernel, tm=tm, rt=rt, alpha=alpha),
        out_shape=(jax.ShapeDtypeStruct((n, d), F32), jax.ShapeDtypeStruct((n, d), BF16)),
        grid_spec=pltpu.PrefetchScalarGridSpec(
            num_scalar_prefetch=1,
            grid=(n // tm,),
            in_specs=[pl.BlockSpec((tm, d), lambda i, s: (i, 0)),
                      pl.BlockSpec((tm, LANES), lambda i, s: (i, 0)),
                      pl.BlockSpec((1, d), lambda i, s: (0, 0)),
                      pl.BlockSpec((1, d), lambda i, s: (0, 0)),
                      pl.BlockSpec(memory_space=pl.ANY)],
            out_specs=(pl.BlockSpec((tm, d), lambda i, s: (i, 0)),
                       pl.BlockSpec((tm, d), lambda i, s: (i, 0))),
            scratch_shapes=[pltpu.VMEM((2, 2 * tm * rt, LANES), jnp.uint32), pltpu.SemaphoreType.DMA((2,))],
        ),
        compiler_params=_params(("arbitrary",)),
        name="moe_combine_ln",
    )(slots, h, info, g.reshape(1, d), b.reshape(1, d), ys)


def _route_weights(w_group, w_inner):
    d = w_group.shape[0]
    return jnp.concatenate(
        [w_group, w_inner.transpose(1, 0, 2).reshape(d, N_EXPERTS),
         jnp.zeros((d, LANES - N_GROUPS - N_EXPERTS), F32)], axis=1)


def _moe_ln(h, hp, info, cnt, layer, w_gate, w_up, w_down, ln_g, ln_b, alpha):
    n, d = h.shape
    tm = _tile(2 * n // N_EXPERTS, 512)
    n_rows = 2 * n + N_EXPERTS * tm
    n_tiles = n_rows // tm

    expert = info[:, 0:2].astype(jnp.int32)
    rank = info[:, 2:4].astype(jnp.int32)
    counts = cnt[0, ROUTE_LANE0:ROUTE_LANE0 + N_EXPERTS].astype(jnp.int32)
    padded = (counts + tm - 1) // tm * tm
    ends = jnp.cumsum(padded)
    starts = ends - padded
    slots = (starts[expert] + rank).reshape(-1)
    pads = jnp.concatenate([starts + counts, padded - counts])
    tile_ids = jnp.arange(n_tiles, dtype=jnp.int32)
    tile_expert = jnp.minimum(jnp.sum((tile_ids[:, None] >= (ends // tm)[None, :]).astype(jnp.int32), axis=1),
                              N_EXPERTS - 1)
    valid = (tile_ids < ends[-1] // tm).astype(jnp.int32)
    fresh = jnp.concatenate([jnp.ones((1,), jnp.int32),
                             (tile_expert[1:] != tile_expert[:-1]).astype(jnp.int32)])

    xs = _dispatch(hp, slots, pads, (ends[-1:] // tm).astype(jnp.int32), n, tm)
    act = _moe_up(xs, w_gate, w_up, layer, tile_expert, fresh, valid, tm)
    ys = _moe_down(act, w_down, layer, tile_expert, fresh, valid, tm)
    return _combine_ln(h, info, ys, slots, ln_g, ln_b, alpha)


def _rope_tables(seq):
    half = DIFF_QK_DIM // 2
    lane = jnp.arange(LANES)
    inv_freq = ROPE_THETA ** (-(lane % half).astype(F32) / half)
    ang = jnp.arange(seq, dtype=F32)[:, None] * inv_freq[None, :]
    sign = jnp.where((lane % DIFF_QK_DIM) < half, -1.0, 1.0)
    return jnp.cos(ang), jnp.sin(ang) * sign


def kernel(x, sb_w_qkv, sb_w_o, shared_w_kv, diff_w_q, diff_w_o, diff_lambda, diff_subln_g, ln_mix_g, ln_mix_b,
           ln_ffn_g, ln_ffn_b, moe_w_group, moe_w_inner, moe_w_gate, moe_w_up, moe_w_down):
    batch, seq, d = x.shape
    n = batch * seq
    depth = ln_mix_g.shape[0]
    n_sb = sb_w_qkv.shape[0]
    alpha = (2.0 * depth) ** 0.25
    rope_tabs = _rope_tables(seq)

    h = x.reshape(n, d)
    hb = h.astype(BF16)
    kv = None
    for layer in range(depth):
        if layer < n_sb:
            qkv = _proj(hb, sb_w_qkv, layer, scale=HEAD_DIM ** -0.5 * LOG2_E, scale_cols=d)
            o = _sb_attention(qkv, batch, seq)
            w_o = sb_w_o[layer]
        else:
            i = layer - n_sb
            if kv is None:
                kv = _proj(hb, shared_w_kv, rope_cols=d, rope_tabs=rope_tabs, seq=seq)
            qd = _proj(hb, diff_w_q, i, scale=DIFF_QK_DIM ** -0.5 * LOG2_E, scale_cols=d,
                       rope_cols=d, rope_tabs=rope_tabs, seq=seq)
            lam_init = 0.8 - 0.6 * math.exp(-0.3 * layer)
            o = _diff_attention(qd, kv, diff_lambda[i], diff_subln_g[i], lam_init, batch, seq)
            w_o = diff_w_o[i]
        h, hp = _proj_ln(o, w_o.astype(BF16), h, ln_mix_g[layer], ln_mix_b[layer], alpha)
        info, cnt = _route(h, _route_weights(moe_w_group[layer], moe_w_inner[layer]))
        h, hb = _moe_ln(h, hp, info, cnt, layer, moe_w_gate, moe_w_up, moe_w_down,
                        ln_ffn_g[layer], ln_ffn_b[layer], alpha)
    return h.reshape(batch, seq, d)
```

```python
import functools
import math

import jax
import jax.numpy as jnp
from jax import lax
from jax.experimental import pallas as pl
from jax.experimental.pallas import tpu as pltpu

HEAD_DIM = 128
DIFF_QK_DIM = HEAD_DIM // 2
N_GROUPS = 4
EXPERTS_PER_GROUP = 4
N_EXPERTS = N_GROUPS * EXPERTS_PER_GROUP
ROPE_THETA = 10000.0
LN_EPS = 1e-5
SUBLN_EPS = 1e-5
LANES = 128
ROUTE_LANE0 = N_GROUPS
NEG_BIG = -1e30
SB_SKIP_BELOW = -150.0
LOG2_E = 1.0 / math.log(2.0)
DMA_ISSUE_UNROLL = 8
VMEM_LIMIT = 56 * 1024 * 1024

F32 = jnp.float32
BF16 = jnp.bfloat16


def _tile(n, pref):
    t = min(n, pref)
    assert n % t == 0, (n, pref)
    return t


def _params(sem):
    return pltpu.CompilerParams(dimension_semantics=sem, vmem_limit_bytes=VMEM_LIMIT)


def _dot(a, b):
    return jnp.dot(a, b, preferred_element_type=F32)


def _dot_nt(a, b):
    return lax.dot_general(a, b, (((1,), (1,)), ((), ())), preferred_element_type=F32)


def _rope_tile(acc, cos, sin):
    lane = lax.broadcasted_iota(jnp.int32, cos.shape, 1)
    first_half = (lane % DIFF_QK_DIM) < (DIFF_QK_DIM // 2)
    outs = []
    for g in range(acc.shape[1] // LANES):
        xg = acc[:, g * LANES:(g + 1) * LANES]
        up = pltpu.roll(xg, LANES - DIFF_QK_DIM // 2, axis=1)
        dn = pltpu.roll(xg, DIFF_QK_DIM // 2, axis=1)
        partner = jnp.where(first_half, up, dn)
        outs.append(xg * cos + partner * sin)
    return jnp.concatenate(outs, axis=1)


def _proj_kernel(x_ref, w_ref, *rest, scale, scale_tiles, rope_tiles):
    if rope_tiles:
        cos_ref, sin_ref, o_ref, w_bf = rest
    else:
        o_ref, w_bf = rest
    j = pl.program_id(0)

    @pl.when(pl.program_id(1) == 0)
    def _():
        w_bf[...] = w_ref[...].astype(BF16)

    acc = _dot(x_ref[...], w_bf[...])
    if scale_tiles:
        acc = acc * jnp.where(j < scale_tiles, jnp.float32(scale), jnp.float32(1.0))
    if rope_tiles:
        @pl.when(j < rope_tiles)
        def _():
            o_ref[...] = _rope_tile(acc, cos_ref[...], sin_ref[...]).astype(o_ref.dtype)

        @pl.when(j >= rope_tiles)
        def _():
            o_ref[...] = acc.astype(o_ref.dtype)
    else:
        o_ref[...] = acc.astype(o_ref.dtype)


def _proj(x, w, layer=None, *, scale=1.0, scale_cols=0, rope_cols=0, rope_tabs=None, seq=None):
    n, k = x.shape
    m = w.shape[-1]
    tm = math.gcd(_tile(n, 1024), seq) if rope_cols else _tile(n, 1024)
    tn = math.gcd(1024, m, scale_cols, rope_cols)
    assert tn % LANES == 0
    if layer is None:
        w_spec = pl.BlockSpec((k, tn), lambda j, i: (0, j))
    else:
        w_spec = pl.BlockSpec((None, k, tn), lambda j, i: (layer, 0, j))
    in_specs = [pl.BlockSpec((tm, k), lambda j, i: (i, 0)), w_spec]
    args = [x, w]
    if rope_cols:
        assert seq % tm == 0
        spb = seq // tm
        in_specs += [pl.BlockSpec((tm, LANES), lambda j, i: (i % spb, 0))] * 2
        args += list(rope_tabs)
    kern = functools.partial(_proj_kernel, scale=scale, scale_tiles=scale_cols // tn,
                             rope_tiles=rope_cols // tn)
    return pl.pallas_call(
        kern,
        out_shape=jax.ShapeDtypeStruct((n, m), BF16),
        grid=(m // tn, n // tm),
        in_specs=in_specs,
        out_specs=pl.BlockSpec((tm, tn), lambda j, i: (i, j)),
        scratch_shapes=[pltpu.VMEM((k, tn), BF16)],
        compiler_params=_params(("arbitrary", "arbitrary")),
        name="proj",
    )(*args)


def _layer_norm_rows(v, g, b):
    mu = jnp.mean(v, axis=-1, keepdims=True)
    c = v - mu
    var = jnp.mean(c * c, axis=-1, keepdims=True)
    return c * lax.rsqrt(var + LN_EPS) * g + b


def _pack_bf16_pairs(y):
    c = y.shape[1] // 2
    bits = lax.bitcast_convert_type(y.astype(BF16).astype(F32), jnp.uint32)
    return bits[:, :c] | (bits[:, c:] >> 16)


def _unpack_bf16_pairs(w):
    return (lax.bitcast_convert_type(w & jnp.uint32(0xFFFF0000), F32),
            lax.bitcast_convert_type(w << 16, F32))


def _store_row_tiles(ref, packed):
    m, c = packed.shape
    r = c // LANES
    for s in range(r):
        ref[pl.ds(s, m, stride=r), :] = packed[:, s * LANES:(s + 1) * LANES]


def _load_row_tiles(ref, m, r, first=0):
    return jnp.concatenate([ref[pl.ds(first * r + s, m, stride=r), :] for s in range(r)], axis=1)


def _proj_ln_kernel(x_ref, w_ref, res_ref, g_ref, b_ref, o_ref, op_ref, *, alpha, parts):
    tm = x_ref.shape[0]
    rt = op_ref.shape[0] // tm
    ts = tm // parts
    accs = [_dot(x_ref[p * ts:(p + 1) * ts, :], w_ref[...]) for p in range(parts)]
    for p, acc in enumerate(accs):
        rows = slice(p * ts, (p + 1) * ts)
        y = _layer_norm_rows(alpha * res_ref[rows, :] + acc, g_ref[...], b_ref[...])
        o_ref[rows, :] = y
        _store_row_tiles(op_ref.at[p * ts * rt:(p + 1) * ts * rt, :], _pack_bf16_pairs(y))


def _proj_ln(x, w, res, g, b, alpha):
    n, k = x.shape
    d = w.shape[1]
    tm = _tile(n, 512)
    parts = 2 if tm % 512 == 0 else 1
    r = d // 2 // LANES
    return pl.pallas_call(
        functools.partial(_proj_ln_kernel, alpha=alpha, parts=parts),
        out_shape=(jax.ShapeDtypeStruct((n, d), F32), jax.ShapeDtypeStruct((n * r, LANES), jnp.uint32)),
        grid=(n // tm,),
        in_specs=[pl.BlockSpec((tm, k), lambda i: (i, 0)),
                  pl.BlockSpec((k, d), lambda i: (0, 0)),
                  pl.BlockSpec((tm, d), lambda i: (i, 0)),
                  pl.BlockSpec((1, d), lambda i: (0, 0)),
                  pl.BlockSpec((1, d), lambda i: (0, 0))],
        out_specs=(pl.BlockSpec((tm, d), lambda i: (i, 0)),
                   pl.BlockSpec((tm * r, LANES), lambda i: (i, 0))),
        compiler_params=_params(("arbitrary",)),
        name="proj_ln",
    )(x, w, res, g.reshape(1, d), b.reshape(1, d))


def _sb_attn_kernel(q_ref, k_ref, v_ref, o_ref, *, t, hb):
    qi = pl.program_id(2)
    row = lax.broadcasted_iota(jnp.int32, (2 * t, t), 0) % t
    col = lax.broadcasted_iota(jnp.int32, (2 * t, t), 1)
    later = (row > col).astype(BF16)
    past = (lax.broadcasted_iota(jnp.int32, (t, t), 1)
            < lax.broadcasted_iota(jnp.int32, (t, t), 0))

    def step(kc, accs, tails, masked):
        start = pl.multiple_of(kc * t, t)
        cols = [slice(h * HEAD_DIM, (h + 1) * HEAD_DIM) for h in range(hb)]
        zs = [_dot_nt(q_ref[:, c], k_ref[pl.ds(start, t), c]) for c in cols]
        log_betas, log_keeps, splits = [], [], []
        for z in zs:
            nl = -jnp.log2(1.0 + jnp.exp2(-jnp.abs(z)))
            log_betas.append(nl + jnp.minimum(z, 0.0))
            log_keep = nl - jnp.maximum(z, 0.0)
            if masked:
                log_keep = jnp.where(past, log_keep, 0.0)
            hi = log_keep.astype(BF16)
            lo = (log_keep - hi.astype(F32)).astype(BF16)
            log_keeps.append(log_keep)
            splits.append(jnp.concatenate([hi, lo], axis=1))
        tail_ins = [_dot(s, later) for s in splits]
        new_tails = tuple(s + jnp.sum(lk, axis=1, keepdims=True) for s, lk in zip(tails, log_keeps))
        more = any_weight_left(new_tails)
        ws = []
        for h in range(hb):
            w = jnp.exp2(log_betas[h] + tail_ins[h] + tails[h])
            if masked:
                w = jnp.where(past, w, 0.0)
            ws.append(w.astype(BF16))
        outs = [_dot(ws[h], v_ref[pl.ds(start, t), cols[h]]) for h in range(hb)]
        accs = tuple(a + o for a, o in zip(accs, outs))
        return accs, new_tails, more

    def any_weight_left(tails):
        worst = functools.reduce(jnp.maximum, tails)
        return (jnp.max(worst) >= SB_SKIP_BELOW).astype(jnp.int32)

    accs = tuple(jnp.zeros((t, HEAD_DIM), F32) for _ in range(hb))
    tails = tuple(jnp.zeros((t, 1), F32) for _ in range(hb))
    accs, tails, more = step(qi, accs, tails, True)

    def cond(c):
        return jnp.logical_and(c[0] >= 0, c[1] > 0)

    def body(c):
        kc, _, accs, tails = c
        accs, tails, more = step(kc, accs, tails, False)
        return kc - 1, more, accs, tails

    _, _, accs, _ = lax.while_loop(cond, body, (qi - 1, more, accs, tails))
    for h in range(hb):
        o_ref[:, h * HEAD_DIM:(h + 1) * HEAD_DIM] = accs[h].astype(o_ref.dtype)


def _sb_attention(qkv, batch, seq):
    n, three_d = qkv.shape
    d = three_d // 3
    heads = d // HEAD_DIM
    hb = math.gcd(heads, 16)
    t = _tile(seq, 128)
    nq = seq // t
    nh = heads // hb
    w = hb * HEAD_DIM
    return pl.pallas_call(
        functools.partial(_sb_attn_kernel, t=t, hb=hb),
        out_shape=jax.ShapeDtypeStruct((n, d), BF16),
        grid=(batch, nh, nq),
        in_specs=[pl.BlockSpec((t, w), lambda b, h, i: (b * nq + i, h)),
                  pl.BlockSpec((seq, w), lambda b, h, i: (b, nh + h)),
                  pl.BlockSpec((seq, w), lambda b, h, i: (b, 2 * nh + h))],
        out_specs=pl.BlockSpec((t, w), lambda b, h, i: (b * nq + i, h)),
        compiler_params=_params(("arbitrary", "arbitrary", "arbitrary")),
        name="sb_attn",
    )(qkv, qkv, qkv)


def _diff_attn_kernel(q_ref, k_ref, v_ref, lam_ref, g_ref, o_ref, *, t, hb, lam_init):
    qi = pl.program_id(2)
    lane = lax.broadcasted_iota(jnp.int32, (t, HEAD_DIM), 1)
    q_pos = lax.broadcasted_iota(jnp.int32, (t, t), 0)
    k_pos = lax.broadcasted_iota(jnp.int32, (t, t), 1)
    causal = k_pos <= q_pos
    cols = [slice(h * HEAD_DIM, (h + 1) * HEAD_DIM) for h in range(hb)]
    qs = []
    for c in cols:
        q = q_ref[:, c]
        zero = jnp.zeros_like(q)
        qs.append((jnp.where(lane < DIFF_QK_DIM, q, zero),
                   jnp.where(lane >= DIFF_QK_DIM, q, zero)))

    def step(kc, carry, masked):
        start = pl.multiple_of(kc * t, t)
        scores = [[_dot_nt(qx, k_ref[pl.ds(start, t), c]) for qx in qh] for qh, c in zip(qs, cols)]
        out = []
        for h in range(hb):
            v = v_ref[pl.ds(start, t), cols[h]]
            streams = []
            for s, (m, l, acc) in zip(scores[h], carry[h]):
                if masked:
                    s = jnp.where(causal, s, NEG_BIG)
                m_new = jnp.maximum(m, jnp.max(s, axis=1, keepdims=True))
                p = jnp.exp2(s - m_new)
                corr = jnp.exp2(m - m_new)
                l = corr * l + jnp.sum(p, axis=1, keepdims=True)
                acc = corr * acc + _dot(p.astype(BF16), v)
                streams.append((m_new, l, acc))
            out.append(tuple(streams))
        return tuple(out)

    init = (jnp.full((t, 1), NEG_BIG, F32), jnp.zeros((t, 1), F32), jnp.zeros((t, HEAD_DIM), F32))
    carry = step(qi, ((init, init),) * hb, True)
    carry = lax.fori_loop(0, qi, lambda c, s: step(c, s, False), carry)

    lp = lam_ref[...]
    lam = (jnp.exp(jnp.sum(lp[0:1] * lp[1:2], axis=1, keepdims=True))
           - jnp.exp(jnp.sum(lp[2:3] * lp[3:4], axis=1, keepdims=True)) + lam_init)
    for h in range(hb):
        (_, l1, a1), (_, l2, a2) = carry[h]
        o = a1 / l1 - lam * (a2 / l2)
        o = o * lax.rsqrt(jnp.mean(o * o, axis=1, keepdims=True) + SUBLN_EPS)
        o_ref[:, cols[h]] = (o * g_ref[...] * (1.0 - lam_init)).astype(o_ref.dtype)


def _diff_attention(qd, kv, lam_params, subln_g, lam_init, batch, seq):
    n, d = qd.shape
    heads = d // HEAD_DIM
    t = _tile(seq, 512)
    nq = seq // t
    hb = math.gcd(heads, 4)
    nh = heads // hb
    w = hb * HEAD_DIM
    return pl.pallas_call(
        functools.partial(_diff_attn_kernel, t=t, hb=hb, lam_init=lam_init),
        out_shape=jax.ShapeDtypeStruct((n, d), BF16),
        grid=(batch, nh, nq),
        in_specs=[pl.BlockSpec((t, w), lambda b, h, i: (b * nq + i, h)),
                  pl.BlockSpec((seq, w), lambda b, h, i: (b, h)),
                  pl.BlockSpec((seq, w), lambda b, h, i: (b, nh + h)),
                  pl.BlockSpec(lam_params.shape, lambda b, h, i: (0, 0)),
                  pl.BlockSpec((1, HEAD_DIM), lambda b, h, i: (0, 0))],
        out_specs=pl.BlockSpec((t, w), lambda b, h, i: (b * nq + i, h)),
        compiler_params=_params(("arbitrary", "arbitrary", "arbitrary")),
        name="diff_attn",
    )(qd, kv, kv, lam_params, subln_g.reshape(1, HEAD_DIM))


def _split_bf16(x):
    hi = x.astype(BF16)
    return hi, (x - hi.astype(F32)).astype(BF16)


def _route_rows(h, w_ref, info_ref, cnt_ref):
    tm = h.shape[0]
    i = pl.program_id(0)

    @pl.when(i == 0)
    def _():
        cnt_ref[...] = jnp.zeros_like(cnt_ref)

    h_hi, h_lo = _split_bf16(h)
    w_hi, w_lo = _split_bf16(w_ref[...])
    both_w = _dot(h_hi, jnp.concatenate([w_hi, w_lo], axis=1))
    logits = both_w[:, :LANES] + (both_w[:, LANES:] + _dot(h_lo, w_hi))

    lane = lax.broadcasted_iota(jnp.int32, (tm, LANES), 1).astype(F32)

    def first_max(mask, vals):
        top = jnp.max(jnp.where(mask, vals, NEG_BIG), axis=1, keepdims=True)
        idx = jnp.min(jnp.where(mask & (vals == top), lane, float(LANES)), axis=1, keepdims=True)
        return top, idx

    g_mask = lane < N_GROUPS
    g_top, g_sel = first_max(g_mask, logits)
    g_den = jnp.sum(jnp.where(g_mask, jnp.exp(logits - g_top), 0.0), axis=1, keepdims=True)
    g_w = 1.0 / g_den

    lo_lane = ROUTE_LANE0 + EXPERTS_PER_GROUP * g_sel
    in_group = (lane >= lo_lane) & (lane < lo_lane + EXPERTS_PER_GROUP)
    v1, i1 = first_max(in_group, logits)
    v2, i2 = first_max(in_group & (lane != i1), logits)
    ex = jnp.exp(v2 - v1)
    w1 = g_w / (1.0 + ex)
    w2 = w1 * ex

    m1 = (lane == i1).astype(F32)
    m2 = (lane == i2).astype(F32)
    both = m1 + m2
    r = lax.broadcasted_iota(jnp.int32, (tm, tm), 0)
    c = lax.broadcasted_iota(jnp.int32, (tm, tm), 1)
    earlier = (c < r).astype(BF16)
    before = _dot(earlier, both.astype(BF16)) + cnt_ref[0:1, :]
    rank1 = jnp.sum(m1 * before, axis=1, keepdims=True)
    rank2 = jnp.sum(m2 * before, axis=1, keepdims=True)
    cnt_ref[...] = cnt_ref[...] + jnp.sum(both, axis=0, keepdims=True)

    info = jnp.where(lane == 0, i1 - ROUTE_LANE0, 0.0)
    info = jnp.where(lane == 1, i2 - ROUTE_LANE0, info)
    info = jnp.where(lane == 2, rank1, info)
    info = jnp.where(lane == 3, rank2, info)
    info = jnp.where(lane == 4, w1, info)
    info = jnp.where(lane == 5, w2, info)
    info_ref[...] = info


def _route_kernel(h_ref, w_ref, info_ref, cnt_ref):
    _route_rows(h_ref[...], w_ref, info_ref, cnt_ref)


def _route(h, w_route):
    n, d = h.shape
    tm = _tile(n, 512)
    return pl.pallas_call(
        _route_kernel,
        out_shape=(jax.ShapeDtypeStruct((n, LANES), F32), jax.ShapeDtypeStruct((8, LANES), F32)),
        grid=(n // tm,),
        in_specs=[pl.BlockSpec((tm, d), lambda i: (i, 0)),
                  pl.BlockSpec((d, LANES), lambda i: (0, 0))],
        out_specs=(pl.BlockSpec((tm, LANES), lambda i: (i, 0)),
                   pl.BlockSpec((8, LANES), lambda i: (0, 0))),
        compiler_params=_params(("arbitrary",)),
        name="route",
    )(h, w_route)


def _token_rows(ref, token, rt, count=1):
    return ref.at[pl.ds(pl.multiple_of(token * rt, rt), rt * count), :]


def _dispatch_kernel(slot_ref, pad_ref, used_ref, h_ref, xs_hbm, zrow, sem, pad_sem, *, tb, tm, rt, n_tiles):
    i = pl.program_id(0)
    base = i * tb

    def row_copy(r, s):
        return pltpu.make_async_copy(_token_rows(h_ref, r, rt), _token_rows(xs_hbm, s, rt), sem)

    def issue(r, _):
        t = base + r
        row_copy(r, slot_ref[2 * t]).start()
        row_copy(r, slot_ref[2 * t + 1]).start()
        return 0

    lax.fori_loop(0, tb, issue, 0, unroll=DMA_ISSUE_UNROLL)

    def for_each_pad_block(act):
        for e in range(N_EXPERTS):
            first, length = pad_ref[e], pad_ref[N_EXPERTS + e]
            for bit in reversed(range(tm.bit_length() - 1)):
                size = 1 << bit
                slot = first + ((length >> (bit + 1)) << (bit + 1))

                @pl.when(((length >> bit) & 1) == 1)
                def _():
                    act(pltpu.make_async_copy(_token_rows(zrow, 0, rt, size), _token_rows(xs_hbm, slot, rt, size),
                                              pad_sem))

    @pl.when(i == 0)
    def _():
        zrow[...] = jnp.zeros_like(zrow)

        def tile_copy(t):
            return pltpu.make_async_copy(zrow, _token_rows(xs_hbm, t * tm, rt, tm), pad_sem)

        def issue_tile(t, _):
            tile_copy(t).start()
            return 0

        def drain_tile(t, _):
            tile_copy(t).wait()
            return 0

        for_each_pad_block(lambda copy: copy.start())
        lax.fori_loop(used_ref[0], n_tiles, issue_tile, 0)
        for_each_pad_block(lambda copy: copy.wait())
        lax.fori_loop(used_ref[0], n_tiles, drain_tile, 0)

    whole_tile = pltpu.make_async_copy(h_ref, _token_rows(xs_hbm, 0, rt, tb), sem)
    whole_tile.wait()
    whole_tile.wait()


def _dispatch(hp, slots, pads, tiles_used, n, tm):
    assert tm & (tm - 1) == 0
    rt = hp.shape[0] // n
    n_tiles = 2 * n // tm + N_EXPERTS
    tb = _tile(n, 1024)
    return pl.pallas_call(
        functools.partial(_dispatch_kernel, tb=tb, tm=tm, rt=rt, n_tiles=n_tiles),
        out_shape=jax.ShapeDtypeStruct((n_tiles * tm * rt, LANES), hp.dtype),
        grid_spec=pltpu.PrefetchScalarGridSpec(
            num_scalar_prefetch=3,
            grid=(n // tb,),
            in_specs=[pl.BlockSpec((tb * rt, LANES), lambda i, s, p, u: (i, 0))],
            out_specs=pl.BlockSpec(memory_space=pl.ANY),
            scratch_shapes=[pltpu.VMEM((tm * rt, LANES), hp.dtype), pltpu.SemaphoreType.DMA(()),
                            pltpu.SemaphoreType.DMA(())],
        ),
        compiler_params=_params(("arbitrary",)),
        name="moe_dispatch",
    )(slots, pads, tiles_used, hp)


def _moe_up_kernel(te_ref, fresh_ref, valid_ref, x_ref, wg_ref, wu_ref, o_ref, wg_bf, wu_bf):
    i = pl.program_id(0)
    tm = o_ref.shape[0]
    rt = x_ref.shape[0] // tm
    c = rt * LANES

    @pl.when(fresh_ref[i] == 1)
    def _():
        wg_bf[...] = wg_ref[...].astype(BF16)
        wu_bf[...] = wu_ref[...].astype(BF16)

    @pl.when(valid_ref[i] == 1)
    def _():
        xa, xb = _unpack_bf16_pairs(_load_row_tiles(x_ref, tm, rt))
        xa, xb = xa.astype(BF16), xb.astype(BF16)
        g = _dot(xa, wg_bf[:c, :]) + _dot(xb, wg_bf[c:, :])
        u = _dot(xa, wu_bf[:c, :]) + _dot(xb, wu_bf[c:, :])
        o_ref[...] = (g * (1.0 / (1.0 + jnp.exp(-g))) * u).astype(o_ref.dtype)

    @pl.when(valid_ref[i] == 0)
    def _():
        o_ref[...] = jnp.zeros_like(o_ref)


def _tile_or_first(i, valid_ref):
    return jnp.where(valid_ref[i] == 1, i, 0)


def _moe_up(xs, w_gate, w_up, layer, tile_expert, fresh, valid, tm):
    _, _, d, f = w_gate.shape
    rt = d // 2 // LANES
    p = xs.shape[0] // rt
    w_spec = pl.BlockSpec((None, None, d, f), lambda i, te, fr, va: (layer, te[i], 0, 0))
    return pl.pallas_call(
        _moe_up_kernel,
        out_shape=jax.ShapeDtypeStruct((p, f), BF16),
        grid_spec=pltpu.PrefetchScalarGridSpec(
            num_scalar_prefetch=3,
            grid=(p // tm,),
            in_specs=[pl.BlockSpec((tm * rt, LANES), lambda i, te, fr, va: (_tile_or_first(i, va), 0)),
                      w_spec, w_spec],
            out_specs=pl.BlockSpec((tm, f), lambda i, te, fr, va: (i, 0)),
            scratch_shapes=[pltpu.VMEM((d, f), BF16), pltpu.VMEM((d, f), BF16)],
        ),
        compiler_params=_params(("arbitrary",)),
        name="moe_up",
    )(tile_expert, fresh, valid, xs, w_gate, w_up)


def _moe_down_kernel(te_ref, fresh_ref, valid_ref, a_ref, wd_ref, o_ref, wd_bf):
    i = pl.program_id(0)

    @pl.when(fresh_ref[i] == 1)
    def _():
        wd_bf[...] = wd_ref[...].astype(BF16)

    @pl.when(valid_ref[i] == 1)
    def _():
        _store_row_tiles(o_ref, _pack_bf16_pairs(_dot(a_ref[...], wd_bf[...])))

    @pl.when(valid_ref[i] == 0)
    def _():
        o_ref[...] = jnp.zeros_like(o_ref)


def _moe_down(act, w_down, layer, tile_expert, fresh, valid, tm):
    p, f = act.shape
    d = w_down.shape[3]
    rt = d // 2 // LANES
    return pl.pallas_call(
        _moe_down_kernel,
        out_shape=jax.ShapeDtypeStruct((p * rt, LANES), jnp.uint32),
        grid_spec=pltpu.PrefetchScalarGridSpec(
            num_scalar_prefetch=3,
            grid=(p // tm,),
            in_specs=[pl.BlockSpec((tm, f), lambda i, te, fr, va: (_tile_or_first(i, va), 0)),
                      pl.BlockSpec((None, None, f, d), lambda i, te, fr, va: (layer, te[i], 0, 0))],
            out_specs=pl.BlockSpec((tm * rt, LANES), lambda i, te, fr, va: (i, 0)),
            scratch_shapes=[pltpu.VMEM((f, d), BF16)],
        ),
        compiler_params=_params(("arbitrary",)),
        name="moe_down",
    )(tile_expert, fresh, valid, act, w_down)


def _combine_ln_kernel(slot_ref, h_ref, info_ref, g_ref, b_ref, ys_hbm, o_ref, ob_ref, buf, sems, *, tm, rt, alpha):
    i = pl.program_id(0)

    def row_copy(s, par, r):
        return pltpu.make_async_copy(_token_rows(ys_hbm, s, rt), _token_rows(buf.at[par], r, rt), sems.at[par])

    def gather_tile(tile, par):
        def issue(r, _):
            t = tile * tm + r
            row_copy(slot_ref[2 * t], par, r).start()
            row_copy(slot_ref[2 * t + 1], par, tm + r).start()
            return 0
        lax.fori_loop(0, tm, issue, 0, unroll=DMA_ISSUE_UNROLL)

    @pl.when(i == 0)
    def _():
        gather_tile(0, 0)

    @pl.when(i + 1 < pl.num_programs(0))
    def _():
        gather_tile(i + 1, (i + 1) % 2)

    par = i % 2
    pltpu.make_async_copy(_token_rows(ys_hbm, 0, rt, 2 * tm), buf.at[par], sems.at[par]).wait()

    info = info_ref[...]
    w1, w2 = info[:, 4:5], info[:, 5:6]
    a1, b1 = _unpack_bf16_pairs(_load_row_tiles(buf.at[par], tm, rt))
    a2, b2 = _unpack_bf16_pairs(_load_row_tiles(buf.at[par], tm, rt, first=tm))
    ffn = jnp.concatenate([w1 * a1 + w2 * a2, w1 * b1 + w2 * b2], axis=1)
    y = _layer_norm_rows(alpha * h_ref[...] + ffn, g_ref[...], b_ref[...])
    o_ref[...] = y
    ob_ref[...] = y.astype(BF16)


def _combine_ln(h, info, ys, slots, g, b, alpha):
    n, d = h.shape
    tm = _tile(n, 256)
    rt = d // 2 // LANES
    return pl.pallas_call(
        functools.partial(_combine_ln_kernel, tm=tm, rt=rt, alpha=alpha),
        out_shape=(jax.ShapeDtypeStruct((n, d), F32), jax.ShapeDtypeStruct((n, d), BF16)),
        grid_spec=pltpu.PrefetchScalarGridSpec(
            num_scalar_prefetch=1,
            grid=(n // tm,),
            in_specs=[pl.BlockSpec((tm, d), lambda i, s: (i, 0)),
                      pl.BlockSpec((tm, LANES), lambda i, s: (i, 0)),
                      pl.BlockSpec((1, d), lambda i, s: (0, 0)),
                      pl.BlockSpec((1, d), lambda i, s: (0, 0)),
                      pl.BlockSpec(memory_space=pl.ANY)],
            out_specs=(pl.BlockSpec((tm, d), lambda i, s: (i, 0)),
                       pl.BlockSpec((tm, d), lambda i, s: (i, 0))),
            scratch_shapes=[pltpu.VMEM((2, 2 * tm * rt, LANES), jnp.uint32), pltpu.SemaphoreType.DMA((2,))],
        ),
        compiler_params=_params(("arbitrary",)),
        name="moe_combine_ln",
    )(slots, h, info, g.reshape(1, d), b.reshape(1, d), ys)


def _route_weights(w_group, w_inner):
    d = w_group.shape[0]
    return jnp.concatenate(
        [w_group, w_inner.transpose(1, 0, 2).reshape(d, N_EXPERTS),
         jnp.zeros((d, LANES - N_GROUPS - N_EXPERTS), F32)], axis=1)


def _moe_ln(h, hp, info, cnt, layer, w_gate, w_up, w_down, ln_g, ln_b, alpha):
    n, d = h.shape
    tm = _tile(2 * n // N_EXPERTS, 512)
    n_rows = 2 * n + N_EXPERTS * tm
    n_tiles = n_rows // tm

    expert = info[:, 0:2].astype(jnp.int32)
    rank = info[:, 2:4].astype(jnp.int32)
    counts = cnt[0, ROUTE_LANE0:ROUTE_LANE0 + N_EXPERTS].astype(jnp.int32)
    padded = (counts + tm - 1) // tm * tm
    ends = jnp.cumsum(padded)
    starts = ends - padded
    slots = (starts[expert] + rank).reshape(-1)
    pads = jnp.concatenate([starts + counts, padded - counts])
    tile_ids = jnp.arange(n_tiles, dtype=jnp.int32)
    tile_expert = jnp.minimum(jnp.sum((tile_ids[:, None] >= (ends // tm)[None, :]).astype(jnp.int32), axis=1),
                              N_EXPERTS - 1)
    valid = (tile_ids < ends[-1] // tm).astype(jnp.int32)
    fresh = jnp.concatenate([jnp.ones((1,), jnp.int32),
                             (tile_expert[1:] != tile_expert[:-1]).astype(jnp.int32)])

    xs = _dispatch(hp, slots, pads, (ends[-1:] // tm).astype(jnp.int32), n, tm)
    act = _moe_up(xs, w_gate, w_up, layer, tile_expert, fresh, valid, tm)
    ys = _moe_down(act, w_down, layer, tile_expert, fresh, valid, tm)
    return _combine_ln(h, info, ys, slots, ln_g, ln_b, alpha)


def _rope_tables(seq):
    half = DIFF_QK_DIM // 2
    lane = jnp.arange(LANES)
    inv_freq = ROPE_THETA ** (-(lane % half).astype(F32) / half)
    ang = jnp.arange(seq, dtype=F32)[:, None] * inv_freq[None, :]
    sign = jnp.where((lane % DIFF_QK_DIM) < half, -1.0, 1.0)
    return jnp.cos(ang), jnp.sin(ang) * sign


def kernel(x, sb_w_qkv, sb_w_o, shared_w_kv, diff_w_q, diff_w_o, diff_lambda, diff_subln_g, ln_mix_g, ln_mix_b,
           ln_ffn_g, ln_ffn_b, moe_w_group, moe_w_inner, moe_w_gate, moe_w_up, moe_w_down):
    batch, seq, d = x.shape
    n = batch * seq
    depth = ln_mix_g.shape[0]
    n_sb = sb_w_qkv.shape[0]
    alpha = (2.0 * depth) ** 0.25
    rope_tabs = _rope_tables(seq)

    h = x.reshape(n, d)
    hb = h.astype(BF16)
    kv = None
    for layer in range(depth):
        if layer < n_sb:
            qkv = _proj(hb, sb_w_qkv, layer, scale=HEAD_DIM ** -0.5 * LOG2_E, scale_cols=d)
            o = _sb_attention(qkv, batch, seq)
            w_o = sb_w_o[layer]
        else:
            i = layer - n_sb
            if kv is None:
                kv = _proj(hb, shared_w_kv, rope_cols=d, rope_tabs=rope_tabs, seq=seq)
            qd = _proj(hb, diff_w_q, i, scale=DIFF_QK_DIM ** -0.5 * LOG2_E, scale_cols=d,
                       rope_cols=d, rope_tabs=rope_tabs, seq=seq)
            lam_init = 0.8 - 0.6 * math.exp(-0.3 * layer)
            o = _diff_attention(qd, kv, diff_lambda[i], diff_subln_g[i], lam_init, batch, seq)
            w_o = diff_w_o[i]
        h, hp = _proj_ln(o, w_o.astype(BF16), h, ln_mix_g[layer], ln_mix_b[layer], alpha)
        info, cnt = _route(h, _route_weights(moe_w_group[layer], moe_w_inner[layer]))
        h, hb = _moe_ln(h, hp, info, cnt, layer, moe_w_gate, moe_w_up, moe_w_down,
                        ln_ffn_g[layer], ln_ffn_b[layer], alpha)
    return h.reshape(batch, seq, d)
```

```python
import functools
import math

import jax
import jax.numpy as jnp
from jax import lax
from jax.experimental import pallas as pl
from jax.experimental.pallas import tpu as pltpu

HEAD_DIM = 128
DIFF_QK_DIM = HEAD_DIM // 2
N_GROUPS = 4
EXPERTS_PER_GROUP = 4
N_EXPERTS = N_GROUPS * EXPERTS_PER_GROUP
ROPE_THETA = 10000.0
LN_EPS = 1e-5
SUBLN_EPS = 1e-5
LANES = 128
ROUTE_LANE0 = N_GROUPS
ROUTE_INFO_WIDTH = 8
NEG_BIG = -1e30
SB_SKIP_BELOW = -150.0
LOG2_E = 1.0 / math.log(2.0)
DMA_ISSUE_UNROLL = 8
VMEM_LIMIT = 56 * 1024 * 1024

F32 = jnp.float32
BF16 = jnp.bfloat16


def _tile(n, pref):
    t = min(n, pref)
    assert n % t == 0, (n, pref)
    return t


def _params(sem):
    return pltpu.CompilerParams(dimension_semantics=sem, vmem_limit_bytes=VMEM_LIMIT)


def _dot(a, b):
    return jnp.dot(a, b, preferred_element_type=F32)


def _dot_nt(a, b):
    return lax.dot_general(a, b, (((1,), (1,)), ((), ())), preferred_element_type=F32)


def _rope_tile(acc, cos, sin):
    lane = lax.broadcasted_iota(jnp.int32, cos.shape, 1)
    first_half = (lane % DIFF_QK_DIM) < (DIFF_QK_DIM // 2)
    outs = []
    for g in range(acc.shape[1] // LANES):
        xg = acc[:, g * LANES:(g + 1) * LANES]
        up = pltpu.roll(xg, LANES - DIFF_QK_DIM // 2, axis=1)
        dn = pltpu.roll(xg, DIFF_QK_DIM // 2, axis=1)
        partner = jnp.where(first_half, up, dn)
        outs.append(xg * cos + partner * sin)
    return jnp.concatenate(outs, axis=1)


def _proj_kernel(x_ref, w_ref, *rest, scale, scale_tiles, rope_tiles):
    if rope_tiles:
        cos_ref, sin_ref, o_ref, w_bf = rest
    else:
        o_ref, w_bf = rest
    j = pl.program_id(0)

    @pl.when(pl.program_id(1) == 0)
    def _():
        w_bf[...] = w_ref[...].astype(BF16)

    acc = _dot(x_ref[...], w_bf[...])
    if scale_tiles:
        acc = acc * jnp.where(j < scale_tiles, jnp.float32(scale), jnp.float32(1.0))
    if rope_tiles:
        @pl.when(j < rope_tiles)
        def _():
            o_ref[...] = _rope_tile(acc, cos_ref[...], sin_ref[...]).astype(o_ref.dtype)

        @pl.when(j >= rope_tiles)
        def _():
            o_ref[...] = acc.astype(o_ref.dtype)
    else:
        o_ref[...] = acc.astype(o_ref.dtype)


def _proj(x, w, layer=None, *, scale=1.0, scale_cols=0, rope_cols=0, rope_tabs=None, seq=None):
    n, k = x.shape
    m = w.shape[-1]
    tm = math.gcd(_tile(n, 1024), seq) if rope_cols else _tile(n, 1024)
    tn = math.gcd(1024, m, scale_cols, rope_cols)
    assert tn % LANES == 0
    if layer is None:
        w_spec = pl.BlockSpec((k, tn), lambda j, i: (0, j))
    else:
        w_spec = pl.BlockSpec((None, k, tn), lambda j, i: (layer, 0, j))
    in_specs = [pl.BlockSpec((tm, k), lambda j, i: (i, 0)), w_spec]
    args = [x, w]
    if rope_cols:
        assert seq % tm == 0
        spb = seq // tm
        in_specs += [pl.BlockSpec((tm, LANES), lambda j, i: (i % spb, 0))] * 2
        args += list(rope_tabs)
    kern = functools.partial(_proj_kernel, scale=scale, scale_tiles=scale_cols // tn,
                             rope_tiles=rope_cols // tn)
    return pl.pallas_call(
        kern,
        out_shape=jax.ShapeDtypeStruct((n, m), BF16),
        grid=(m // tn, n // tm),
        in_specs=in_specs,
        out_specs=pl.BlockSpec((tm, tn), lambda j, i: (i, j)),
        scratch_shapes=[pltpu.VMEM((k, tn), BF16)],
        compiler_params=_params(("arbitrary", "arbitrary")),
        name="proj",
    )(*args)


def _layer_norm_rows(v, g, b):
    mu = jnp.mean(v, axis=-1, keepdims=True)
    c = v - mu
    var = jnp.mean(c * c, axis=-1, keepdims=True)
    return c * lax.rsqrt(var + LN_EPS) * g + b


def _pack_bf16_pairs(y):
    c = y.shape[1] // 2
    bits = lax.bitcast_convert_type(y.astype(BF16).astype(F32), jnp.uint32)
    return bits[:, :c] | (bits[:, c:] >> 16)


def _unpack_bf16_pairs(w):
    return (lax.bitcast_convert_type(w & jnp.uint32(0xFFFF0000), F32),
            lax.bitcast_convert_type(w << 16, F32))


def _store_row_tiles(ref, packed):
    m, c = packed.shape
    r = c // LANES
    for s in range(r):
        ref[pl.ds(s, m, stride=r), :] = packed[:, s * LANES:(s + 1) * LANES]


def _load_row_tiles(ref, m, r, first=0):
    return jnp.concatenate([ref[pl.ds(first * r + s, m, stride=r), :] for s in range(r)], axis=1)


def _proj_ln_kernel(x_ref, w_ref, res_ref, g_ref, b_ref, o_ref, op_ref, *, alpha, parts):
    tm = x_ref.shape[0]
    rt = op_ref.shape[0] // tm
    ts = tm // parts
    accs = [_dot(x_ref[p * ts:(p + 1) * ts, :], w_ref[...]) for p in range(parts)]
    for p, acc in enumerate(accs):
        rows = slice(p * ts, (p + 1) * ts)
        y = _layer_norm_rows(alpha * res_ref[rows, :] + acc, g_ref[...], b_ref[...])
        o_ref[rows, :] = y
        _store_row_tiles(op_ref.at[p * ts * rt:(p + 1) * ts * rt, :], _pack_bf16_pairs(y))


def _proj_ln(x, w, res, g, b, alpha):
    n, k = x.shape
    d = w.shape[1]
    tm = _tile(n, 512)
    parts = 4 if tm % 512 == 0 else 1
    r = d // 2 // LANES
    return pl.pallas_call(
        functools.partial(_proj_ln_kernel, alpha=alpha, parts=parts),
        out_shape=(jax.ShapeDtypeStruct((n, d), F32), jax.ShapeDtypeStruct((n * r, LANES), jnp.uint32)),
        grid=(n // tm,),
        in_specs=[pl.BlockSpec((tm, k), lambda i: (i, 0)),
                  pl.BlockSpec((k, d), lambda i: (0, 0)),
                  pl.BlockSpec((tm, d), lambda i: (i, 0)),
                  pl.BlockSpec((1, d), lambda i: (0, 0)),
                  pl.BlockSpec((1, d), lambda i: (0, 0))],
        out_specs=(pl.BlockSpec((tm, d), lambda i: (i, 0)),
                   pl.BlockSpec((tm * r, LANES), lambda i: (i, 0))),
        compiler_params=_params(("arbitrary",)),
        name="proj_ln",
    )(x, w, res, g.reshape(1, d), b.reshape(1, d))


def _sb_attn_kernel(q_ref, k_ref, v_ref, o_ref, *, t, hb):
    qi = pl.program_id(2)
    row = lax.broadcasted_iota(jnp.int32, (2 * t, t), 0) % t
    col = lax.broadcasted_iota(jnp.int32, (2 * t, t), 1)
    later = (row > col).astype(BF16)
    past = (lax.broadcasted_iota(jnp.int32, (t, t), 1)
            < lax.broadcasted_iota(jnp.int32, (t, t), 0))

    def step(kc, accs, tails, masked):
        start = pl.multiple_of(kc * t, t)
        cols = [slice(h * HEAD_DIM, (h + 1) * HEAD_DIM) for h in range(hb)]
        zs = [_dot_nt(q_ref[:, c], k_ref[pl.ds(start, t), c]) for c in cols]
        log_betas, log_keeps, splits = [], [], []
        for z in zs:
            nl = -jnp.log2(1.0 + jnp.exp2(-jnp.abs(z)))
            log_betas.append(nl + jnp.minimum(z, 0.0))
            log_keep = nl - jnp.maximum(z, 0.0)
            if masked:
                log_keep = jnp.where(past, log_keep, 0.0)
            hi = log_keep.astype(BF16)
            lo = (log_keep - hi.astype(F32)).astype(BF16)
            log_keeps.append(log_keep)
            splits.append(jnp.concatenate([hi, lo], axis=1))
        tail_ins = [_dot(s, later) for s in splits]
        new_tails = tuple(s + jnp.sum(lk, axis=1, keepdims=True) for s, lk in zip(tails, log_keeps))
        more = any_weight_left(new_tails)
        ws = []
        for h in range(hb):
            w = jnp.exp2(log_betas[h] + tail_ins[h] + tails[h])
            if masked:
                w = jnp.where(past, w, 0.0)
            ws.append(w.astype(BF16))
        outs = [_dot(ws[h], v_ref[pl.ds(start, t), cols[h]]) for h in range(hb)]
        accs = tuple(a + o for a, o in zip(accs, outs))
        return accs, new_tails, more

    def any_weight_left(tails):
        worst = functools.reduce(jnp.maximum, tails)
        return (jnp.max(worst) >= SB_SKIP_BELOW).astype(jnp.int32)

    accs = tuple(jnp.zeros((t, HEAD_DIM), F32) for _ in range(hb))
    tails = tuple(jnp.zeros((t, 1), F32) for _ in range(hb))
    accs, tails, more = step(qi, accs, tails, True)

    def cond(c):
        return jnp.logical_and(c[0] >= 0, c[1] > 0)

    def body(c):
        kc, _, accs, tails = c
        accs, tails, more = step(kc, accs, tails, False)
        return kc - 1, more, accs, tails

    _, _, accs, _ = lax.while_loop(cond, body, (qi - 1, more, accs, tails))
    for h in range(hb):
        o_ref[:, h * HEAD_DIM:(h + 1) * HEAD_DIM] = accs[h].astype(o_ref.dtype)


def _sb_attention(qkv, batch, seq):
    n, three_d = qkv.shape
    d = three_d // 3
    heads = d // HEAD_DIM
    hb = math.gcd(heads, 16)
    t = _tile(seq, 128)
    nq = seq // t
    nh = heads // hb
    w = hb * HEAD_DIM
    return pl.pallas_call(
        functools.partial(_sb_attn_kernel, t=t, hb=hb),
        out_shape=jax.ShapeDtypeStruct((n, d), BF16),
        grid=(batch, nh, nq),
        in_specs=[pl.BlockSpec((t, w), lambda b, h, i: (b * nq + i, h)),
                  pl.BlockSpec((seq, w), lambda b, h, i: (b, nh + h)),
                  pl.BlockSpec((seq, w), lambda b, h, i: (b, 2 * nh + h))],
        out_specs=pl.BlockSpec((t, w), lambda b, h, i: (b * nq + i, h)),
        compiler_params=_params(("arbitrary", "arbitrary", "arbitrary")),
        name="sb_attn",
    )(qkv, qkv, qkv)


def _diff_attn_kernel(q_ref, k_ref, v_ref, lam_ref, g_ref, o_ref, *, t, hb, lam_init):
    qi = pl.program_id(2)
    lane = lax.broadcasted_iota(jnp.int32, (t, HEAD_DIM), 1)
    q_pos = lax.broadcasted_iota(jnp.int32, (t, t), 0)
    k_pos = lax.broadcasted_iota(jnp.int32, (t, t), 1)
    causal = k_pos <= q_pos
    cols = [slice(h * HEAD_DIM, (h + 1) * HEAD_DIM) for h in range(hb)]
    qs = []
    for c in cols:
        q = q_ref[:, c]
        zero = jnp.zeros_like(q)
        qs.append((jnp.where(lane < DIFF_QK_DIM, q, zero),
                   jnp.where(lane >= DIFF_QK_DIM, q, zero)))

    def step(kc, carry, masked):
        start = pl.multiple_of(kc * t, t)
        scores = [[_dot_nt(qx, k_ref[pl.ds(start, t), c]) for qx in qh] for qh, c in zip(qs, cols)]
        out = []
        for h in range(hb):
            v = v_ref[pl.ds(start, t), cols[h]]
            streams = []
            for s, (m, l, acc) in zip(scores[h], carry[h]):
                if masked:
                    s = jnp.where(causal, s, NEG_BIG)
                m_new = jnp.maximum(m, jnp.max(s, axis=1, keepdims=True))
                p = jnp.exp2(s - m_new)
                corr = jnp.exp2(m - m_new)
                l = corr * l + jnp.sum(p, axis=1, keepdims=True)
                acc = corr * acc + _dot(p.astype(BF16), v)
                streams.append((m_new, l, acc))
            out.append(tuple(streams))
        return tuple(out)

    init = (jnp.full((t, 1), NEG_BIG, F32), jnp.zeros((t, 1), F32), jnp.zeros((t, HEAD_DIM), F32))
    carry = step(qi, ((init, init),) * hb, True)
    carry = lax.fori_loop(0, qi, lambda c, s: step(c, s, False), carry)

    lp = lam_ref[...]
    lam = (jnp.exp(jnp.sum(lp[0:1] * lp[1:2], axis=1, keepdims=True))
           - jnp.exp(jnp.sum(lp[2:3] * lp[3:4], axis=1, keepdims=True)) + lam_init)
    for h in range(hb):
        (_, l1, a1), (_, l2, a2) = carry[h]
        o = a1 / l1 - lam * (a2 / l2)
        o = o * lax.rsqrt(jnp.mean(o * o, axis=1, keepdims=True) + SUBLN_EPS)
        o_ref[:, cols[h]] = (o * g_ref[...] * (1.0 - lam_init)).astype(o_ref.dtype)


def _diff_attention(qd, kv, lam_params, subln_g, lam_init, batch, seq):
    n, d = qd.shape
    heads = d // HEAD_DIM
    t = _tile(seq, 512)
    nq = seq // t
    hb = math.gcd(heads, 4)
    nh = heads // hb
    w = hb * HEAD_DIM
    return pl.pallas_call(
        functools.partial(_diff_attn_kernel, t=t, hb=hb, lam_init=lam_init),
        out_shape=jax.ShapeDtypeStruct((n, d), BF16),
        grid=(batch, nh, nq),
        in_specs=[pl.BlockSpec((t, w), lambda b, h, i: (b * nq + i, h)),
                  pl.BlockSpec((seq, w), lambda b, h, i: (b, h)),
                  pl.BlockSpec((seq, w), lambda b, h, i: (b, nh + h)),
                  pl.BlockSpec(lam_params.shape, lambda b, h, i: (0, 0)),
                  pl.BlockSpec((1, HEAD_DIM), lambda b, h, i: (0, 0))],
        out_specs=pl.BlockSpec((t, w), lambda b, h, i: (b * nq + i, h)),
        compiler_params=_params(("arbitrary", "arbitrary", "arbitrary")),
        name="diff_attn",
    )(qd, kv, kv, lam_params, subln_g.reshape(1, HEAD_DIM))


def _split_bf16(x):
    hi = x.astype(BF16)
    return hi, (x - hi.astype(F32)).astype(BF16)


def _route_rows(h, w_ref, info_ref, cnt_ref):
    tm = h.shape[0]
    i = pl.program_id(0)

    @pl.when(i == 0)
    def _():
        cnt_ref[...] = jnp.zeros_like(cnt_ref)

    h_hi, h_lo = _split_bf16(h)
    w_hi, w_lo = _split_bf16(w_ref[...])
    both_w = _dot(h_hi, jnp.concatenate([w_hi, w_lo], axis=1))
    logits = both_w[:, :LANES] + (both_w[:, LANES:] + _dot(h_lo, w_hi))

    lane = lax.broadcasted_iota(jnp.int32, (tm, LANES), 1).astype(F32)

    def first_max(mask, vals):
        top = jnp.max(jnp.where(mask, vals, NEG_BIG), axis=1, keepdims=True)
        idx = jnp.min(jnp.where(mask & (vals == top), lane, float(LANES)), axis=1, keepdims=True)
        return top, idx

    g_mask = lane < N_GROUPS
    g_top, g_sel = first_max(g_mask, logits)
    g_den = jnp.sum(jnp.where(g_mask, jnp.exp(logits - g_top), 0.0), axis=1, keepdims=True)
    g_w = 1.0 / g_den

    lo_lane = ROUTE_LANE0 + EXPERTS_PER_GROUP * g_sel
    in_group = (lane >= lo_lane) & (lane < lo_lane + EXPERTS_PER_GROUP)
    v1, i1 = first_max(in_group, logits)
    v2, i2 = first_max(in_group & (lane != i1), logits)
    ex = jnp.exp(v2 - v1)
    w1 = g_w / (1.0 + ex)
    w2 = w1 * ex

    m1 = (lane == i1).astype(F32)
    m2 = (lane == i2).astype(F32)
    both = m1 + m2
    r = lax.broadcasted_iota(jnp.int32, (tm, tm), 0)
    c = lax.broadcasted_iota(jnp.int32, (tm, tm), 1)
    earlier = (c < r).astype(BF16)
    before = _dot(earlier, both.astype(BF16)) + cnt_ref[0:1, :]
    rank1 = jnp.sum(m1 * before, axis=1, keepdims=True)
    rank2 = jnp.sum(m2 * before, axis=1, keepdims=True)
    cnt_ref[...] = cnt_ref[...] + jnp.sum(both, axis=0, keepdims=True)

    info = jnp.where(lane == 0, i1 - ROUTE_LANE0, 0.0)
    info = jnp.where(lane == 1, i2 - ROUTE_LANE0, info)
    info = jnp.where(lane == 2, rank1, info)
    info = jnp.where(lane == 3, rank2, info)
    info = jnp.where(lane == 4, w1, info)
    info = jnp.where(lane == 5, w2, info)
    info_ref[...] = info[:, :ROUTE_INFO_WIDTH]


def _route_kernel(h_ref, w_ref, info_ref, cnt_ref):
    _route_rows(h_ref[...], w_ref, info_ref, cnt_ref)


def _route(h, w_route):
    n, d = h.shape
    tm = _tile(n, 512)
    return pl.pallas_call(
        _route_kernel,
        out_shape=(jax.ShapeDtypeStruct((n, ROUTE_INFO_WIDTH), F32), jax.ShapeDtypeStruct((8, LANES), F32)),
        grid=(n // tm,),
        in_specs=[pl.BlockSpec((tm, d), lambda i: (i, 0)),
                  pl.BlockSpec((d, LANES), lambda i: (0, 0))],
        out_specs=(pl.BlockSpec((tm, ROUTE_INFO_WIDTH), lambda i: (i, 0)),
                   pl.BlockSpec((8, LANES), lambda i: (0, 0))),
        compiler_params=_params(("arbitrary",)),
        name="route",
    )(h, w_route)


def _token_rows(ref, token, rt, count=1):
    return ref.at[pl.ds(pl.multiple_of(token * rt, rt), rt * count), :]


def _dispatch_kernel(slot_ref, pad_ref, used_ref, h_ref, xs_hbm, zrow, sem, pad_sem, *, tb, tm, rt, n_tiles):
    i = pl.program_id(0)
    base = i * tb

    def row_copy(r, s):
        return pltpu.make_async_copy(_token_rows(h_ref, r, rt), _token_rows(xs_hbm, s, rt), sem)

    def issue(r, _):
        t = base + r
        row_copy(r, slot_ref[2 * t]).start()
        row_copy(r, slot_ref[2 * t + 1]).start()
        return 0

    lax.fori_loop(0, tb, issue, 0, unroll=DMA_ISSUE_UNROLL)

    def for_each_pad_block(act):
        for e in range(N_EXPERTS):
            first, length = pad_ref[e], pad_ref[N_EXPERTS + e]
            for bit in reversed(range(tm.bit_length() - 1)):
                size = 1 << bit
                slot = first + ((length >> (bit + 1)) << (bit + 1))

                @pl.when(((length >> bit) & 1) == 1)
                def _():
                    act(pltpu.make_async_copy(_token_rows(zrow, 0, rt, size), _token_rows(xs_hbm, slot, rt, size),
                                              pad_sem))

    @pl.when(i == 0)
    def _():
        zrow[...] = jnp.zeros_like(zrow)

        def tile_copy(t):
            return pltpu.make_async_copy(zrow, _token_rows(xs_hbm, t * tm, rt, tm), pad_sem)

        def issue_tile(t, _):
            tile_copy(t).start()
            return 0

        def drain_tile(t, _):
            tile_copy(t).wait()
            return 0

        for_each_pad_block(lambda copy: copy.start())
        lax.fori_loop(used_ref[0], n_tiles, issue_tile, 0)
        for_each_pad_block(lambda copy: copy.wait())
        lax.fori_loop(used_ref[0], n_tiles, drain_tile, 0)

    whole_tile = pltpu.make_async_copy(h_ref, _token_rows(xs_hbm, 0, rt, tb), sem)
    whole_tile.wait()
    whole_tile.wait()


def _dispatch(hp, slots, pads, tiles_used, n, tm):
    assert tm & (tm - 1) == 0
    rt = hp.shape[0] // n
    n_tiles = 2 * n // tm + N_EXPERTS
    tb = _tile(n, 1024)
    return pl.pallas_call(
        functools.partial(_dispatch_kernel, tb=tb, tm=tm, rt=rt, n_tiles=n_tiles),
        out_shape=jax.ShapeDtypeStruct((n_tiles * tm * rt, LANES), hp.dtype),
        grid_spec=pltpu.PrefetchScalarGridSpec(
            num_scalar_prefetch=3,
            grid=(n // tb,),
            in_specs=[pl.BlockSpec((tb * rt, LANES), lambda i, s, p, u: (i, 0))],
            out_specs=pl.BlockSpec(memory_space=pl.ANY),
            scratch_shapes=[pltpu.VMEM((tm * rt, LANES), hp.dtype), pltpu.SemaphoreType.DMA(()),
                            pltpu.SemaphoreType.DMA(())],
        ),
        compiler_params=_params(("arbitrary",)),
        name="moe_dispatch",
    )(slots, pads, tiles_used, hp)


def _moe_up_kernel(te_ref, fresh_ref, valid_ref, x_ref, wg_ref, wu_ref, o_ref, wg_bf, wu_bf):
    i = pl.program_id(0)
    tm = o_ref.shape[0]
    rt = x_ref.shape[0] // tm
    c = rt * LANES

    @pl.when(fresh_ref[i] == 1)
    def _():
        wg_bf[...] = wg_ref[...].astype(BF16)
        wu_bf[...] = wu_ref[...].astype(BF16)

    @pl.when(valid_ref[i] == 1)
    def _():
        xa, xb = _unpack_bf16_pairs(_load_row_tiles(x_ref, tm, rt))
        xa, xb = xa.astype(BF16), xb.astype(BF16)
        g = _dot(xa, wg_bf[:c, :]) + _dot(xb, wg_bf[c:, :])
        u = _dot(xa, wu_bf[:c, :]) + _dot(xb, wu_bf[c:, :])
        o_ref[...] = (g * (1.0 / (1.0 + jnp.exp(-g))) * u).astype(o_ref.dtype)

    @pl.when(valid_ref[i] == 0)
    def _():
        o_ref[...] = jnp.zeros_like(o_ref)


def _tile_or_first(i, valid_ref):
    return jnp.where(valid_ref[i] == 1, i, 0)


def _moe_up(xs, w_gate, w_up, layer, tile_expert, fresh, valid, tm):
    _, _, d, f = w_gate.shape
    rt = d // 2 // LANES
    p = xs.shape[0] // rt
    w_spec = pl.BlockSpec((None, None, d, f), lambda i, te, fr, va: (layer, te[i], 0, 0))
    return pl.pallas_call(
        _moe_up_kernel,
        out_shape=jax.ShapeDtypeStruct((p, f), BF16),
        grid_spec=pltpu.PrefetchScalarGridSpec(
            num_scalar_prefetch=3,
            grid=(p // tm,),
            in_specs=[pl.BlockSpec((tm * rt, LANES), lambda i, te, fr, va: (_tile_or_first(i, va), 0)),
                      w_spec, w_spec],
            out_specs=pl.BlockSpec((tm, f), lambda i, te, fr, va: (i, 0)),
            scratch_shapes=[pltpu.VMEM((d, f), BF16), pltpu.VMEM((d, f), BF16)],
        ),
        compiler_params=_params(("arbitrary",)),
        name="moe_up",
    )(tile_expert, fresh, valid, xs, w_gate, w_up)


def _moe_down_kernel(te_ref, fresh_ref, valid_ref, a_ref, wd_ref, o_ref, wd_bf):
    i = pl.program_id(0)

    @pl.when(fresh_ref[i] == 1)
    def _():
        wd_bf[...] = wd_ref[...].astype(BF16)

    @pl.when(valid_ref[i] == 1)
    def _():
        _store_row_tiles(o_ref, _pack_bf16_pairs(_dot(a_ref[...], wd_bf[...])))

    @pl.when(valid_ref[i] == 0)
    def _():
        o_ref[...] = jnp.zeros_like(o_ref)


def _moe_down(act, w_down, layer, tile_expert, fresh, valid, tm):
    p, f = act.shape
    d = w_down.shape[3]
    rt = d // 2 // LANES
    return pl.pallas_call(
        _moe_down_kernel,
        out_shape=jax.ShapeDtypeStruct((p * rt, LANES), jnp.uint32),
        grid_spec=pltpu.PrefetchScalarGridSpec(
            num_scalar_prefetch=3,
            grid=(p // tm,),
            in_specs=[pl.BlockSpec((tm, f), lambda i, te, fr, va: (_tile_or_first(i, va), 0)),
                      pl.BlockSpec((None, None, f, d), lambda i, te, fr, va: (layer, te[i], 0, 0))],
            out_specs=pl.BlockSpec((tm * rt, LANES), lambda i, te, fr, va: (i, 0)),
            scratch_shapes=[pltpu.VMEM((f, d), BF16)],
        ),
        compiler_params=_params(("arbitrary",)),
        name="moe_down",
    )(tile_expert, fresh, valid, act, w_down)


def _combine_ln_kernel(slot_ref, h_ref, info_ref, g_ref, b_ref, ys_hbm, o_ref, ob_ref, buf, sems, *, tm, rt, alpha):
    i = pl.program_id(0)

    def row_copy(s, par, r):
        return pltpu.make_async_copy(_token_rows(ys_hbm, s, rt), _token_rows(buf.at[par], r, rt), sems.at[par])

    def gather_tile(tile, par):
        def issue(r, _):
            t = tile * tm + r
            row_copy(slot_ref[2 * t], par, r).start()
            row_copy(slot_ref[2 * t + 1], par, tm + r).start()
            return 0
        lax.fori_loop(0, tm, issue, 0, unroll=DMA_ISSUE_UNROLL)

    @pl.when(i == 0)
    def _():
        gather_tile(0, 0)

    @pl.when(i + 1 < pl.num_programs(0))
    def _():
        gather_tile(i + 1, (i + 1) % 2)

    par = i % 2
    pltpu.make_async_copy(_token_rows(ys_hbm, 0, rt, 2 * tm), buf.at[par], sems.at[par]).wait()

    info = info_ref[...]
    w1, w2 = info[:, 4:5], info[:, 5:6]
    a1, b1 = _unpack_bf16_pairs(_load_row_tiles(buf.at[par], tm, rt))
    a2, b2 = _unpack_bf16_pairs(_load_row_tiles(buf.at[par], tm, rt, first=tm))
    ffn = jnp.concatenate([w1 * a1 + w2 * a2, w1 * b1 + w2 * b2], axis=1)
    y = _layer_norm_rows(alpha * h_ref[...] + ffn, g_ref[...], b_ref[...])
    o_ref[...] = y
    ob_ref[...] = y.astype(BF16)


def _combine_ln(h, info, ys, slots, g, b, alpha):
    n, d = h.shape
    tm = _tile(n, 256)
    rt = d // 2 // LANES
    return pl.pallas_call(
        functools.partial(_combine_ln_kernel, tm=tm, rt=rt, alpha=alpha),
        out_shape=(jax.ShapeDtypeStruct((n, d), F32), jax.ShapeDtypeStruct((n, d), BF16)),
        grid_spec=pltpu.PrefetchScalarGridSpec(
            num_scalar_prefetch=1,
            grid=(n // tm,),
            in_specs=[pl.BlockSpec((tm, d), lambda i, s: (i, 0)),
                      pl.BlockSpec((tm, ROUTE_INFO_WIDTH), lambda i, s: (i, 0)),
                      pl.BlockSpec((1, d), lambda i, s: (0, 0)),
                      pl.BlockSpec((1, d), lambda i, s: (0, 0)),
                      pl.BlockSpec(memory_space=pl.ANY)],
            out_specs=(pl.BlockSpec((tm, d), lambda i, s: (i, 0)),
                       pl.BlockSpec((tm, d), lambda i, s: (i, 0))),
            scratch_shapes=[pltpu.VMEM((2, 2 * tm * rt, LANES), jnp.uint32), pltpu.SemaphoreType.DMA((2,))],
        ),
        compiler_params=_params(("arbitrary",)),
        name="moe_combine_ln",
    )(slots, h, info, g.reshape(1, d), b.reshape(1, d), ys)


def _route_weights(w_group, w_inner):
    d = w_group.shape[0]
    return jnp.concatenate(
        [w_group, w_inner.transpose(1, 0, 2).reshape(d, N_EXPERTS),
         jnp.zeros((d, LANES - N_GROUPS - N_EXPERTS), F32)], axis=1)


def _moe_ln(h, hp, info, cnt, layer, w_gate, w_up, w_down, ln_g, ln_b, alpha):
    n, d = h.shape
    tm = _tile(2 * n // N_EXPERTS, 512)
    n_rows = 2 * n + N_EXPERTS * tm
    n_tiles = n_rows // tm

    expert = info[:, 0:2].astype(jnp.int32)
    rank = info[:, 2:4].astype(jnp.int32)
    counts = cnt[0, ROUTE_LANE0:ROUTE_LANE0 + N_EXPERTS].astype(jnp.int32)
    padded = (counts + tm - 1) // tm * tm
    ends = jnp.cumsum(padded)
    starts = ends - padded
    slots = (starts[expert] + rank).reshape(-1)
    pads = jnp.concatenate([starts + counts, padded - counts])
    tile_ids = jnp.arange(n_tiles, dtype=jnp.int32)
    tile_expert = jnp.minimum(jnp.sum((tile_ids[:, None] >= (ends // tm)[None, :]).astype(jnp.int32), axis=1),
                              N_EXPERTS - 1)
    valid = (tile_ids < ends[-1] // tm).astype(jnp.int32)
    fresh = jnp.concatenate([jnp.ones((1,), jnp.int32),
                             (tile_expert[1:] != tile_expert[:-1]).astype(jnp.int32)])

    xs = _dispatch(hp, slots, pads, (ends[-1:] // tm).astype(jnp.int32), n, tm)
    act = _moe_up(xs, w_gate, w_up, layer, tile_expert, fresh, valid, tm)
    ys = _moe_down(act, w_down, layer, tile_expert, fresh, valid, tm)
    return _combine_ln(h, info, ys, slots, ln_g, ln_b, alpha)


def _rope_tables(seq):
    half = DIFF_QK_DIM // 2
    lane = jnp.arange(LANES)
    inv_freq = ROPE_THETA ** (-(lane % half).astype(F32) / half)
    ang = jnp.arange(seq, dtype=F32)[:, None] * inv_freq[None, :]
    sign = jnp.where((lane % DIFF_QK_DIM) < half, -1.0, 1.0)
    return jnp.cos(ang), jnp.sin(ang) * sign


def kernel(x, sb_w_qkv, sb_w_o, shared_w_kv, diff_w_q, diff_w_o, diff_lambda, diff_subln_g, ln_mix_g, ln_mix_b,
           ln_ffn_g, ln_ffn_b, moe_w_group, moe_w_inner, moe_w_gate, moe_w_up, moe_w_down):
    batch, seq, d = x.shape
    n = batch * seq
    depth = ln_mix_g.shape[0]
    n_sb = sb_w_qkv.shape[0]
    alpha = (2.0 * depth) ** 0.25
    rope_tabs = _rope_tables(seq)

    h = x.reshape(n, d)
    hb = h.astype(BF16)
    kv = None
    for layer in range(depth):
        if layer < n_sb:
            qkv = _proj(hb, sb_w_qkv, layer, scale=HEAD_DIM ** -0.5 * LOG2_E, scale_cols=d)
            o = _sb_attention(qkv, batch, seq)
            w_o = sb_w_o[layer]
        else:
            i = layer - n_sb
            if kv is None:
                kv = _proj(hb, shared_w_kv, rope_cols=d, rope_tabs=rope_tabs, seq=seq)
            qd = _proj(hb, diff_w_q, i, scale=DIFF_QK_DIM ** -0.5 * LOG2_E, scale_cols=d,
                       rope_cols=d, rope_tabs=rope_tabs, seq=seq)
            lam_init = 0.8 - 0.6 * math.exp(-0.3 * layer)
            o = _diff_attention(qd, kv, diff_lambda[i], diff_subln_g[i], lam_init, batch, seq)
            w_o = diff_w_o[i]
        h, hp = _proj_ln(o, w_o.astype(BF16), h, ln_mix_g[layer], ln_mix_b[layer], alpha)
        info, cnt = _route(h, _route_weights(moe_w_group[layer], moe_w_inner[layer]))
        h, hb = _moe_ln(h, hp, info, cnt, layer, moe_w_gate, moe_w_up, moe_w_down,
                        ln_ffn_g[layer], ln_ffn_b[layer], alpha)
    return h.reshape(batch, seq, d)
```

```python
import functools
import math

import jax
import jax.numpy as jnp
from jax import lax
from jax.experimental import pallas as pl
from jax.experimental.pallas import tpu as pltpu

HEAD_DIM = 128
DIFF_QK_DIM = HEAD_DIM // 2
N_GROUPS = 4
EXPERTS_PER_GROUP = 4
N_EXPERTS = N_GROUPS * EXPERTS_PER_GROUP
ROPE_THETA = 10000.0
LN_EPS = 1e-5
SUBLN_EPS = 1e-5
LANES = 128
ROUTE_LANE0 = N_GROUPS
NEG_BIG = -1e30
SB_SKIP_BELOW = -150.0
LOG2_E = 1.0 / math.log(2.0)
DMA_ISSUE_UNROLL = 8
VMEM_LIMIT = 56 * 1024 * 1024

F32 = jnp.float32
BF16 = jnp.bfloat16


def _tile(n, pref):
    t = min(n, pref)
    assert n % t == 0, (n, pref)
    return t


def _params(sem):
    return pltpu.CompilerParams(dimension_semantics=sem, vmem_limit_bytes=VMEM_LIMIT)


def _dot(a, b):
    return jnp.dot(a, b, preferred_element_type=F32)


def _dot_nt(a, b):
    return lax.dot_general(a, b, (((1,), (1,)), ((), ())), preferred_element_type=F32)


def _rope_tile(acc, cos, sin):
    lane = lax.broadcasted_iota(jnp.int32, cos.shape, 1)
    first_half = (lane % DIFF_QK_DIM) < (DIFF_QK_DIM // 2)
    outs = []
    for g in range(acc.shape[1] // LANES):
        xg = acc[:, g * LANES:(g + 1) * LANES]
        up = pltpu.roll(xg, LANES - DIFF_QK_DIM // 2, axis=1)
        dn = pltpu.roll(xg, DIFF_QK_DIM // 2, axis=1)
        partner = jnp.where(first_half, up, dn)
        outs.append(xg * cos + partner * sin)
    return jnp.concatenate(outs, axis=1)


def _proj_kernel(x_ref, w_ref, *rest, scale, scale_tiles, rope_tiles):
    if rope_tiles:
        cos_ref, sin_ref, o_ref, w_bf = rest
    else:
        o_ref, w_bf = rest
    j = pl.program_id(0)

    @pl.when(pl.program_id(1) == 0)
    def _():
        w_bf[...] = w_ref[...].astype(BF16)

    acc = _dot(x_ref[...], w_bf[...])
    if scale_tiles:
        acc = acc * jnp.where(j < scale_tiles, jnp.float32(scale), jnp.float32(1.0))
    if rope_tiles:
        @pl.when(j < rope_tiles)
        def _():
            o_ref[...] = _rope_tile(acc, cos_ref[...], sin_ref[...]).astype(o_ref.dtype)

        @pl.when(j >= rope_tiles)
        def _():
            o_ref[...] = acc.astype(o_ref.dtype)
    else:
        o_ref[...] = acc.astype(o_ref.dtype)


def _proj(x, w, layer=None, *, scale=1.0, scale_cols=0, rope_cols=0, rope_tabs=None, seq=None):
    n, k = x.shape
    m = w.shape[-1]
    tm = math.gcd(_tile(n, 1024), seq) if rope_cols else _tile(n, 1024)
    tn = math.gcd(1024, m, scale_cols, rope_cols)
    assert tn % LANES == 0
    if layer is None:
        w_spec = pl.BlockSpec((k, tn), lambda j, i: (0, j))
    else:
        w_spec = pl.BlockSpec((None, k, tn), lambda j, i: (layer, 0, j))
    in_specs = [pl.BlockSpec((tm, k), lambda j, i: (i, 0)), w_spec]
    args = [x, w]
    if rope_cols:
        assert seq % tm == 0
        spb = seq // tm
        in_specs += [pl.BlockSpec((tm, LANES), lambda j, i: (i % spb, 0))] * 2
        args += list(rope_tabs)
    kern = functools.partial(_proj_kernel, scale=scale, scale_tiles=scale_cols // tn,
                             rope_tiles=rope_cols // tn)
    return pl.pallas_call(
        kern,
        out_shape=jax.ShapeDtypeStruct((n, m), BF16),
        grid=(m // tn, n // tm),
        in_specs=in_specs,
        out_specs=pl.BlockSpec((tm, tn), lambda j, i: (i, j)),
        scratch_shapes=[pltpu.VMEM((k, tn), BF16)],
        compiler_params=_params(("arbitrary", "arbitrary")),
        name="proj",
    )(*args)


def _layer_norm_rows(v, g, b):
    mu = jnp.mean(v, axis=-1, keepdims=True)
    c = v - mu
    var = jnp.mean(c * c, axis=-1, keepdims=True)
    return c * lax.rsqrt(var + LN_EPS) * g + b


def _pack_bf16_pairs(y):
    c = y.shape[1] // 2
    bits = lax.bitcast_convert_type(y.astype(BF16).astype(F32), jnp.uint32)
    return bits[:, :c] | (bits[:, c:] >> 16)


def _unpack_bf16_pairs(w):
    return (lax.bitcast_convert_type(w & jnp.uint32(0xFFFF0000), F32),
            lax.bitcast_convert_type(w << 16, F32))


def _store_row_tiles(ref, packed):
    m, c = packed.shape
    r = c // LANES
    for s in range(r):
        ref[pl.ds(s, m, stride=r), :] = packed[:, s * LANES:(s + 1) * LANES]


def _load_row_tiles(ref, m, r, first=0):
    return jnp.concatenate([ref[pl.ds(first * r + s, m, stride=r), :] for s in range(r)], axis=1)


def _proj_ln_kernel(x_ref, w_ref, res_ref, g_ref, b_ref, o_ref, op_ref, *, alpha, parts):
    tm = x_ref.shape[0]
    rt = op_ref.shape[0] // tm
    ts = tm // parts
    accs = [_dot(x_ref[p * ts:(p + 1) * ts, :], w_ref[...]) for p in range(parts)]
    for p, acc in enumerate(accs):
        rows = slice(p * ts, (p + 1) * ts)
        y = _layer_norm_rows(alpha * res_ref[rows, :] + acc, g_ref[...], b_ref[...])
        o_ref[rows, :] = y
        _store_row_tiles(op_ref.at[p * ts * rt:(p + 1) * ts * rt, :], _pack_bf16_pairs(y))


def _proj_ln(x, w, res, g, b, alpha):
    n, k = x.shape
    d = w.shape[1]
    tm = _tile(n, 512)
    parts = 2 if tm % 512 == 0 else 1
    r = d // 2 // LANES
    return pl.pallas_call(
        functools.partial(_proj_ln_kernel, alpha=alpha, parts=parts),
        out_shape=(jax.ShapeDtypeStruct((n, d), F32), jax.ShapeDtypeStruct((n * r, LANES), jnp.uint32)),
        grid=(n // tm,),
        in_specs=[pl.BlockSpec((tm, k), lambda i: (i, 0)),
                  pl.BlockSpec((k, d), lambda i: (0, 0)),
                  pl.BlockSpec((tm, d), lambda i: (i, 0)),
                  pl.BlockSpec((1, d), lambda i: (0, 0)),
                  pl.BlockSpec((1, d), lambda i: (0, 0))],
        out_specs=(pl.BlockSpec((tm, d), lambda i: (i, 0)),
                   pl.BlockSpec((tm * r, LANES), lambda i: (i, 0))),
        compiler_params=_params(("arbitrary",)),
        name="proj_ln",
    )(x, w, res, g.reshape(1, d), b.reshape(1, d))


def _sb_attn_kernel(q_ref, k_ref, v_ref, o_ref, *, t, hb):
    qi = pl.program_id(2)
    row = lax.broadcasted_iota(jnp.int32, (2 * t, t), 0) % t
    col = lax.broadcasted_iota(jnp.int32, (2 * t, t), 1)
    later = (row > col).astype(BF16)
    past = (lax.broadcasted_iota(jnp.int32, (t, t), 1)
            < lax.broadcasted_iota(jnp.int32, (t, t), 0))

    def step(kc, accs, tails, masked, chunks=1):
        starts = [pl.multiple_of((kc - j) * t, t) for j in range(chunks)]
        cols = [slice(h * HEAD_DIM, (h + 1) * HEAD_DIM) for h in range(hb)]
        zs = [[_dot_nt(q_ref[:, c], k_ref[pl.ds(st, t), c]) for c in cols] for st in starts]
        log_betas, log_keeps, splits = [], [], []
        for j, zc in enumerate(zs):
            lb, lk, sp = [], [], []
            for z in zc:
                nl = -jnp.log2(1.0 + jnp.exp2(-jnp.abs(z)))
                lb.append(nl + jnp.minimum(z, 0.0))
                log_keep = nl - jnp.maximum(z, 0.0)
                if masked and j == 0:
                    log_keep = jnp.where(past, log_keep, 0.0)
                hi = log_keep.astype(BF16)
                lo = (log_keep - hi.astype(F32)).astype(BF16)
                lk.append(log_keep)
                sp.append(jnp.concatenate([hi, lo], axis=1))
            log_betas.append(lb)
            log_keeps.append(lk)
            splits.append(sp)
        tail_ins = [[_dot(s, later) for s in sp] for sp in splits]
        chunk_tails = [tails]
        for lk in log_keeps:
            chunk_tails.append(tuple(s + jnp.sum(x, axis=1, keepdims=True) for s, x in zip(chunk_tails[-1], lk)))
        more = any_weight_left(chunk_tails[-1])
        ws = []
        for j in range(chunks):
            wc = []
            for h in range(hb):
                w = jnp.exp2(log_betas[j][h] + tail_ins[j][h] + chunk_tails[j][h])
                if masked and j == 0:
                    w = jnp.where(past, w, 0.0)
                wc.append(w.astype(BF16))
            ws.append(wc)
        accs = list(accs)
        for j, st in enumerate(starts):
            for h in range(hb):
                accs[h] = accs[h] + _dot(ws[j][h], v_ref[pl.ds(st, t), cols[h]])
        return tuple(accs), chunk_tails[-1], more

    def any_weight_left(tails):
        worst = functools.reduce(jnp.maximum, tails)
        return (jnp.max(worst) >= SB_SKIP_BELOW).astype(jnp.int32)

    def cond(c):
        return jnp.logical_and(c[0] >= 0, c[1] > 0)

    def body(c):
        kc, _, accs, tails = c
        accs, tails, more = step(kc, accs, tails, False)
        return kc - 1, more, accs, tails

    def run(first_chunks):
        accs = tuple(jnp.zeros((t, HEAD_DIM), F32) for _ in range(hb))
        tails = tuple(jnp.zeros((t, 1), F32) for _ in range(hb))
        accs, tails, more = step(qi, accs, tails, True, first_chunks)
        _, _, accs, _ = lax.while_loop(cond, body, (qi - first_chunks, more, accs, tails))
        for h in range(hb):
            o_ref[:, h * HEAD_DIM:(h + 1) * HEAD_DIM] = accs[h].astype(o_ref.dtype)

    lead = 3

    @pl.when(qi >= lead - 1)
    def _():
        run(lead)

    @pl.when(qi < lead - 1)
    def _():
        run(1)


def _sb_attention(qkv, batch, seq):
    n, three_d = qkv.shape
    d = three_d // 3
    heads = d // HEAD_DIM
    hb = math.gcd(heads, 16)
    t = _tile(seq, 128)
    nq = seq // t
    nh = heads // hb
    w = hb * HEAD_DIM
    return pl.pallas_call(
        functools.partial(_sb_attn_kernel, t=t, hb=hb),
        out_shape=jax.ShapeDtypeStruct((n, d), BF16),
        grid=(batch, nh, nq),
        in_specs=[pl.BlockSpec((t, w), lambda b, h, i: (b * nq + i, h)),
                  pl.BlockSpec((seq, w), lambda b, h, i: (b, nh + h)),
                  pl.BlockSpec((seq, w), lambda b, h, i: (b, 2 * nh + h))],
        out_specs=pl.BlockSpec((t, w), lambda b, h, i: (b * nq + i, h)),
        compiler_params=_params(("arbitrary", "arbitrary", "arbitrary")),
        name="sb_attn",
    )(qkv, qkv, qkv)


def _diff_attn_kernel(q_ref, k_ref, v_ref, lam_ref, g_ref, o_ref, *, t, hb, lam_init):
    qi = pl.program_id(2)
    lane = lax.broadcasted_iota(jnp.int32, (t, HEAD_DIM), 1)
    q_pos = lax.broadcasted_iota(jnp.int32, (t, t), 0)
    k_pos = lax.broadcasted_iota(jnp.int32, (t, t), 1)
    causal = k_pos <= q_pos
    cols = [slice(h * HEAD_DIM, (h + 1) * HEAD_DIM) for h in range(hb)]
    qs = []
    for c in cols:
        q = q_ref[:, c]
        zero = jnp.zeros_like(q)
        qs.append((jnp.where(lane < DIFF_QK_DIM, q, zero),
                   jnp.where(lane >= DIFF_QK_DIM, q, zero)))

    def step(kc, carry, masked):
        start = pl.multiple_of(kc * t, t)
        scores = [[_dot_nt(qx, k_ref[pl.ds(start, t), c]) for qx in qh] for qh, c in zip(qs, cols)]
        out = []
        for h in range(hb):
            v = v_ref[pl.ds(start, t), cols[h]]
            streams = []
            for s, (m, l, acc) in zip(scores[h], carry[h]):
                if masked:
                    s = jnp.where(causal, s, NEG_BIG)
                m_new = jnp.maximum(m, jnp.max(s, axis=1, keepdims=True))
                p = jnp.exp2(s - m_new)
                corr = jnp.exp2(m - m_new)
                l = corr * l + jnp.sum(p, axis=1, keepdims=True)
                acc = corr * acc + _dot(p.astype(BF16), v)
                streams.append((m_new, l, acc))
            out.append(tuple(streams))
        return tuple(out)

    init = (jnp.full((t, 1), NEG_BIG, F32), jnp.zeros((t, 1), F32), jnp.zeros((t, HEAD_DIM), F32))
    carry = step(qi, ((init, init),) * hb, True)
    carry = lax.fori_loop(0, qi, lambda c, s: step(c, s, False), carry)

    lp = lam_ref[...]
    lam = (jnp.exp(jnp.sum(lp[0:1] * lp[1:2], axis=1, keepdims=True))
           - jnp.exp(jnp.sum(lp[2:3] * lp[3:4], axis=1, keepdims=True)) + lam_init)
    for h in range(hb):
        (_, l1, a1), (_, l2, a2) = carry[h]
        o = a1 / l1 - lam * (a2 / l2)
        o = o * lax.rsqrt(jnp.mean(o * o, axis=1, keepdims=True) + SUBLN_EPS)
        o_ref[:, cols[h]] = (o * g_ref[...] * (1.0 - lam_init)).astype(o_ref.dtype)


def _diff_attention(qd, kv, lam_params, subln_g, lam_init, batch, seq):
    n, d = qd.shape
    heads = d // HEAD_DIM
    t = _tile(seq, 512)
    nq = seq // t
    hb = math.gcd(heads, 4)
    nh = heads // hb
    w = hb * HEAD_DIM
    return pl.pallas_call(
        functools.partial(_diff_attn_kernel, t=t, hb=hb, lam_init=lam_init),
        out_shape=jax.ShapeDtypeStruct((n, d), BF16),
        grid=(batch, nh, nq),
        in_specs=[pl.BlockSpec((t, w), lambda b, h, i: (b * nq + i, h)),
                  pl.BlockSpec((seq, w), lambda b, h, i: (b, h)),
                  pl.BlockSpec((seq, w), lambda b, h, i: (b, nh + h)),
                  pl.BlockSpec(lam_params.shape, lambda b, h, i: (0, 0)),
                  pl.BlockSpec((1, HEAD_DIM), lambda b, h, i: (0, 0))],
        out_specs=pl.BlockSpec((t, w), lambda b, h, i: (b * nq + i, h)),
        compiler_params=_params(("arbitrary", "arbitrary", "arbitrary")),
        name="diff_attn",
    )(qd, kv, kv, lam_params, subln_g.reshape(1, HEAD_DIM))


def _split_bf16(x):
    hi = x.astype(BF16)
    return hi, (x - hi.astype(F32)).astype(BF16)


def _route_rows(h, w_ref, info_ref, cnt_ref):
    tm = h.shape[0]
    i = pl.program_id(0)

    @pl.when(i == 0)
    def _():
        cnt_ref[...] = jnp.zeros_like(cnt_ref)

    h_hi, h_lo = _split_bf16(h)
    w_hi, w_lo = _split_bf16(w_ref[...])
    both_w = _dot(h_hi, jnp.concatenate([w_hi, w_lo], axis=1))
    logits = both_w[:, :LANES] + (both_w[:, LANES:] + _dot(h_lo, w_hi))

    lane = lax.broadcasted_iota(jnp.int32, (tm, LANES), 1).astype(F32)

    def first_max(mask, vals):
        top = jnp.max(jnp.where(mask, vals, NEG_BIG), axis=1, keepdims=True)
        idx = jnp.min(jnp.where(mask & (vals == top), lane, float(LANES)), axis=1, keepdims=True)
        return top, idx

    g_mask = lane < N_GROUPS
    g_top, g_sel = first_max(g_mask, logits)
    g_den = jnp.sum(jnp.where(g_mask, jnp.exp(logits - g_top), 0.0), axis=1, keepdims=True)
    g_w = 1.0 / g_den

    lo_lane = ROUTE_LANE0 + EXPERTS_PER_GROUP * g_sel
    in_group = (lane >= lo_lane) & (lane < lo_lane + EXPERTS_PER_GROUP)
    v1, i1 = first_max(in_group, logits)
    v2, i2 = first_max(in_group & (lane != i1), logits)
    ex = jnp.exp(v2 - v1)
    w1 = g_w / (1.0 + ex)
    w2 = w1 * ex

    m1 = (lane == i1).astype(F32)
    m2 = (lane == i2).astype(F32)
    both = m1 + m2
    r = lax.broadcasted_iota(jnp.int32, (tm, tm), 0)
    c = lax.broadcasted_iota(jnp.int32, (tm, tm), 1)
    earlier = (c < r).astype(BF16)
    before = _dot(earlier, both.astype(BF16)) + cnt_ref[0:1, :]
    rank1 = jnp.sum(m1 * before, axis=1, keepdims=True)
    rank2 = jnp.sum(m2 * before, axis=1, keepdims=True)
    cnt_ref[...] = cnt_ref[...] + jnp.sum(both, axis=0, keepdims=True)

    info = jnp.where(lane == 0, i1 - ROUTE_LANE0, 0.0)
    info = jnp.where(lane == 1, i2 - ROUTE_LANE0, info)
    info = jnp.where(lane == 2, rank1, info)
    info = jnp.where(lane == 3, rank2, info)
    info = jnp.where(lane == 4, w1, info)
    info = jnp.where(lane == 5, w2, info)
    info_ref[...] = info


def _route_kernel(h_ref, w_ref, info_ref, cnt_ref):
    _route_rows(h_ref[...], w_ref, info_ref, cnt_ref)


def _route(h, w_route):
    n, d = h.shape
    tm = _tile(n, 512)
    return pl.pallas_call(
        _route_kernel,
        out_shape=(jax.ShapeDtypeStruct((n, LANES), F32), jax.ShapeDtypeStruct((8, LANES), F32)),
        grid=(n // tm,),
        in_specs=[pl.BlockSpec((tm, d), lambda i: (i, 0)),
                  pl.BlockSpec((d, LANES), lambda i: (0, 0))],
        out_specs=(pl.BlockSpec((tm, LANES), lambda i: (i, 0)),
                   pl.BlockSpec((8, LANES), lambda i: (0, 0))),
        compiler_params=_params(("arbitrary",)),
        name="route",
    )(h, w_route)


def _token_rows(ref, token, rt, count=1):
    return ref.at[pl.ds(pl.multiple_of(token * rt, rt), rt * count), :]


def _dispatch_kernel(slot_ref, pad_ref, used_ref, h_ref, xs_hbm, zrow, sem, pad_sem, *, tb, tm, rt, n_tiles):
    i = pl.program_id(0)
    base = i * tb

    def row_copy(r, s):
        return pltpu.make_async_copy(_token_rows(h_ref, r, rt), _token_rows(xs_hbm, s, rt), sem)

    def issue(r, _):
        t = base + r
        row_copy(r, slot_ref[2 * t]).start()
        row_copy(r, slot_ref[2 * t + 1]).start()
        return 0

    lax.fori_loop(0, tb, issue, 0, unroll=DMA_ISSUE_UNROLL)

    def for_each_pad_block(act):
        for e in range(N_EXPERTS):
            first, length = pad_ref[e], pad_ref[N_EXPERTS + e]
            for bit in reversed(range(tm.bit_length() - 1)):
                size = 1 << bit
                slot = first + ((length >> (bit + 1)) << (bit + 1))

                @pl.when(((length >> bit) & 1) == 1)
                def _():
                    act(pltpu.make_async_copy(_token_rows(zrow, 0, rt, size), _token_rows(xs_hbm, slot, rt, size),
                                              pad_sem))

    @pl.when(i == 0)
    def _():
        zrow[...] = jnp.zeros_like(zrow)

        def tile_copy(t):
            return pltpu.make_async_copy(zrow, _token_rows(xs_hbm, t * tm, rt, tm), pad_sem)

        def issue_tile(t, _):
            tile_copy(t).start()
            return 0

        def drain_tile(t, _):
            tile_copy(t).wait()
            return 0

        for_each_pad_block(lambda copy: copy.start())
        lax.fori_loop(used_ref[0], n_tiles, issue_tile, 0)
        for_each_pad_block(lambda copy: copy.wait())
        lax.fori_loop(used_ref[0], n_tiles, drain_tile, 0)

    whole_tile = pltpu.make_async_copy(h_ref, _token_rows(xs_hbm, 0, rt, tb), sem)
    whole_tile.wait()
    whole_tile.wait()


def _dispatch(hp, slots, pads, tiles_used, n, tm):
    assert tm & (tm - 1) == 0
    rt = hp.shape[0] // n
    n_tiles = 2 * n // tm + N_EXPERTS
    tb = _tile(n, 1024)
    return pl.pallas_call(
        functools.partial(_dispatch_kernel, tb=tb, tm=tm, rt=rt, n_tiles=n_tiles),
        out_shape=jax.ShapeDtypeStruct((n_tiles * tm * rt, LANES), hp.dtype),
        grid_spec=pltpu.PrefetchScalarGridSpec(
            num_scalar_prefetch=3,
            grid=(n // tb,),
            in_specs=[pl.BlockSpec((tb * rt, LANES), lambda i, s, p, u: (i, 0))],
            out_specs=pl.BlockSpec(memory_space=pl.ANY),
            scratch_shapes=[pltpu.VMEM((tm * rt, LANES), hp.dtype), pltpu.SemaphoreType.DMA(()),
                            pltpu.SemaphoreType.DMA(())],
        ),
        compiler_params=_params(("arbitrary",)),
        name="moe_dispatch",
    )(slots, pads, tiles_used, hp)


def _moe_up_kernel(te_ref, fresh_ref, valid_ref, x_ref, wg_ref, wu_ref, o_ref, wg_bf, wu_bf):
    i = pl.program_id(0)
    tm = o_ref.shape[0]
    rt = x_ref.shape[0] // tm
    c = rt * LANES

    @pl.when(fresh_ref[i] == 1)
    def _():
        wg_bf[...] = wg_ref[...].astype(BF16)
        wu_bf[...] = wu_ref[...].astype(BF16)

    @pl.when(valid_ref[i] == 1)
    def _():
        xa, xb = _unpack_bf16_pairs(_load_row_tiles(x_ref, tm, rt))
        xa, xb = xa.astype(BF16), xb.astype(BF16)
        g = _dot(xa, wg_bf[:c, :]) + _dot(xb, wg_bf[c:, :])
        u = _dot(xa, wu_bf[:c, :]) + _dot(xb, wu_bf[c:, :])
        o_ref[...] = (g * (1.0 / (1.0 + jnp.exp(-g))) * u).astype(o_ref.dtype)

    @pl.when(valid_ref[i] == 0)
    def _():
        o_ref[...] = jnp.zeros_like(o_ref)


def _tile_or_first(i, valid_ref):
    return jnp.where(valid_ref[i] == 1, i, 0)


def _moe_up(xs, w_gate, w_up, layer, tile_expert, fresh, valid, tm):
    _, _, d, f = w_gate.shape
    rt = d // 2 // LANES
    p = xs.shape[0] // rt
    w_spec = pl.BlockSpec((None, None, d, f), lambda i, te, fr, va: (layer, te[i], 0, 0))
    return pl.pallas_call(
        _moe_up_kernel,
        out_shape=jax.ShapeDtypeStruct((p, f), BF16),
        grid_spec=pltpu.PrefetchScalarGridSpec(
            num_scalar_prefetch=3,
            grid=(p // tm,),
            in_specs=[pl.BlockSpec((tm * rt, LANES), lambda i, te, fr, va: (_tile_or_first(i, va), 0)),
                      w_spec, w_spec],
            out_specs=pl.BlockSpec((tm, f), lambda i, te, fr, va: (i, 0)),
            scratch_shapes=[pltpu.VMEM((d, f), BF16), pltpu.VMEM((d, f), BF16)],
        ),
        compiler_params=_params(("arbitrary",)),
        name="moe_up",
    )(tile_expert, fresh, valid, xs, w_gate, w_up)


def _moe_down_kernel(te_ref, fresh_ref, valid_ref, a_ref, wd_ref, o_ref, wd_bf):
    i = pl.program_id(0)

    @pl.when(fresh_ref[i] == 1)
    def _():
        wd_bf[...] = wd_ref[...].astype(BF16)

    @pl.when(valid_ref[i] == 1)
    def _():
        _store_row_tiles(o_ref, _pack_bf16_pairs(_dot(a_ref[...], wd_bf[...])))

    @pl.when(valid_ref[i] == 0)
    def _():
        o_ref[...] = jnp.zeros_like(o_ref)


def _moe_down(act, w_down, layer, tile_expert, fresh, valid, tm):
    p, f = act.shape
    d = w_down.shape[3]
    rt = d // 2 // LANES
    return pl.pallas_call(
        _moe_down_kernel,
        out_shape=jax.ShapeDtypeStruct((p * rt, LANES), jnp.uint32),
        grid_spec=pltpu.PrefetchScalarGridSpec(
            num_scalar_prefetch=3,
            grid=(p // tm,),
            in_specs=[pl.BlockSpec((tm, f), lambda i, te, fr, va: (_tile_or_first(i, va), 0)),
                      pl.BlockSpec((None, None, f, d), lambda i, te, fr, va: (layer, te[i], 0, 0))],
            out_specs=pl.BlockSpec((tm * rt, LANES), lambda i, te, fr, va: (i, 0)),
            scratch_shapes=[pltpu.VMEM((f, d), BF16)],
        ),
        compiler_params=_params(("arbitrary",)),
        name="moe_down",
    )(tile_expert, fresh, valid, act, w_down)


def _combine_ln_kernel(slot_ref, h_ref, info_ref, g_ref, b_ref, ys_hbm, o_ref, ob_ref, buf, sems, *, tm, rt, alpha):
    i = pl.program_id(0)

    def row_copy(s, par, r):
        return pltpu.make_async_copy(_token_rows(ys_hbm, s, rt), _token_rows(buf.at[par], r, rt), sems.at[par])

    def gather_tile(tile, par):
        def issue(r, _):
            t = tile * tm + r
            row_copy(slot_ref[2 * t], par, r).start()
            row_copy(slot_ref[2 * t + 1], par, tm + r).start()
            return 0
        lax.fori_loop(0, tm, issue, 0, unroll=DMA_ISSUE_UNROLL)

    @pl.when(i == 0)
    def _():
        gather_tile(0, 0)

    @pl.when(i + 1 < pl.num_programs(0))
    def _():
        gather_tile(i + 1, (i + 1) % 2)

    par = i % 2
    pltpu.make_async_copy(_token_rows(ys_hbm, 0, rt, 2 * tm), buf.at[par], sems.at[par]).wait()

    info = info_ref[...]
    w1, w2 = info[:, 4:5], info[:, 5:6]
    a1, b1 = _unpack_bf16_pairs(_load_row_tiles(buf.at[par], tm, rt))
    a2, b2 = _unpack_bf16_pairs(_load_row_tiles(buf.at[par], tm, rt, first=tm))
    ffn = jnp.concatenate([w1 * a1 + w2 * a2, w1 * b1 + w2 * b2], axis=1)
    y = _layer_norm_rows(alpha * h_ref[...] + ffn, g_ref[...], b_ref[...])
    o_ref[...] = y
    ob_ref[...] = y.astype(BF16)


def _combine_ln(h, info, ys, slots, g, b, alpha):
    n, d = h.shape
    tm = _tile(n, 256)
    rt = d // 2 // LANES
    return pl.pallas_call(
        functools.partial(_combine_ln_kernel, tm=tm, rt=rt, alpha=alpha),
        out_shape=(jax.ShapeDtypeStruct((n, d), F32), jax.ShapeDtypeStruct((n, d), BF16)),
        grid_spec=pltpu.PrefetchScalarGridSpec(
            num_scalar_prefetch=1,
            grid=(n // tm,),
            in_specs=[pl.BlockSpec((tm, d), lambda i, s: (i, 0)),
                      pl.BlockSpec((tm, LANES), lambda i, s: (i, 0)),
                      pl.BlockSpec((1, d), lambda i, s: (0, 0)),
                      pl.BlockSpec((1, d), lambda i, s: (0, 0)),
                      pl.BlockSpec(memory_space=pl.ANY)],
            out_specs=(pl.BlockSpec((tm, d), lambda i, s: (i, 0)),
                       pl.BlockSpec((tm, d), lambda i, s: (i, 0))),
            scratch_shapes=[pltpu.VMEM((2, 2 * tm * rt, LANES), jnp.uint32), pltpu.SemaphoreType.DMA((2,))],
        ),
        compiler_params=_params(("arbitrary",)),
        name="moe_combine_ln",
    )(slots, h, info, g.reshape(1, d), b.reshape(1, d), ys)


def _route_weights(w_group, w_inner):
    d = w_group.shape[0]
    return jnp.concatenate(
        [w_group, w_inner.transpose(1, 0, 2).reshape(d, N_EXPERTS),
         jnp.zeros((d, LANES - N_GROUPS - N_EXPERTS), F32)], axis=1)


def _moe_ln(h, hp, info, cnt, layer, w_gate, w_up, w_down, ln_g, ln_b, alpha):
    n, d = h.shape
    tm = _tile(2 * n // N_EXPERTS, 512)
    n_rows = 2 * n + N_EXPERTS * tm
    n_tiles = n_rows // tm

    expert = info[:, 0:2].astype(jnp.int32)
    rank = info[:, 2:4].astype(jnp.int32)
    counts = cnt[0, ROUTE_LANE0:ROUTE_LANE0 + N_EXPERTS].astype(jnp.int32)
    padded = (counts + tm - 1) // tm * tm
    ends = jnp.cumsum(padded)
    starts = ends - padded
    slots = (starts[expert] + rank).reshape(-1)
    pads = jnp.concatenate([starts + counts, padded - counts])
    tile_ids = jnp.arange(n_tiles, dtype=jnp.int32)
    tile_expert = jnp.minimum(jnp.sum((tile_ids[:, None] >= (ends // tm)[None, :]).astype(jnp.int32), axis=1),
                              N_EXPERTS - 1)
    valid = (tile_ids < ends[-1] // tm).astype(jnp.int32)
    fresh = jnp.concatenate([jnp.ones((1,), jnp.int32),
                             (tile_expert[1:] != tile_expert[:-1]).astype(jnp.int32)])

    xs = _dispatch(hp, slots, pads, (ends[-1:] // tm).astype(jnp.int32), n, tm)
    act = _moe_up(xs, w_gate, w_up, layer, tile_expert, fresh, valid, tm)
    ys = _moe_down(act, w_down, layer, tile_expert, fresh, valid, tm)
    return _combine_ln(h, info, ys, slots, ln_g, ln_b, alpha)


def _rope_tables(seq):
    half = DIFF_QK_DIM // 2
    lane = jnp.arange(LANES)
    inv_freq = ROPE_THETA ** (-(lane % half).astype(F32) / half)
    ang = jnp.arange(seq, dtype=F32)[:, None] * inv_freq[None, :]
    sign = jnp.where((lane % DIFF_QK_DIM) < half, -1.0, 1.0)
    return jnp.cos(ang), jnp.sin(ang) * sign


def kernel(x, sb_w_qkv, sb_w_o, shared_w_kv, diff_w_q, diff_w_o, diff_lambda, diff_subln_g, ln_mix_g, ln_mix_b,
           ln_ffn_g, ln_ffn_b, moe_w_group, moe_w_inner, moe_w_gate, moe_w_up, moe_w_down):
    batch, seq, d = x.shape
    n = batch * seq
    depth = ln_mix_g.shape[0]
    n_sb = sb_w_qkv.shape[0]
    alpha = (2.0 * depth) ** 0.25
    rope_tabs = _rope_tables(seq)

    h = x.reshape(n, d)
    hb = h.astype(BF16)
    kv = None
    for layer in range(depth):
        if layer < n_sb:
            qkv = _proj(hb, sb_w_qkv, layer, scale=HEAD_DIM ** -0.5 * LOG2_E, scale_cols=d)
            o = _sb_attention(qkv, batch, seq)
            w_o = sb_w_o[layer]
        else:
            i = layer - n_sb
            if kv is None:
                kv = _proj(hb, shared_w_kv, rope_cols=d, rope_tabs=rope_tabs, seq=seq)
            qd = _proj(hb, diff_w_q, i, scale=DIFF_QK_DIM ** -0.5 * LOG2_E, scale_cols=d,
                       rope_cols=d, rope_tabs=rope_tabs, seq=seq)
            lam_init = 0.8 - 0.6 * math.exp(-0.3 * layer)
            o = _diff_attention(qd, kv, diff_lambda[i], diff_subln_g[i], lam_init, batch, seq)
            w_o = diff_w_o[i]
        h, hp = _proj_ln(o, w_o.astype(BF16), h, ln_mix_g[layer], ln_mix_b[layer], alpha)
        info, cnt = _route(h, _route_weights(moe_w_group[layer], moe_w_inner[layer]))
        h, hb = _moe_ln(h, hp, info, cnt, layer, moe_w_gate, moe_w_up, moe_w_down,
                        ln_ffn_g[layer], ln_ffn_b[layer], alpha)
    return h.reshape(batch, seq, d)
```

```python
import functools
import math

import jax
import jax.numpy as jnp
from jax import lax
from jax.experimental import pallas as pl
from jax.experimental.pallas import tpu as pltpu

HEAD_DIM = 128
DIFF_QK_DIM = HEAD_DIM // 2
N_GROUPS = 4
EXPERTS_PER_GROUP = 4
N_EXPERTS = N_GROUPS * EXPERTS_PER_GROUP
ROPE_THETA = 10000.0
LN_EPS = 1e-5
SUBLN_EPS = 1e-5
LANES = 128
ROUTE_LANE0 = N_GROUPS
NEG_BIG = -1e30
SB_SKIP_BELOW = -150.0
LOG2_E = 1.0 / math.log(2.0)
DMA_ISSUE_UNROLL = 8
VMEM_LIMIT = 56 * 1024 * 1024

F32 = jnp.float32
BF16 = jnp.bfloat16


def _tile(n, pref):
    t = min(n, pref)
    assert n % t == 0, (n, pref)
    return t


def _params(sem):
    return pltpu.CompilerParams(dimension_semantics=sem, vmem_limit_bytes=VMEM_LIMIT)


def _dot(a, b):
    return jnp.dot(a, b, preferred_element_type=F32)


def _dot_nt(a, b):
    return lax.dot_general(a, b, (((1,), (1,)), ((), ())), preferred_element_type=F32)


def _rope_tile(acc, cos, sin):
    lane = lax.broadcasted_iota(jnp.int32, cos.shape, 1)
    first_half = (lane % DIFF_QK_DIM) < (DIFF_QK_DIM // 2)
    outs = []
    for g in range(acc.shape[1] // LANES):
        xg = acc[:, g * LANES:(g + 1) * LANES]
        up = pltpu.roll(xg, LANES - DIFF_QK_DIM // 2, axis=1)
        dn = pltpu.roll(xg, DIFF_QK_DIM // 2, axis=1)
        partner = jnp.where(first_half, up, dn)
        outs.append(xg * cos + partner * sin)
    return jnp.concatenate(outs, axis=1)


def _proj_kernel(x_ref, w_ref, *rest, scale, scale_tiles, rope_tiles):
    if rope_tiles:
        cos_ref, sin_ref, o_ref, w_bf = rest
    else:
        o_ref, w_bf = rest
    j = pl.program_id(0)

    @pl.when(pl.program_id(1) == 0)
    def _():
        w_bf[...] = w_ref[...].astype(BF16)

    acc = _dot(x_ref[...], w_bf[...])
    if scale_tiles:
        acc = acc * jnp.where(j < scale_tiles, jnp.float32(scale), jnp.float32(1.0))
    if rope_tiles:
        @pl.when(j < rope_tiles)
        def _():
            o_ref[...] = _rope_tile(acc, cos_ref[...], sin_ref[...]).astype(o_ref.dtype)

        @pl.when(j >= rope_tiles)
        def _():
            o_ref[...] = acc.astype(o_ref.dtype)
    else:
        o_ref[...] = acc.astype(o_ref.dtype)


def _proj(x, w, layer=None, *, scale=1.0, scale_cols=0, rope_cols=0, rope_tabs=None, seq=None):
    n, k = x.shape
    m = w.shape[-1]
    tm = math.gcd(_tile(n, 1024), seq) if rope_cols else _tile(n, 1024)
    tn = math.gcd(1024, m, scale_cols, rope_cols)
    assert tn % LANES == 0
    if layer is None:
        w_spec = pl.BlockSpec((k, tn), lambda j, i: (0, j))
    else:
        w_spec = pl.BlockSpec((None, k, tn), lambda j, i: (layer, 0, j))
    in_specs = [pl.BlockSpec((tm, k), lambda j, i: (i, 0)), w_spec]
    args = [x, w]
    if rope_cols:
        assert seq % tm == 0
        spb = seq // tm
        in_specs += [pl.BlockSpec((tm, LANES), lambda j, i: (i % spb, 0))] * 2
        args += list(rope_tabs)
    kern = functools.partial(_proj_kernel, scale=scale, scale_tiles=scale_cols // tn,
                             rope_tiles=rope_cols // tn)
    return pl.pallas_call(
        kern,
        out_shape=jax.ShapeDtypeStruct((n, m), BF16),
        grid=(m // tn, n // tm),
        in_specs=in_specs,
        out_specs=pl.BlockSpec((tm, tn), lambda j, i: (i, j)),
        scratch_shapes=[pltpu.VMEM((k, tn), BF16)],
        compiler_params=_params(("arbitrary", "arbitrary")),
        name="proj",
    )(*args)


def _layer_norm_rows(v, g, b):
    mu = jnp.mean(v, axis=-1, keepdims=True)
    c = v - mu
    var = jnp.mean(c * c, axis=-1, keepdims=True)
    return c * lax.rsqrt(var + LN_EPS) * g + b


def _pack_bf16_pairs(y):
    return y


def _unpack_bf16_pairs(w):
    c = w.shape[1] // 2
    return w[:, :c], w[:, c:]


def _store_row_tiles(ref, packed):
    m, c = packed.shape
    r = c // LANES
    for s in range(r):
        ref[pl.ds(s, m, stride=r), :] = packed[:, s * LANES:(s + 1) * LANES]


def _load_row_tiles(ref, m, r, first=0):
    return jnp.concatenate([ref[pl.ds(first * r + s, m, stride=r), :] for s in range(r)], axis=1)


def _proj_ln_kernel(x_ref, w_ref, res_ref, g_ref, b_ref, o_ref, op_ref, *, alpha, parts):
    tm = x_ref.shape[0]
    rt = op_ref.shape[0] // tm
    ts = tm // parts
    accs = [_dot(x_ref[p * ts:(p + 1) * ts, :], w_ref[...]) for p in range(parts)]
    for p, acc in enumerate(accs):
        rows = slice(p * ts, (p + 1) * ts)
        y = _layer_norm_rows(alpha * res_ref[rows, :] + acc, g_ref[...], b_ref[...])
        o_ref[rows, :] = y
        _store_row_tiles(op_ref.at[p * ts * rt:(p + 1) * ts * rt, :], _pack_bf16_pairs(y))


def _proj_ln(x, w, res, g, b, alpha):
    n, k = x.shape
    d = w.shape[1]
    tm = _tile(n, 512)
    parts = 2 if tm % 512 == 0 else 1
    r = d // LANES
    return pl.pallas_call(
        functools.partial(_proj_ln_kernel, alpha=alpha, parts=parts),
        out_shape=(jax.ShapeDtypeStruct((n, d), F32), jax.ShapeDtypeStruct((n * r, LANES), F32)),
        grid=(n // tm,),
        in_specs=[pl.BlockSpec((tm, k), lambda i: (i, 0)),
                  pl.BlockSpec((k, d), lambda i: (0, 0)),
                  pl.BlockSpec((tm, d), lambda i: (i, 0)),
                  pl.BlockSpec((1, d), lambda i: (0, 0)),
                  pl.BlockSpec((1, d), lambda i: (0, 0))],
        out_specs=(pl.BlockSpec((tm, d), lambda i: (i, 0)),
                   pl.BlockSpec((tm * r, LANES), lambda i: (i, 0))),
        compiler_params=_params(("arbitrary",)),
        name="proj_ln",
    )(x, w, res, g.reshape(1, d), b.reshape(1, d))


def _sb_attn_kernel(q_ref, k_ref, v_ref, o_ref, *, t, hb):
    qi = pl.program_id(2)
    row = lax.broadcasted_iota(jnp.int32, (2 * t, t), 0) % t
    col = lax.broadcasted_iota(jnp.int32, (2 * t, t), 1)
    later = (row > col).astype(BF16)
    past = (lax.broadcasted_iota(jnp.int32, (t, t), 1)
            < lax.broadcasted_iota(jnp.int32, (t, t), 0))

    def step(kc, accs, tails, masked, chunks=1):
        starts = [pl.multiple_of((kc - j) * t, t) for j in range(chunks)]
        cols = [slice(h * HEAD_DIM, (h + 1) * HEAD_DIM) for h in range(hb)]
        zs = [[_dot_nt(q_ref[:, c], k_ref[pl.ds(st, t), c]) for c in cols] for st in starts]
        log_betas, log_keeps, splits = [], [], []
        for j, zc in enumerate(zs):
            lb, lk, sp = [], [], []
            for z in zc:
                nl = -jnp.log2(1.0 + jnp.exp2(-jnp.abs(z)))
                lb.append(nl + jnp.minimum(z, 0.0))
                log_keep = nl - jnp.maximum(z, 0.0)
                if masked and j == 0:
                    log_keep = jnp.where(past, log_keep, 0.0)
                hi = log_keep.astype(BF16)
                lo = (log_keep - hi.astype(F32)).astype(BF16)
                lk.append(log_keep)
                sp.append(jnp.concatenate([hi, lo], axis=1))
            log_betas.append(lb)
            log_keeps.append(lk)
            splits.append(sp)
        tail_ins = [[_dot(s, later) for s in sp] for sp in splits]
        chunk_tails = [tails]
        for lk in log_keeps:
            chunk_tails.append(tuple(s + jnp.sum(x, axis=1, keepdims=True) for s, x in zip(chunk_tails[-1], lk)))
        more = any_weight_left(chunk_tails[-1])
        ws = []
        for j in range(chunks):
            wc = []
            for h in range(hb):
                w = jnp.exp2(log_betas[j][h] + tail_ins[j][h] + chunk_tails[j][h])
                if masked and j == 0:
                    w = jnp.where(past, w, 0.0)
                wc.append(w.astype(BF16))
            ws.append(wc)
        accs = list(accs)
        for j, st in enumerate(starts):
            for h in range(hb):
                accs[h] = accs[h] + _dot(ws[j][h], v_ref[pl.ds(st, t), cols[h]])
        return tuple(accs), chunk_tails[-1], more

    def any_weight_left(tails):
        worst = functools.reduce(jnp.maximum, tails)
        return (jnp.max(worst) >= SB_SKIP_BELOW).astype(jnp.int32)

    def cond(c):
        return jnp.logical_and(c[0] >= 0, c[1] > 0)

    def body(c):
        kc, _, accs, tails = c
        accs, tails, more = step(kc, accs, tails, False)
        return kc - 1, more, accs, tails

    def run(first_chunks):
        accs = tuple(jnp.zeros((t, HEAD_DIM), F32) for _ in range(hb))
        tails = tuple(jnp.zeros((t, 1), F32) for _ in range(hb))
        accs, tails, more = step(qi, accs, tails, True, first_chunks)
        _, _, accs, _ = lax.while_loop(cond, body, (qi - first_chunks, more, accs, tails))
        for h in range(hb):
            o_ref[:, h * HEAD_DIM:(h + 1) * HEAD_DIM] = accs[h].astype(o_ref.dtype)

    lead = 3

    @pl.when(qi >= lead - 1)
    def _():
        run(lead)

    @pl.when(qi < lead - 1)
    def _():
        run(1)


def _sb_attention(qkv, batch, seq):
    n, three_d = qkv.shape
    d = three_d // 3
    heads = d // HEAD_DIM
    hb = math.gcd(heads, 16)
    t = _tile(seq, 128)
    nq = seq // t
    nh = heads // hb
    w = hb * HEAD_DIM
    return pl.pallas_call(
        functools.partial(_sb_attn_kernel, t=t, hb=hb),
        out_shape=jax.ShapeDtypeStruct((n, d), BF16),
        grid=(batch, nh, nq),
        in_specs=[pl.BlockSpec((t, w), lambda b, h, i: (b * nq + i, h)),
                  pl.BlockSpec((seq, w), lambda b, h, i: (b, nh + h)),
                  pl.BlockSpec((seq, w), lambda b, h, i: (b, 2 * nh + h))],
        out_specs=pl.BlockSpec((t, w), lambda b, h, i: (b * nq + i, h)),
        compiler_params=_params(("arbitrary", "arbitrary", "arbitrary")),
        name="sb_attn",
    )(qkv, qkv, qkv)


def _diff_attn_kernel(q_ref, k_ref, v_ref, lam_ref, g_ref, o_ref, *, t, hb, lam_init):
    qi = pl.program_id(2)
    lane = lax.broadcasted_iota(jnp.int32, (t, HEAD_DIM), 1)
    q_pos = lax.broadcasted_iota(jnp.int32, (t, t), 0)
    k_pos = lax.broadcasted_iota(jnp.int32, (t, t), 1)
    causal = k_pos <= q_pos
    cols = [slice(h * HEAD_DIM, (h + 1) * HEAD_DIM) for h in range(hb)]
    qs = []
    for c in cols:
        q = q_ref[:, c]
        zero = jnp.zeros_like(q)
        qs.append((jnp.where(lane < DIFF_QK_DIM, q, zero),
                   jnp.where(lane >= DIFF_QK_DIM, q, zero)))

    def step(kc, carry, masked):
        start = pl.multiple_of(kc * t, t)
        scores = [[_dot_nt(qx, k_ref[pl.ds(start, t), c]) for qx in qh] for qh, c in zip(qs, cols)]
        out = []
        for h in range(hb):
            v = v_ref[pl.ds(start, t), cols[h]]
            streams = []
            for s, (m, l, acc) in zip(scores[h], carry[h]):
                if masked:
                    s = jnp.where(causal, s, NEG_BIG)
                m_new = jnp.maximum(m, jnp.max(s, axis=1, keepdims=True))
                p = jnp.exp2(s - m_new)
                corr = jnp.exp2(m - m_new)
                l = corr * l + jnp.sum(p, axis=1, keepdims=True)
                acc = corr * acc + _dot(p.astype(BF16), v)
                streams.append((m_new, l, acc))
            out.append(tuple(streams))
        return tuple(out)

    init = (jnp.full((t, 1), NEG_BIG, F32), jnp.zeros((t, 1), F32), jnp.zeros((t, HEAD_DIM), F32))
    carry = step(qi, ((init, init),) * hb, True)
    carry = lax.fori_loop(0, qi, lambda c, s: step(c, s, False), carry)

    lp = lam_ref[...]
    lam = (jnp.exp(jnp.sum(lp[0:1] * lp[1:2], axis=1, keepdims=True))
           - jnp.exp(jnp.sum(lp[2:3] * lp[3:4], axis=1, keepdims=True)) + lam_init)
    for h in range(hb):
        (_, l1, a1), (_, l2, a2) = carry[h]
        o = a1 / l1 - lam * (a2 / l2)
        o = o * lax.rsqrt(jnp.mean(o * o, axis=1, keepdims=True) + SUBLN_EPS)
        o_ref[:, cols[h]] = (o * g_ref[...] * (1.0 - lam_init)).astype(o_ref.dtype)


def _diff_attention(qd, kv, lam_params, subln_g, lam_init, batch, seq):
    n, d = qd.shape
    heads = d // HEAD_DIM
    t = _tile(seq, 512)
    nq = seq // t
    hb = math.gcd(heads, 4)
    nh = heads // hb
    w = hb * HEAD_DIM
    return pl.pallas_call(
        functools.partial(_diff_attn_kernel, t=t, hb=hb, lam_init=lam_init),
        out_shape=jax.ShapeDtypeStruct((n, d), BF16),
        grid=(batch, nh, nq),
        in_specs=[pl.BlockSpec((t, w), lambda b, h, i: (b * nq + i, h)),
                  pl.BlockSpec((seq, w), lambda b, h, i: (b, h)),
                  pl.BlockSpec((seq, w), lambda b, h, i: (b, nh + h)),
                  pl.BlockSpec(lam_params.shape, lambda b, h, i: (0, 0)),
                  pl.BlockSpec((1, HEAD_DIM), lambda b, h, i: (0, 0))],
        out_specs=pl.BlockSpec((t, w), lambda b, h, i: (b * nq + i, h)),
        compiler_params=_params(("arbitrary", "arbitrary", "arbitrary")),
        name="diff_attn",
    )(qd, kv, kv, lam_params, subln_g.reshape(1, HEAD_DIM))


def _split_bf16(x):
    hi = x.astype(BF16)
    return hi, (x - hi.astype(F32)).astype(BF16)


def _route_rows(h, w_ref, info_ref, cnt_ref):
    tm = h.shape[0]
    i = pl.program_id(0)

    @pl.when(i == 0)
    def _():
        cnt_ref[...] = jnp.zeros_like(cnt_ref)

    h_hi, h_lo = _split_bf16(h)
    w_hi, w_lo = _split_bf16(w_ref[...])
    both_w = _dot(h_hi, jnp.concatenate([w_hi, w_lo], axis=1))
    logits = both_w[:, :LANES] + (both_w[:, LANES:] + _dot(h_lo, w_hi))

    lane = lax.broadcasted_iota(jnp.int32, (tm, LANES), 1).astype(F32)

    def first_max(mask, vals):
        top = jnp.max(jnp.where(mask, vals, NEG_BIG), axis=1, keepdims=True)
        idx = jnp.min(jnp.where(mask & (vals == top), lane, float(LANES)), axis=1, keepdims=True)
        return top, idx

    g_mask = lane < N_GROUPS
    g_top, g_sel = first_max(g_mask, logits)
    g_den = jnp.sum(jnp.where(g_mask, jnp.exp(logits - g_top), 0.0), axis=1, keepdims=True)
    g_w = 1.0 / g_den

    lo_lane = ROUTE_LANE0 + EXPERTS_PER_GROUP * g_sel
    in_group = (lane >= lo_lane) & (lane < lo_lane + EXPERTS_PER_GROUP)
    v1, i1 = first_max(in_group, logits)
    v2, i2 = first_max(in_group & (lane != i1), logits)
    ex = jnp.exp(v2 - v1)
    w1 = g_w / (1.0 + ex)
    w2 = w1 * ex

    m1 = (lane == i1).astype(F32)
    m2 = (lane == i2).astype(F32)
    both = m1 + m2
    r = lax.broadcasted_iota(jnp.int32, (tm, tm), 0)
    c = lax.broadcasted_iota(jnp.int32, (tm, tm), 1)
    earlier = (c < r).astype(BF16)
    before = _dot(earlier, both.astype(BF16)) + cnt_ref[0:1, :]
    rank1 = jnp.sum(m1 * before, axis=1, keepdims=True)
    rank2 = jnp.sum(m2 * before, axis=1, keepdims=True)
    cnt_ref[...] = cnt_ref[...] + jnp.sum(both, axis=0, keepdims=True)

    info = jnp.where(lane == 0, i1 - ROUTE_LANE0, 0.0)
    info = jnp.where(lane == 1, i2 - ROUTE_LANE0, info)
    info = jnp.where(lane == 2, rank1, info)
    info = jnp.where(lane == 3, rank2, info)
    info = jnp.where(lane == 4, w1, info)
    info = jnp.where(lane == 5, w2, info)
    info_ref[...] = info


def _route_kernel(h_ref, w_ref, info_ref, cnt_ref):
    _route_rows(h_ref[...], w_ref, info_ref, cnt_ref)


def _route(h, w_route):
    n, d = h.shape
    tm = _tile(n, 512)
    return pl.pallas_call(
        _route_kernel,
        out_shape=(jax.ShapeDtypeStruct((n, LANES), F32), jax.ShapeDtypeStruct((8, LANES), F32)),
        grid=(n // tm,),
        in_specs=[pl.BlockSpec((tm, d), lambda i: (i, 0)),
                  pl.BlockSpec((d, LANES), lambda i: (0, 0))],
        out_specs=(pl.BlockSpec((tm, LANES), lambda i: (i, 0)),
                   pl.BlockSpec((8, LANES), lambda i: (0, 0))),
        compiler_params=_params(("arbitrary",)),
        name="route",
    )(h, w_route)


def _token_rows(ref, token, rt, count=1):
    return ref.at[pl.ds(pl.multiple_of(token * rt, rt), rt * count), :]


def _dispatch_kernel(slot_ref, pad_ref, used_ref, h_ref, xs_hbm, zrow, sem, pad_sem, *, tb, tm, rt, n_tiles):
    i = pl.program_id(0)
    base = i * tb

    def row_copy(r, s):
        return pltpu.make_async_copy(_token_rows(h_ref, r, rt), _token_rows(xs_hbm, s, rt), sem)

    def issue(r, _):
        t = base + r
        row_copy(r, slot_ref[2 * t]).start()
        row_copy(r, slot_ref[2 * t + 1]).start()
        return 0

    lax.fori_loop(0, tb, issue, 0, unroll=DMA_ISSUE_UNROLL)

    def for_each_pad_block(act):
        for e in range(N_EXPERTS):
            first, length = pad_ref[e], pad_ref[N_EXPERTS + e]
            for bit in reversed(range(tm.bit_length() - 1)):
                size = 1 << bit
                slot = first + ((length >> (bit + 1)) << (bit + 1))

                @pl.when(((length >> bit) & 1) == 1)
                def _():
                    act(pltpu.make_async_copy(_token_rows(zrow, 0, rt, size), _token_rows(xs_hbm, slot, rt, size),
                                              pad_sem))

    @pl.when(i == 0)
    def _():
        zrow[...] = jnp.zeros_like(zrow)

        def tile_copy(t):
            return pltpu.make_async_copy(zrow, _token_rows(xs_hbm, t * tm, rt, tm), pad_sem)

        def issue_tile(t, _):
            tile_copy(t).start()
            return 0

        def drain_tile(t, _):
            tile_copy(t).wait()
            return 0

        for_each_pad_block(lambda copy: copy.start())
        lax.fori_loop(used_ref[0], n_tiles, issue_tile, 0)
        for_each_pad_block(lambda copy: copy.wait())
        lax.fori_loop(used_ref[0], n_tiles, drain_tile, 0)

    whole_tile = pltpu.make_async_copy(h_ref, _token_rows(xs_hbm, 0, rt, tb), sem)
    whole_tile.wait()
    whole_tile.wait()


def _dispatch(hp, slots, pads, tiles_used, n, tm):
    assert tm & (tm - 1) == 0
    rt = hp.shape[0] // n
    n_tiles = 2 * n // tm + N_EXPERTS
    tb = _tile(n, 1024)
    return pl.pallas_call(
        functools.partial(_dispatch_kernel, tb=tb, tm=tm, rt=rt, n_tiles=n_tiles),
        out_shape=jax.ShapeDtypeStruct((n_tiles * tm * rt, LANES), hp.dtype),
        grid_spec=pltpu.PrefetchScalarGridSpec(
            num_scalar_prefetch=3,
            grid=(n // tb,),
            in_specs=[pl.BlockSpec((tb * rt, LANES), lambda i, s, p, u: (i, 0))],
            out_specs=pl.BlockSpec(memory_space=pl.ANY),
            scratch_shapes=[pltpu.VMEM((tm * rt, LANES), hp.dtype), pltpu.SemaphoreType.DMA(()),
                            pltpu.SemaphoreType.DMA(())],
        ),
        compiler_params=_params(("arbitrary",)),
        name="moe_dispatch",
    )(slots, pads, tiles_used, hp)


def _moe_up_kernel(te_ref, fresh_ref, valid_ref, x_ref, wg_ref, wu_ref, o_ref, wg_bf, wu_bf):
    i = pl.program_id(0)
    tm = o_ref.shape[0]
    rt = x_ref.shape[0] // tm
    c = rt * LANES // 2

    @pl.when(fresh_ref[i] == 1)
    def _():
        wg_bf[...] = wg_ref[...].astype(BF16)
        wu_bf[...] = wu_ref[...].astype(BF16)

    @pl.when(valid_ref[i] == 1)
    def _():
        xa, xb = _unpack_bf16_pairs(_load_row_tiles(x_ref, tm, rt))
        xa, xb = xa.astype(BF16), xb.astype(BF16)
        g = _dot(xa, wg_bf[:c, :]) + _dot(xb, wg_bf[c:, :])
        u = _dot(xa, wu_bf[:c, :]) + _dot(xb, wu_bf[c:, :])
        o_ref[...] = (g * (1.0 / (1.0 + jnp.exp(-g))) * u).astype(o_ref.dtype)

    @pl.when(valid_ref[i] == 0)
    def _():
        o_ref[...] = jnp.zeros_like(o_ref)


def _tile_or_first(i, valid_ref):
    return jnp.where(valid_ref[i] == 1, i, 0)


def _moe_up(xs, w_gate, w_up, layer, tile_expert, fresh, valid, tm):
    _, _, d, f = w_gate.shape
    rt = d // LANES
    p = xs.shape[0] // rt
    w_spec = pl.BlockSpec((None, None, d, f), lambda i, te, fr, va: (layer, te[i], 0, 0))
    return pl.pallas_call(
        _moe_up_kernel,
        out_shape=jax.ShapeDtypeStruct((p, f), BF16),
        grid_spec=pltpu.PrefetchScalarGridSpec(
            num_scalar_prefetch=3,
            grid=(p // tm,),
            in_specs=[pl.BlockSpec((tm * rt, LANES), lambda i, te, fr, va: (_tile_or_first(i, va), 0)),
                      w_spec, w_spec],
            out_specs=pl.BlockSpec((tm, f), lambda i, te, fr, va: (i, 0)),
            scratch_shapes=[pltpu.VMEM((d, f), BF16), pltpu.VMEM((d, f), BF16)],
        ),
        compiler_params=_params(("arbitrary",)),
        name="moe_up",
    )(tile_expert, fresh, valid, xs, w_gate, w_up)


def _moe_down_kernel(te_ref, fresh_ref, valid_ref, a_ref, wd_ref, o_ref, wd_bf):
    i = pl.program_id(0)

    @pl.when(fresh_ref[i] == 1)
    def _():
        wd_bf[...] = wd_ref[...].astype(BF16)

    @pl.when(valid_ref[i] == 1)
    def _():
        _store_row_tiles(o_ref, _pack_bf16_pairs(_dot(a_ref[...], wd_bf[...])))

    @pl.when(valid_ref[i] == 0)
    def _():
        o_ref[...] = jnp.zeros_like(o_ref)


def _moe_down(act, w_down, layer, tile_expert, fresh, valid, tm):
    p, f = act.shape
    d = w_down.shape[3]
    rt = d // LANES
    return pl.pallas_call(
        _moe_down_kernel,
        out_shape=jax.ShapeDtypeStruct((p * rt, LANES), F32),
        grid_spec=pltpu.PrefetchScalarGridSpec(
            num_scalar_prefetch=3,
            grid=(p // tm,),
            in_specs=[pl.BlockSpec((tm, f), lambda i, te, fr, va: (_tile_or_first(i, va), 0)),
                      pl.BlockSpec((None, None, f, d), lambda i, te, fr, va: (layer, te[i], 0, 0))],
            out_specs=pl.BlockSpec((tm * rt, LANES), lambda i, te, fr, va: (i, 0)),
            scratch_shapes=[pltpu.VMEM((f, d), BF16)],
        ),
        compiler_params=_params(("arbitrary",)),
        name="moe_down",
    )(tile_expert, fresh, valid, act, w_down)


def _combine_ln_kernel(slot_ref, h_ref, info_ref, g_ref, b_ref, ys_hbm, o_ref, ob_ref, buf, sems, *, tm, rt, alpha):
    i = pl.program_id(0)

    def row_copy(s, par, r):
        return pltpu.make_async_copy(_token_rows(ys_hbm, s, rt), _token_rows(buf.at[par], r, rt), sems.at[par])

    def gather_tile(tile, par):
        def issue(r, _):
            t = tile * tm + r
            row_copy(slot_ref[2 * t], par, r).start()
            row_copy(slot_ref[2 * t + 1], par, tm + r).start()
            return 0
        lax.fori_loop(0, tm, issue, 0, unroll=DMA_ISSUE_UNROLL)

    @pl.when(i == 0)
    def _():
        gather_tile(0, 0)

    @pl.when(i + 1 < pl.num_programs(0))
    def _():
        gather_tile(i + 1, (i + 1) % 2)

    par = i % 2
    pltpu.make_async_copy(_token_rows(ys_hbm, 0, rt, 2 * tm), buf.at[par], sems.at[par]).wait()

    info = info_ref[...]
    w1, w2 = info[:, 4:5], info[:, 5:6]
    a1, b1 = _unpack_bf16_pairs(_load_row_tiles(buf.at[par], tm, rt))
    a2, b2 = _unpack_bf16_pairs(_load_row_tiles(buf.at[par], tm, rt, first=tm))
    ffn = jnp.concatenate([w1 * a1 + w2 * a2, w1 * b1 + w2 * b2], axis=1)
    y = _layer_norm_rows(alpha * h_ref[...] + ffn, g_ref[...], b_ref[...])
    o_ref[...] = y
    ob_ref[...] = y.astype(BF16)


def _combine_ln(h, info, ys, slots, g, b, alpha):
    n, d = h.shape
    tm = _tile(n, 256)
    rt = d // LANES
    return pl.pallas_call(
        functools.partial(_combine_ln_kernel, tm=tm, rt=rt, alpha=alpha),
        out_shape=(jax.ShapeDtypeStruct((n, d), F32), jax.ShapeDtypeStruct((n, d), BF16)),
        grid_spec=pltpu.PrefetchScalarGridSpec(
            num_scalar_prefetch=1,
            grid=(n // tm,),
            in_specs=[pl.BlockSpec((tm, d), lambda i, s: (i, 0)),
                      pl.BlockSpec((tm, LANES), lambda i, s: (i, 0)),
                      pl.BlockSpec((1, d), lambda i, s: (0, 0)),
                      pl.BlockSpec((1, d), lambda i, s: (0, 0)),
                      pl.BlockSpec(memory_space=pl.ANY)],
            out_specs=(pl.BlockSpec((tm, d), lambda i, s: (i, 0)),
                       pl.BlockSpec((tm, d), lambda i, s: (i, 0))),
            scratch_shapes=[pltpu.VMEM((2, 2 * tm * rt, LANES), F32), pltpu.SemaphoreType.DMA((2,))],
        ),
        compiler_params=_params(("arbitrary",)),
        name="moe_combine_ln",
    )(slots, h, info, g.reshape(1, d), b.reshape(1, d), ys)


def _route_weights(w_group, w_inner):
    d = w_group.shape[0]
    return jnp.concatenate(
        [w_group, w_inner.transpose(1, 0, 2).reshape(d, N_EXPERTS),
         jnp.zeros((d, LANES - N_GROUPS - N_EXPERTS), F32)], axis=1)


def _moe_ln(h, hp, info, cnt, layer, w_gate, w_up, w_down, ln_g, ln_b, alpha):
    n, d = h.shape
    tm = _tile(2 * n // N_EXPERTS, 512)
    n_rows = 2 * n + N_EXPERTS * tm
    n_tiles = n_rows // tm

    expert = info[:, 0:2].astype(jnp.int32)
    rank = info[:, 2:4].astype(jnp.int32)
    counts = cnt[0, ROUTE_LANE0:ROUTE_LANE0 + N_EXPERTS].astype(jnp.int32)
    padded = (counts + tm - 1) // tm * tm
    ends = jnp.cumsum(padded)
    starts = ends - padded
    slots = (starts[expert] + rank).reshape(-1)
    pads = jnp.concatenate([starts + counts, padded - counts])
    tile_ids = jnp.arange(n_tiles, dtype=jnp.int32)
    tile_expert = jnp.minimum(jnp.sum((tile_ids[:, None] >= (ends // tm)[None, :]).astype(jnp.int32), axis=1),
                              N_EXPERTS - 1)
    valid = (tile_ids < ends[-1] // tm).astype(jnp.int32)
    fresh = jnp.concatenate([jnp.ones((1,), jnp.int32),
                             (tile_expert[1:] != tile_expert[:-1]).astype(jnp.int32)])

    xs = _dispatch(hp, slots, pads, (ends[-1:] // tm).astype(jnp.int32), n, tm)
    act = _moe_up(xs, w_gate, w_up, layer, tile_expert, fresh, valid, tm)
    ys = _moe_down(act, w_down, layer, tile_expert, fresh, valid, tm)
    return _combine_ln(h, info, ys, slots, ln_g, ln_b, alpha)


def _rope_tables(seq):
    half = DIFF_QK_DIM // 2
    lane = jnp.arange(LANES)
    inv_freq = ROPE_THETA ** (-(lane % half).astype(F32) / half)
    ang = jnp.arange(seq, dtype=F32)[:, None] * inv_freq[None, :]
    sign = jnp.where((lane % DIFF_QK_DIM) < half, -1.0, 1.0)
    return jnp.cos(ang), jnp.sin(ang) * sign


def kernel(x, sb_w_qkv, sb_w_o, shared_w_kv, diff_w_q, diff_w_o, diff_lambda, diff_subln_g, ln_mix_g, ln_mix_b,
           ln_ffn_g, ln_ffn_b, moe_w_group, moe_w_inner, moe_w_gate, moe_w_up, moe_w_down):
    batch, seq, d = x.shape
    n = batch * seq
    depth = ln_mix_g.shape[0]
    n_sb = sb_w_qkv.shape[0]
    alpha = (2.0 * depth) ** 0.25
    rope_tabs = _rope_tables(seq)

    h = x.reshape(n, d)
    hb = h.astype(BF16)
    kv = None
    for layer in range(depth):
        if layer < n_sb:
            qkv = _proj(hb, sb_w_qkv, layer, scale=HEAD_DIM ** -0.5 * LOG2_E, scale_cols=d)
            o = _sb_attention(qkv, batch, seq)
            w_o = sb_w_o[layer]
        else:
            i = layer - n_sb
            if kv is None:
                kv = _proj(hb, shared_w_kv, rope_cols=d, rope_tabs=rope_tabs, seq=seq)
            qd = _proj(hb, diff_w_q, i, scale=DIFF_QK_DIM ** -0.5 * LOG2_E, scale_cols=d,
                       rope_cols=d, rope_tabs=rope_tabs, seq=seq)
            lam_init = 0.8 - 0.6 * math.exp(-0.3 * layer)
            o = _diff_attention(qd, kv, diff_lambda[i], diff_subln_g[i], lam_init, batch, seq)
            w_o = diff_w_o[i]
        h, hp = _proj_ln(o, w_o.astype(BF16), h, ln_mix_g[layer], ln_mix_b[layer], alpha)
        info, cnt = _route(h, _route_weights(moe_w_group[layer], moe_w_inner[layer]))
        h, hb = _moe_ln(h, hp, info, cnt, layer, moe_w_gate, moe_w_up, moe_w_down,
                        ln_ffn_g[layer], ln_ffn_b[layer], alpha)
    return h.reshape(batch, seq, d)
```
